```python
import math
import jax
import jax.numpy as jnp
from jax import lax
import numpy as np

D_MODEL = 1024
BATCH = 2
SEQ = 8192
DEPTH = 2
DEC_BATCH = 128
DEC_SEQ = 4
PAST_LEN = 2048
PAGE_SIZE = 128

HEAD_DIM = 64
SB_WIDTH = D_MODEL // 2
SB_HEADS = SB_WIDTH // HEAD_DIM
MOBA_WIDTH = D_MODEL // 2
MOBA_HEADS = MOBA_WIDTH // HEAD_DIM
MOBA_BLOCK = 256
MOBA_TOPK = 3
SSM_INNER = D_MODEL
SSM_HEAD_DIM = 64
SSM_HEADS = SSM_INNER // SSM_HEAD_DIM
SSM_GROUPS = 4
SSM_STATE = 128
CONV_WIDTH = 4
CONV_DIM = SSM_INNER + 2 * SSM_GROUPS * SSM_STATE
SSD_CHUNK = 128
Q_BLOCK = 128
REL_BUCKETS = 32
REL_MAX_DIST = 128
N_BRANCHES = 3
N_EXPERT_GROUPS = 4
EXPERTS_PER_GROUP = 8
N_EXPERTS = N_EXPERT_GROUPS * EXPERTS_PER_GROUP
EXPERT_TOPK = 2
EXPERT_FF = D_MODEL // 4
NORM_EPS = 1e-6
IN_SIZES = (SB_WIDTH, SB_WIDTH, SB_WIDTH, MOBA_WIDTH, MOBA_WIDTH, MOBA_WIDTH,
            SSM_INNER, CONV_DIM, SSM_HEADS, N_BRANCHES * D_MODEL)
IN_COLS = sum(IN_SIZES)
IN_OFFSETS = tuple(int(o) for o in np.cumsum(IN_SIZES)[:-1])

kernel_name = 'hybrid_sb_moba_ssd_hmoe_step'


def rms_norm(x, g):
    x32 = x.astype(jnp.float32)
    y = x32 * lax.rsqrt(jnp.mean(jnp.square(x32), axis=-1, keepdims=True) + NORM_EPS)
    return (y * g.astype(jnp.float32)).astype(x.dtype)


def rel_bucket(dist):
    exact = REL_BUCKETS // 2
    d = jnp.maximum(dist, 0)
    df = jnp.maximum(d, 1).astype(jnp.float32)
    large = exact + (jnp.log(df / exact) / math.log(REL_MAX_DIST / exact)
                     * (REL_BUCKETS - exact)).astype(jnp.int32)
    return jnp.where(d < exact, d, jnp.minimum(large, REL_BUCKETS - 1))


def stick_breaking_attention(q, k, v, start):
    bsz, t_len, n_h, hd = q.shape
    qb = min(Q_BLOCK, t_len)
    n_q = t_len // qb
    k32 = k.astype(jnp.float32)
    v32 = v.astype(jnp.float32)
    k_pos = jnp.arange(k.shape[1])
    q_blocks = jnp.swapaxes(q.astype(jnp.float32).reshape(bsz, n_q, qb, n_h, hd), 0, 1)
    scale = hd ** -0.5

    def one_block(args):
        qi, i = args
        q_pos = start + i * qb + jnp.arange(qb)
        z = jnp.einsum('bqhd,bkhd->bhqk', qi, k32) * scale
        past = k_pos[None, :] < q_pos[:, None]
        log_keep = jnp.where(past, jax.nn.log_sigmoid(-z), 0.0)
        tail = lax.cumsum(log_keep, axis=3, reverse=True) - log_keep
        w = jnp.where(past, jnp.exp(jax.nn.log_sigmoid(z) + tail), 0.0)
        return jnp.einsum('bhqk,bkhd->bqhd', w, v32)

    out = lax.map(one_block, (q_blocks, jnp.arange(n_q)))
    return jnp.swapaxes(out, 0, 1).reshape(bsz, t_len, n_h, hd)


def moba_attention(q, k, v, start, rel_bias):
    bsz, t_len, n_h, hd = q.shape
    k_len = k.shape[1]
    n_blk = max(-(-k_len // MOBA_BLOCK), MOBA_TOPK)
    pad = ((0, 0), (0, n_blk * MOBA_BLOCK - k_len), (0, 0), (0, 0))
    kb = jnp.pad(k, pad).reshape(bsz, n_blk, MOBA_BLOCK, n_h, hd)
    vb = jnp.pad(v, pad).reshape(bsz, n_blk, MOBA_BLOCK, n_h, hd)
    k_mean = jnp.mean(kb, axis=2, dtype=jnp.float32)
    qb = min(Q_BLOCK, t_len)
    n_q = t_len // qb
    q_blocks = jnp.swapaxes(q.astype(jnp.float32).reshape(bsz, n_q, qb, n_h, hd), 0, 1)
    b_idx = jnp.arange(bsz)[:, None, None]
    h_idx = jnp.arange(n_h)[None, :, None]
    offs = jnp.arange(MOBA_BLOCK)
    scale = hd ** -0.5

    def one_block(args):
        qi, i = args
        q0 = start + i * qb
        q_pos = q0 + jnp.arange(qb)
        q_blk = q_pos // MOBA_BLOCK
        own = q0 // MOBA_BLOCK
        route = jnp.einsum('bqhd,bnhd->bhqn', qi, k_mean)
        route = jnp.where(jnp.arange(n_blk)[None, :] < q_blk[:, None], route, -jnp.inf)
        _, sel = lax.top_k(route, MOBA_TOPK)
        sel_ok = jnp.arange(MOBA_TOPK)[None, :] < q_blk[:, None]
        logits = []
        for j in range(MOBA_TOPK):
            kj = kb[b_idx, sel[..., j], :, h_idx, :].astype(jnp.float32)
            dist = q_pos[:, None] - (sel[..., j, None] * MOBA_BLOCK + offs)
            lj = (jnp.einsum('bqhd,bhqsd->bhqs', qi, kj) * scale
                  + rel_bias[rel_bucket(dist), h_idx[..., None]])
            logits.append(jnp.where(sel_ok[:, j, None], lj, -jnp.inf))
        k_own = lax.dynamic_index_in_dim(kb, own, axis=1, keepdims=False).astype(jnp.float32)
        v_own = lax.dynamic_index_in_dim(vb, own, axis=1, keepdims=False).astype(jnp.float32)
        kpos_own = own * MOBA_BLOCK + offs
        bias_own = jnp.moveaxis(rel_bias[rel_bucket(q_pos[:, None] - kpos_own[None, :])], -1, 0)
        l_own = jnp.einsum('bqhd,bshd->bhqs', qi, k_own) * scale + bias_own
        logits.append(jnp.where(kpos_own[None, :] <= q_pos[:, None], l_own, -jnp.inf))
        probs = jax.nn.softmax(jnp.stack(logits, axis=3), axis=(3, 4))
        out = jnp.einsum('bhqs,bshd->bqhd', probs[:, :, :, MOBA_TOPK], v_own)
        for j in range(MOBA_TOPK):
            vj = vb[b_idx, sel[..., j], :, h_idx, :].astype(jnp.float32)
            out = out + jnp.einsum('bhqs,bhqsd->bqhd', probs[:, :, :, j], vj)
        return out

    out = lax.map(one_block, (q_blocks, jnp.arange(n_q)))
    return jnp.swapaxes(out, 0, 1).reshape(bsz, t_len, n_h, hd)


def causal_conv(xbc, buf, w, b):
    t_len = xbc.shape[1]
    full = jnp.concatenate([buf.astype(xbc.dtype), xbc], axis=1)
    y = b
    for j in range(CONV_WIDTH):
        y = y + full[:, j:j + t_len] * w[j]
    return y, full[:, t_len:]


def ssd_scan(x, dt, a_log, b_in, c_in, h0):
    f32 = jnp.float32
    bsz, t_len, n_h, hp = x.shape
    n_g, n_s = b_in.shape[2], b_in.shape[3]
    e = n_h // n_g
    q = min(SSD_CHUNK, t_len)
    n_c = t_len // q
    a = dt * -jnp.exp(a_log.astype(f32))
    xr = (x.astype(f32) * dt[..., None]).reshape(bsz, n_c, q, n_g, e, hp)
    ar = a.reshape(bsz, n_c, q, n_g, e)
    br = b_in.astype(f32).reshape(bsz, n_c, q, n_g, n_s)
    cr = c_in.astype(f32).reshape(bsz, n_c, q, n_g, n_s)
    acs = jnp.cumsum(ar, axis=2)
    causal = jnp.tril(jnp.ones((q, q), bool))
    seg = acs[:, :, :, None] - acs[:, :, None, :]
    decay = jnp.exp(jnp.where(causal[:, :, None, None], seg, -jnp.inf))
    cb = jnp.einsum('bctgn,bcsgn->bctsg', cr, br)
    y_diag = jnp.einsum('bctsge,bcsgep->bctgep', cb[..., None] * decay, xr)
    decay_out = jnp.exp(acs[:, :, -1:] - acs)
    states = jnp.einsum('bcsgn,bcsgep->bcgepn', br, xr * decay_out[..., None])
    chunk_decay = jnp.exp(acs[:, :, -1])

    def step(h, inp):
        st, dec = inp
        return h * dec[..., None, None] + st, h

    h_last, h_in = lax.scan(step, h0.astype(f32).reshape(bsz, n_g, e, hp, n_s),
                            (jnp.moveaxis(states, 1, 0), jnp.moveaxis(chunk_decay, 1, 0)))
    h_in = jnp.moveaxis(h_in, 0, 1)
    y_off = jnp.einsum('bctgn,bcgepn->bctgep', cr, h_in) * jnp.exp(acs)[..., None]
    y = (y_diag + y_off).reshape(bsz, t_len, n_h, hp)
    return y, h_last.reshape(bsz, n_h, hp, n_s)


def token_mixers(h, past, lp, rel_bias):
    pk_sb, pv_sb, pk_mb, pv_mb, ssm_h0, conv_buf = past
    bsz, t_len, _ = h.shape
    start = pk_sb.shape[1]
    u = h @ lp['w_in']
    q_sb, k_sb, v_sb, q_mb, k_mb, v_mb, z, xbc, dt_raw, gate_logits = jnp.split(u, IN_OFFSETS, axis=-1)
    k_sb = k_sb.reshape(bsz, t_len, SB_HEADS, HEAD_DIM)
    v_sb = v_sb.reshape(bsz, t_len, SB_HEADS, HEAD_DIM)
    k_mb = k_mb.reshape(bsz, t_len, MOBA_HEADS, HEAD_DIM)
    v_mb = v_mb.reshape(bsz, t_len, MOBA_HEADS, HEAD_DIM)
    o_sb = stick_breaking_attention(q_sb.reshape(bsz, t_len, SB_HEADS, HEAD_DIM),
                                    jnp.concatenate([pk_sb, k_sb], axis=1),
                                    jnp.concatenate([pv_sb, v_sb], axis=1), start)
    o_mb = moba_attention(q_mb.reshape(bsz, t_len, MOBA_HEADS, HEAD_DIM),
                          jnp.concatenate([pk_mb, k_mb], axis=1),
                          jnp.concatenate([pv_mb, v_mb], axis=1), start, rel_bias)
    conv_out, conv_new = causal_conv(xbc, conv_buf, lp['conv_w'], lp['conv_b'])
    xbc = jax.nn.silu(conv_out)
    xs, bs, cs = jnp.split(xbc, [SSM_INNER, SSM_INNER + SSM_GROUPS * SSM_STATE], axis=-1)
    dt = jax.nn.softplus(dt_raw.astype(jnp.float32) + lp['dt_bias'].astype(jnp.float32))
    xs_h = xs.reshape(bsz, t_len, SSM_HEADS, SSM_HEAD_DIM)
    y, ssm_new = ssd_scan(xs_h, dt, lp['a_log'],
                          bs.reshape(bsz, t_len, SSM_GROUPS, SSM_STATE),
                          cs.reshape(bsz, t_len, SSM_GROUPS, SSM_STATE), ssm_h0)
    y = y + xs_h.astype(jnp.float32) * lp['d_skip'].astype(jnp.float32)[:, None]
    y = y.reshape(bsz, t_len, SSM_INNER) * jax.nn.silu(z.astype(jnp.float32))
    y = rms_norm(y, lp['g_ssm']).astype(h.dtype)
    gates = jax.nn.sigmoid(gate_logits.astype(jnp.float32)).reshape(bsz, t_len, N_BRANCHES, D_MODEL)
    br_sb = o_sb.reshape(bsz, t_len, SB_WIDTH).astype(h.dtype) @ lp['w_branch_sb']
    br_mb = o_mb.reshape(bsz, t_len, MOBA_WIDTH).astype(h.dtype) @ lp['w_branch_moba']
    br_ssm = y @ lp['w_branch_ssm']
    merged = gates[:, :, 0] * br_sb + gates[:, :, 1] * br_mb + gates[:, :, 2] * br_ssm
    out = merged.astype(h.dtype) @ lp['w_out']
    return out, (k_sb, v_sb, k_mb, v_mb, ssm_new, conv_new)


def hier_moe(h, lp):
    f32 = jnp.float32
    bsz, t_len, d = h.shape
    tok = h.reshape(-1, d)
    g_logit = (tok @ lp['w_router_group']).astype(f32) + lp['b_router_group'].astype(f32)
    grp = jnp.argmax(g_logit, axis=-1)
    p_grp = jnp.take_along_axis(jax.nn.softmax(g_logit, axis=-1), grp[:, None], axis=1)
    e_logit = ((tok @ lp['w_router_expert']).astype(f32) + lp['b_router_expert'].astype(f32)
               ).reshape(-1, N_EXPERT_GROUPS, EXPERTS_PER_GROUP)
    e_logit = jnp.take_along_axis(e_logit, grp[:, None, None], axis=1)[:, 0]
    top_val, top_idx = lax.top_k(e_logit, EXPERT_TOPK)
    w_top = jax.nn.softmax(top_val, axis=-1) * p_grp
    expert_id = grp[:, None] * EXPERTS_PER_GROUP + top_idx
    combine = jnp.sum(jax.nn.one_hot(expert_id, N_EXPERTS, dtype=f32) * w_top[..., None], axis=1)
    out = jnp.zeros(tok.shape, f32)
    for e in range(N_EXPERTS):
        hid = jax.nn.silu(tok @ lp['w_expert_gate'][e]) * (tok @ lp['w_expert_up'][e])
        out = out + combine[:, e:e + 1] * (hid @ lp['w_expert_down'][e]).astype(f32)
    return out.reshape(bsz, t_len, d).astype(h.dtype)


def trunk_layer(x, past, lp, rel_bias):
    mix, new_state = token_mixers(rms_norm(x, lp['g_mix']), past, lp, rel_bias)
    x = x + mix
    x = x + hier_moe(rms_norm(x, lp['g_ffn']), lp)
    return x, new_state


def gather_pages(pool, page_table):
    g = pool[page_table]
    return g.reshape(page_table.shape[0], -1, pool.shape[2], pool.shape[3])


def stack_layers(states):
    return tuple(jnp.stack([s[i] for s in states]) for i in range(6))


def setup_inputs(seed: int = 0) -> dict:
    key = jax.random.key(seed)
    ks = jax.random.split(key, 32)
    f32 = jnp.float32

    def nrm(i, shape, scale):
        return jax.random.normal(ks[i], shape, f32) * scale

    n_pages = PAST_LEN // PAGE_SIZE
    n_used = DEC_BATCH * n_pages
    n_pool = n_used + max(1, n_used // 4)
    page_table = jax.random.permutation(ks[6], n_pool)[:n_used].reshape(DEC_BATCH, n_pages).astype(jnp.int32)
    dt0 = jnp.exp(jax.random.uniform(ks[14], (DEPTH, SSM_HEADS), f32, math.log(1e-3), math.log(1e-1)))
    dt_bias = dt0 + jnp.log(-jnp.expm1(-dt0))
    a_log = jnp.log(jax.random.uniform(ks[15], (DEPTH, SSM_HEADS), f32, 1.0, 16.0))
    return {
        'x_prompt': nrm(0, (BATCH, SEQ, D_MODEL), 1.0),
        'x_sample': nrm(1, (DEC_BATCH, DEC_SEQ, D_MODEL), 1.0),
        'cache_k_sb': nrm(2, (DEPTH, n_pool, PAGE_SIZE, SB_HEADS, HEAD_DIM), 1.0),
        'cache_v_sb': nrm(3, (DEPTH, n_pool, PAGE_SIZE, SB_HEADS, HEAD_DIM), 1.0),
        'cache_k_moba': nrm(4, (DEPTH, n_pool, PAGE_SIZE, MOBA_HEADS, HEAD_DIM), 1.0),
        'cache_v_moba': nrm(5, (DEPTH, n_pool, PAGE_SIZE, MOBA_HEADS, HEAD_DIM), 1.0),
        'page_table': page_table,
        'state_ssm': nrm(7, (DEPTH, DEC_BATCH, SSM_HEADS, SSM_HEAD_DIM, SSM_STATE), 0.5),
        'state_conv': nrm(8, (DEPTH, DEC_BATCH, CONV_WIDTH - 1, CONV_DIM), 1.0),
        'rel_bias': nrm(9, (REL_BUCKETS, MOBA_HEADS), 0.5),
        'g_mix': 1.0 + nrm(10, (DEPTH, D_MODEL), 0.01),
        'w_in': nrm(11, (DEPTH, D_MODEL, IN_COLS), D_MODEL ** -0.5),
        'conv_w': nrm(12, (DEPTH, CONV_WIDTH, CONV_DIM), CONV_WIDTH ** -0.5),
        'conv_b': nrm(13, (DEPTH, CONV_DIM), 0.01),
        'dt_bias': dt_bias,
        'a_log': a_log,
        'd_skip': 1.0 + nrm(16, (DEPTH, SSM_HEADS), 0.1),
        'g_ssm': 1.0 + nrm(17, (DEPTH, SSM_INNER), 0.01),
        'w_branch_sb': nrm(18, (DEPTH, SB_WIDTH, D_MODEL), SB_WIDTH ** -0.5),
        'w_branch_moba': nrm(19, (DEPTH, MOBA_WIDTH, D_MODEL), MOBA_WIDTH ** -0.5),
        'w_branch_ssm': nrm(20, (DEPTH, SSM_INNER, D_MODEL), SSM_INNER ** -0.5),
        'w_out': nrm(21, (DEPTH, D_MODEL, D_MODEL), D_MODEL ** -0.5),
        'g_ffn': 1.0 + nrm(22, (DEPTH, D_MODEL), 0.01),
        'w_router_group': nrm(23, (DEPTH, D_MODEL, N_EXPERT_GROUPS), D_MODEL ** -0.5),
        'b_router_group': nrm(24, (DEPTH, N_EXPERT_GROUPS), 0.01),
        'w_router_expert': nrm(25, (DEPTH, D_MODEL, N_EXPERTS), D_MODEL ** -0.5),
        'b_router_expert': nrm(26, (DEPTH, N_EXPERTS), 0.01),
        'w_expert_gate': nrm(27, (DEPTH, N_EXPERTS, D_MODEL, EXPERT_FF), D_MODEL ** -0.5),
        'w_expert_up': nrm(28, (DEPTH, N_EXPERTS, D_MODEL, EXPERT_FF), D_MODEL ** -0.5),
        'w_expert_down': nrm(29, (DEPTH, N_EXPERTS, EXPERT_FF, D_MODEL), EXPERT_FF ** -0.5),
        'g_final': 1.0 + nrm(30, (D_MODEL,), 0.01),
    }


def reference(x_prompt, x_sample, cache_k_sb, cache_v_sb, cache_k_moba, cache_v_moba, page_table,
              state_ssm, state_conv, rel_bias, g_mix, w_in, conv_w, conv_b, dt_bias, a_log, d_skip,
              g_ssm, w_branch_sb, w_branch_moba, w_branch_ssm, w_out, g_ffn, w_router_group,
              b_router_group, w_router_expert, b_router_expert, w_expert_gate, w_expert_up,
              w_expert_down, g_final):
    bsz = x_prompt.shape[0]
    act = x_prompt.dtype
    hp, hs = x_prompt, x_sample
    new_p, new_s = [], []
    for l in range(DEPTH):
        lp = {'g_mix': g_mix[l], 'w_in': w_in[l], 'conv_w': conv_w[l], 'conv_b': conv_b[l],
              'dt_bias': dt_bias[l], 'a_log': a_log[l], 'd_skip': d_skip[l], 'g_ssm': g_ssm[l],
              'w_branch_sb': w_branch_sb[l], 'w_branch_moba': w_branch_moba[l],
              'w_branch_ssm': w_branch_ssm[l], 'w_out': w_out[l], 'g_ffn': g_ffn[l],
              'w_router_group': w_router_group[l], 'b_router_group': b_router_group[l],
              'w_router_expert': w_router_expert[l], 'b_router_expert': b_router_expert[l],
              'w_expert_gate': w_expert_gate[l], 'w_expert_up': w_expert_up[l],
              'w_expert_down': w_expert_down[l]}
        past_p = (jnp.zeros((bsz, 0, SB_HEADS, HEAD_DIM), act),
                  jnp.zeros((bsz, 0, SB_HEADS, HEAD_DIM), act),
                  jnp.zeros((bsz, 0, MOBA_HEADS, HEAD_DIM), act),
                  jnp.zeros((bsz, 0, MOBA_HEADS, HEAD_DIM), act),
                  jnp.zeros((bsz, SSM_HEADS, SSM_HEAD_DIM, SSM_STATE), jnp.float32),
                  jnp.zeros((bsz, CONV_WIDTH - 1, CONV_DIM), act))
        past_s = (gather_pages(cache_k_sb[l], page_table), gather_pages(cache_v_sb[l], page_table),
                  gather_pages(cache_k_moba[l], page_table), gather_pages(cache_v_moba[l], page_table),
                  state_ssm[l], state_conv[l])
        hp, st_p = trunk_layer(hp, past_p, lp, rel_bias)
        hs, st_s = trunk_layer(hs, past_s, lp, rel_bias)
        new_p.append(st_p)
        new_s.append(st_s)
    y_prompt = rms_norm(hp, g_final)
    y_sample = rms_norm(hs, g_final)
    k_sb_p, v_sb_p, k_moba_p, v_moba_p, ssm_p, conv_p = stack_layers(new_p)
    k_sb_s, v_sb_s, k_moba_s, v_moba_s, ssm_s, conv_s = stack_layers(new_s)
    return (y_prompt, y_sample, k_sb_p, v_sb_p, k_moba_p, v_moba_p, ssm_p, conv_p,
            k_sb_s, v_sb_s, k_moba_s, v_moba_s, ssm_s, conv_s)
```

```python
import functools
import math

import numpy as np
import jax
import jax.numpy as jnp
from jax import lax
from jax.experimental import pallas as pl
from jax.experimental.pallas import tpu as pltpu

F32 = jnp.float32
BF16 = jnp.bfloat16

LANES = 128
SUBLANES = 8
VMEM_LIMIT = 56 * 1024 * 1024

D_MODEL = 1024
HEAD_DIM = 64
N_HEADS = 8
ATT_WIDTH = N_HEADS * HEAD_DIM
MOBA_BLOCK = 256
MOBA_TOPK = 3
PAGE = 128
SSM_HEADS = 16
SSM_P = 64
SSM_GROUPS = 4
SSM_N = 128
SSM_INNER = SSM_HEADS * SSM_P
CONV_W = 4
CONV_DIM = SSM_INNER + 2 * SSM_GROUPS * SSM_N
SSD_CHUNK = 128
REL_BUCKETS = 32
REL_MAX_DIST = 128
N_GROUPS = 4
EPG = 8
N_EXPERTS = N_GROUPS * EPG
EXPERT_FF = 256
EPS = 1e-6
EXP_ZERO = -104.0

CB_Q_SB, CB_K_SB, CB_V_SB = 0, 4, 8
CB_Q_MB, CB_K_MB, CB_V_MB = 12, 16, 20
CB_Z = 24
CB_XBC = 32
CB_GATE = 48
U_COLS = 72 * LANES


def _cparams(sem):
    return pltpu.CompilerParams(dimension_semantics=sem, vmem_limit_bytes=VMEM_LIMIT)


def _split2(a):
    hi = a.astype(BF16)
    lo = (a - hi.astype(F32)).astype(BF16)
    return hi, lo


def _dot(a, b):
    return jnp.dot(a, b, preferred_element_type=F32)


def _dot_nt(a, b):
    return lax.dot_general(a, b, (((1,), (1,)), ((), ())), preferred_element_type=F32)


def _dot2(a, b_exact):
    hi, lo = _split2(a)
    return _dot(hi, b_exact) + _dot(lo, b_exact)


def _dot2_nt(a_exact, b):
    hi, lo = _split2(b)
    return _dot_nt(a_exact, hi) + _dot_nt(a_exact, lo)


def _dot3_nt(a, b):
    ah, al = _split2(a)
    bh, bl = _split2(b)
    return _dot_nt(ah, bh) + _dot_nt(ah, bl) + _dot_nt(al, bh)


def _sigmoid(x):
    return 1.0 / (1.0 + jnp.exp(-x))


def _softplus(x):
    return jnp.maximum(x, 0.0) + jnp.log(1.0 + jnp.exp(-jnp.abs(x)))


def _rms(x, g):
    ms = jnp.mean(x * x, axis=-1, keepdims=True)
    return x * lax.rsqrt(ms + EPS) * g


def _in_proj_body(x_ref, g_ref, w_ref, wdt_ref, u_ref, dt_ref, xn_ref):
    @pl.when(pl.program_id(1) == 0)
    def _():
        xn = _rms(x_ref[...], g_ref[...]).astype(BF16)
        xn_ref[...] = xn
        dt_ref[...] = _dot(xn, wdt_ref[...])

    u_ref[...] = _dot(xn_ref[...], w_ref[...])


def in_proj(x, g, w, wdt, tm=512, tn=1024):
    n, d = x.shape
    cols = w.shape[1]
    return pl.pallas_call(
        _in_proj_body,
        grid=(n // tm, cols // tn),
        in_specs=[
            pl.BlockSpec((tm, d), lambda i, j: (i, 0)),
            pl.BlockSpec((1, d), lambda i, j: (0, 0)),
            pl.BlockSpec((d, tn), lambda i, j: (0, j)),
            pl.BlockSpec((d, LANES), lambda i, j: (0, 0)),
        ],
        out_specs=[
            pl.BlockSpec((tm, tn), lambda i, j: (i, j)),
            pl.BlockSpec((tm, LANES), lambda i, j: (i, 0)),
        ],
        out_shape=[jax.ShapeDtypeStruct((n, cols), F32),
                   jax.ShapeDtypeStruct((n, LANES), F32)],
        scratch_shapes=[pltpu.VMEM((tm, d), BF16)],
        compiler_params=_cparams(("parallel", "arbitrary")),
        name="in_proj",
    )(x, g, w, wdt)


def _sb_tile(qh, k, v, carry, tri, mask):
    z = _dot_nt(qh, k)
    lk = -_softplus(z)
    if mask is not None:
        lk = jnp.where(mask, lk, 0.0)
    incl = _dot2(lk, tri)
    w = jnp.exp(z + incl + carry)
    if mask is not None:
        w = jnp.where(mask, w, 0.0)
    return _dot(w.astype(BF16), v), carry + incl[:, 0:1]


def _sb_prompt_body(q_ref, k_ref, v_ref, tri_ref, o_ref, acc_ref, car_ref, *, tq):
    i = pl.program_id(2)
    scale = HEAD_DIM ** -0.5
    q = q_ref[...] * scale
    lane = lax.broadcasted_iota(jnp.int32, q.shape, 1)
    qh = [jnp.where((lane < HEAD_DIM) == (h == 0), q, 0.0).astype(BF16) for h in range(2)]
    tri = tri_ref[...]
    row = lax.broadcasted_iota(jnp.int32, (tq, tq), 0)
    col = lax.broadcasted_iota(jnp.int32, (tq, tq), 1)
    past = col < row

    k0 = pl.multiple_of(i * tq, tq)
    kd = k_ref[pl.ds(k0, tq), :].astype(BF16)
    vd = v_ref[pl.ds(k0, tq), :].astype(BF16)
    zero = jnp.zeros((tq, 1), F32)
    for h in range(2):
        o, c = _sb_tile(qh[h], kd, vd, zero, tri, past)
        acc_ref[h] = o
        car_ref[h] = jnp.broadcast_to(c, (tq, LANES))

    def cond(j):
        alive = jnp.max(jnp.maximum(car_ref[0], car_ref[1])) >= EXP_ZERO
        return jnp.logical_and(j >= 0, alive)

    def body(j):
        kj = pl.multiple_of(j * tq, tq)
        kt = k_ref[pl.ds(kj, tq), :].astype(BF16)
        vt = v_ref[pl.ds(kj, tq), :].astype(BF16)
        for h in range(2):
            o, c = _sb_tile(qh[h], kt, vt, car_ref[h][:, 0:1], tri, None)
            acc_ref[h] = acc_ref[h] + o
            car_ref[h] = jnp.broadcast_to(c, (tq, LANES))
        return j - 1

    lax.while_loop(cond, body, i - 1)
    o_ref[...] = jnp.where(lane < HEAD_DIM, acc_ref[0], acc_ref[1]).astype(o_ref.dtype)


def _tri_ge(n):
    return jnp.asarray(np.tril(np.ones((n, n), np.float32)), BF16)


def sb_prompt(qa, ka, va, cbq, cbk, cbv, bsz, t_len, tq=256):
    n_q = t_len // tq
    n_hp = N_HEADS // 2
    return pl.pallas_call(
        functools.partial(_sb_prompt_body, tq=tq),
        grid=(bsz, n_hp, n_q),
        in_specs=[
            pl.BlockSpec((tq, LANES), lambda b, p, i: (b * n_q + i, cbq + p)),
            pl.BlockSpec((t_len, LANES), lambda b, p, i: (b, cbk + p)),
            pl.BlockSpec((t_len, LANES), lambda b, p, i: (b, cbv + p)),
            pl.BlockSpec((tq, tq), lambda b, p, i: (0, 0)),
        ],
        out_specs=pl.BlockSpec((tq, LANES), lambda b, p, i: (b * n_q + i, p)),
        out_shape=jax.ShapeDtypeStruct((bsz * t_len, ATT_WIDTH), BF16),
        scratch_shapes=[pltpu.VMEM((2, tq, LANES), F32), pltpu.VMEM((2, tq, LANES), F32)],
        compiler_params=_cparams(("parallel", "parallel", "arbitrary")),
        name="sb_prompt",
    )(qa, ka, va, _tri_ge(tq))


def _rel_bucket_np(dist):
    exact = REL_BUCKETS // 2
    d = np.maximum(dist, 0)
    df = np.maximum(d, 1).astype(np.float32)
    large = exact + (np.log(df / np.float32(exact)) / np.float32(math.log(REL_MAX_DIST / exact))
                     * np.float32(REL_BUCKETS - exact)).astype(np.int32)
    return np.where(d < exact, d, np.minimum(large, REL_BUCKETS - 1)).astype(np.int32)


def _bias_body(rb_ref, idx_ref, o_ref):
    h = pl.program_id(1)
    idx = idx_ref[...]
    acc = jnp.zeros(idx.shape, F32)
    for b in range(REL_BUCKETS):
        acc = jnp.where(idx == b, rb_ref[b, h], acc)
    o_ref[...] = acc


def bias_tiles(rel_bias, idx):
    nv, r, c = idx.shape
    return pl.pallas_call(
        _bias_body,
        grid=(nv, N_HEADS),
        in_specs=[pl.BlockSpec(memory_space=pltpu.SMEM),
                  pl.BlockSpec((None, r, c), lambda v, h: (v, 0, 0))],
        out_specs=pl.BlockSpec((None, None, r, c), lambda v, h: (v, h, 0, 0)),
        out_shape=jax.ShapeDtypeStruct((nv, N_HEADS, r, c), F32),
        compiler_params=_cparams(("parallel", "parallel")),
        name="bias_tiles",
    )(rel_bias, idx)


def _kmean_body(k_ref, o_ref):
    n = pl.program_id(1)

    @pl.when(n == 0)
    def _():
        o_ref[...] = jnp.zeros_like(o_ref)

    o_ref[pl.ds(n, 1), :] = jnp.mean(k_ref[...], axis=0, keepdims=True)


def kmean_prompt(ka, cbk, bsz, t_len):
    n_blk = t_len // MOBA_BLOCK
    return pl.pallas_call(
        _kmean_body,
        grid=(bsz, n_blk),
        in_specs=[pl.BlockSpec((MOBA_BLOCK, ATT_WIDTH), lambda b, n: (b * n_blk + n, cbk // 4))],
        out_specs=pl.BlockSpec((None, LANES, ATT_WIDTH), lambda b, n: (b, 0, 0)),
        out_shape=jax.ShapeDtypeStruct((bsz, LANES, ATT_WIDTH), F32),
        compiler_params=_cparams(("parallel", "arbitrary")),
        name="kmean_prompt",
    )(ka)


def _route_topk(route, lane, n_valid, n_ok):
    neg = -jnp.inf
    r = jnp.where(lane < n_valid, route, neg)
    sel = jnp.zeros(route.shape, F32)
    for j in range(MOBA_TOPK):
        m = jnp.max(r, axis=-1, keepdims=True)
        idx = jnp.min(jnp.where(r == m, lane, LANES), axis=-1, keepdims=True)
        hit = lane == idx
        sel = jnp.where(jnp.logical_and(hit, j < n_ok), 1.0, sel)
        r = jnp.where(hit, neg, r)
    return sel


def _softmax_step(s, vb, m_ref, l_ref, acc_ref):
    m_old = m_ref[...][:, 0:1]
    m_new = jnp.maximum(m_old, jnp.max(s, axis=-1, keepdims=True))
    alpha = jnp.exp(m_old - m_new)
    pexp = jnp.exp(s - m_new)
    l_new = l_ref[...][:, 0:1] * alpha + jnp.sum(pexp, axis=-1, keepdims=True)
    acc_ref[...] = acc_ref[...] * alpha + _dot(pexp.astype(BF16), vb)
    m_ref[...] = jnp.broadcast_to(m_new, m_ref.shape)
    l_ref[...] = jnp.broadcast_to(l_new, l_ref.shape)


def _moba_prompt_body(rb_ref, q_ref, k_ref, v_ref, km_ref, bias_ref, o_ref,
                      sel_ref, m_ref, l_ref, acc_ref, *, tq):
    p = pl.program_id(1)
    i = pl.program_id(2)
    per = MOBA_BLOCK // tq
    own = i // per
    par = i % per
    scale = HEAD_DIM ** -0.5
    q = q_ref[...]
    lane = lax.broadcasted_iota(jnp.int32, q.shape, 1)
    row = lax.broadcasted_iota(jnp.int32, (tq, MOBA_BLOCK), 0)
    col = lax.broadcasted_iota(jnp.int32, (tq, MOBA_BLOCK), 1)
    neg = -jnp.inf
    km = km_ref[...]
    qs = []
    for h in range(2):
        hm = (lane < HEAD_DIM) if h == 0 else (lane >= HEAD_DIM)
        qf = jnp.where(hm, q, 0.0)
        sel_ref[h] = _route_topk(_dot3_nt(qf, km), lane, own, own)
        qs.append((qf * scale).astype(BF16))
        m_ref[h] = jnp.full((tq, LANES), neg, F32)
        l_ref[h] = jnp.zeros((tq, LANES), F32)
        acc_ref[h] = jnp.zeros((tq, LANES), F32)

    def block(n):
        n0 = pl.multiple_of(n * MOBA_BLOCK, MOBA_BLOCK)
        return (k_ref[pl.ds(n0, MOBA_BLOCK), :].astype(BF16),
                v_ref[pl.ds(n0, MOBA_BLOCK), :].astype(BF16))

    def picked(h, n):
        return jnp.sum(jnp.where(lane == n, sel_ref[h], 0.0), axis=-1, keepdims=True) > 0.5

    kb, vb = block(own)
    causal = col <= row + par * tq
    for h in range(2):
        s = _dot_nt(qs[h], kb) + bias_ref[par, 0, h]
        _softmax_step(jnp.where(causal, s, neg), vb, m_ref.at[h], l_ref.at[h], acc_ref.at[h])

    @pl.when(own >= 1)
    def _():
        kb, vb = block(own - 1)
        for h in range(2):
            s = _dot_nt(qs[h], kb) + bias_ref[par, 1, h]
            _softmax_step(jnp.where(picked(h, own - 1), s, neg), vb,
                          m_ref.at[h], l_ref.at[h], acc_ref.at[h])

    def far(n, carry):
        kb, vb = block(n)
        for h in range(2):
            s = _dot_nt(qs[h], kb) + rb_ref[REL_BUCKETS - 1, 2 * p + h]
            _softmax_step(jnp.where(picked(h, n), s, neg), vb,
                          m_ref.at[h], l_ref.at[h], acc_ref.at[h])
        return carry

    lax.fori_loop(0, own - 1, far, 0)
    o0 = acc_ref[0] / l_ref[0]
    o1 = acc_ref[1] / l_ref[1]
    o_ref[...] = jnp.where(lane < HEAD_DIM, o0, o1).astype(o_ref.dtype)


def moba_prompt_bias_idx(tq):
    per = MOBA_BLOCK // tq
    t = np.arange(tq)[:, None]
    s = np.arange(MOBA_BLOCK)[None, :]
    idx = np.stack([np.stack([_rel_bucket_np(par * tq + kind * MOBA_BLOCK + t - s)
                              for kind in range(2)]) for par in range(per)])
    return idx.reshape(per * 2, tq, MOBA_BLOCK)


def moba_prompt(qa, ka, va, kmean, rel_bias, bias, cbq, cbk, cbv, bsz, t_len, tq=128):
    n_q = t_len // tq
    n_hp = N_HEADS // 2
    per = MOBA_BLOCK // tq
    bias = bias.reshape(per, 2, N_HEADS, tq, MOBA_BLOCK)
    return pl.pallas_call(
        functools.partial(_moba_prompt_body, tq=tq),
        grid=(bsz, n_hp, n_q),
        in_specs=[
            pl.BlockSpec(memory_space=pltpu.SMEM),
            pl.BlockSpec((tq, LANES), lambda b, p, i: (b * n_q + i, cbq + p)),
            pl.BlockSpec((t_len, LANES), lambda b, p, i: (b, cbk + p)),
            pl.BlockSpec((t_len, LANES), lambda b, p, i: (b, cbv + p)),
            pl.BlockSpec((None, LANES, LANES), lambda b, p, i: (b, 0, p)),
            pl.BlockSpec((per, 2, 2, tq, MOBA_BLOCK), lambda b, p, i: (0, 0, p, 0, 0)),
        ],
        out_specs=pl.BlockSpec((tq, LANES), lambda b, p, i: (b * n_q + i, p)),
        out_shape=jax.ShapeDtypeStruct((bsz * t_len, ATT_WIDTH), BF16),
        scratch_shapes=[pltpu.VMEM((2, tq, LANES), F32)] * 4,
        compiler_params=_cparams(("parallel", "parallel", "arbitrary")),
        name="moba_prompt",
    )(rel_bias, qa, ka, va, kmean, bias)


def _ssd_body(xbc_ref, dt_ref, cprev_ref, h0_ref, cw_ref, cb_ref, dtb_ref, alog_ref, dsk_ref,
              e_ref, et_ref, eye_ref, tril_ref, y_ref, h_ref, xf_ref, *, q, t_valid):
    c = pl.program_id(1)
    hp2 = 2 * SSM_P
    gw = (SSM_HEADS // SSM_GROUPS) * SSM_P

    @pl.when(c == 0)
    def _():
        xf_ref[0:SUBLANES, :] = cprev_ref[...]
        h_ref[...] = h0_ref[...]

    xf_ref[SUBLANES:SUBLANES + q, :] = xbc_ref[...]
    base = SUBLANES - (CONV_W - 1)
    conv = cb_ref[...]
    for j in range(CONV_W):
        conv = conv + xf_ref[base + j:base + j + q, :] * cw_ref[j:j + 1, :]
    xf_ref[0:SUBLANES, :] = xf_ref[q:q + SUBLANES, :]
    act = conv * _sigmoid(conv)
    xs = act[:, :SSM_INNER]

    dt = _softplus(dt_ref[...] + dtb_ref[...])
    if t_valid < q:
        trow = lax.broadcasted_iota(jnp.int32, dt.shape, 0)
        dt = jnp.where(trow < t_valid, dt, 0.0)
    a = dt * (-jnp.exp(alog_ref[...]))
    a_hi, a_lo = _split2(a)
    tril = tril_ref[...]
    acs = _dot(tril, a_hi) + _dot(tril, a_lo)
    acs_t = _dot2_nt(eye_ref[...], acs)
    e = e_ref[...]
    xr = xs * _dot2(dt, e)
    eacs_e = _dot2(jnp.exp(acs), e)
    xd = xr * _dot2(jnp.exp(acs[q - 1:q, :] - acs), e)
    xd_t = xd.T
    cd = jnp.broadcast_to(jnp.exp(acs_t[:, q - 1:q]), (LANES, LANES))
    cd_hi, cd_lo = _split2(cd)
    et = et_ref[...]
    f = _dot(et, cd_hi) + _dot(et, cd_lo)
    dsk_e = _dot2(jnp.broadcast_to(dsk_ref[...], (SUBLANES, LANES)), e)[0:1, :]

    row = lax.broadcasted_iota(jnp.int32, (q, q), 0)
    col = lax.broadcasted_iota(jnp.int32, (q, q), 1)
    causal = col <= row
    lane = lax.broadcasted_iota(jnp.int32, (q, hp2), 1)
    hpg = SSM_HEADS // SSM_GROUPS
    for g in range(SSM_GROUPS):
        bg = act[:, SSM_INNER + g * SSM_N:SSM_INNER + (g + 1) * SSM_N].astype(BF16)
        cg = act[:, SSM_INNER + (SSM_GROUPS + g) * SSM_N:
                 SSM_INNER + (SSM_GROUPS + g + 1) * SSM_N].astype(BF16)
        cbm = _dot_nt(cg, bg)
        hg = h_ref[g * hpg:(g + 1) * hpg].reshape(gw, SSM_N)
        y_off = _dot_nt(cg, hg.astype(BF16)) * eacs_e[:, g * gw:(g + 1) * gw]
        for pr in range(hpg // 2):
            yd = []
            xr_pair = xr[:, g * gw + pr * hp2:g * gw + (pr + 1) * hp2].astype(BF16)
            for hh in range(2):
                h = g * hpg + pr * 2 + hh
                seg = acs[:, h:h + 1] - acs_t[h:h + 1, :]
                m = (cbm * jnp.where(causal, jnp.exp(seg), 0.0)).astype(BF16)
                yd.append(_dot(m, xr_pair))
            y_pair = (jnp.where(lane < SSM_P, yd[0], yd[1]) + y_off[:, pr * hp2:(pr + 1) * hp2]
                      + xs[:, g * gw + pr * hp2:g * gw + (pr + 1) * hp2]
                      * dsk_e[:, g * gw + pr * hp2:g * gw + (pr + 1) * hp2])
            y_ref[:, g * gw + pr * hp2:g * gw + (pr + 1) * hp2] = y_pair
        st = _dot(xd_t[g * gw:(g + 1) * gw, :].astype(BF16), bg)
        h_ref[g * hpg:(g + 1) * hpg] = (hg * f[g * gw:(g + 1) * gw, :] + st).reshape(hpg, SSM_P, SSM_N)


def _ssd_consts(q):
    hidx = np.arange(SSM_INNER) // SSM_P
    e = (np.arange(LANES)[:, None] == hidx[None, :]).astype(np.float32)
    return (jnp.asarray(e, BF16), jnp.asarray(e.T, BF16),
            jnp.asarray(np.eye(LANES, dtype=np.float32), BF16),
            jnp.asarray(np.tril(np.ones((q, q), np.float32)), BF16))


def ssd(xa, cbx, dta, cprev, h0, layer, cw, cb, dtb, alog, dsk, bsz, n_c, t_valid, q=SSD_CHUNK):
    e, et, eye, tril = _ssd_consts(q)
    xblk = CONV_DIM // LANES
    const = lambda shape: pl.BlockSpec(shape, lambda b, c: (0,) * len(shape))
    return pl.pallas_call(
        functools.partial(_ssd_body, q=q, t_valid=t_valid),
        grid=(bsz, n_c),
        in_specs=[
            pl.BlockSpec((q, CONV_DIM), lambda b, c: (b * n_c + c, cbx // xblk)),
            pl.BlockSpec((q, LANES), lambda b, c: (b * n_c + c, 0)),
            pl.BlockSpec((None, SUBLANES, CONV_DIM), lambda b, c: (b, 0, 0)),
            pl.BlockSpec((None, None, SSM_HEADS, SSM_P, SSM_N), lambda b, c: (layer, b, 0, 0, 0)),
            const((CONV_W, CONV_DIM)), const((1, CONV_DIM)),
            const((1, LANES)), const((1, LANES)), const((1, LANES)),
            const((LANES, SSM_INNER)), const((SSM_INNER, LANES)), const((LANES, LANES)),
            const((q, q)),
        ],
        out_specs=[
            pl.BlockSpec((q, SSM_INNER), lambda b, c: (b * n_c + c, 0)),
            pl.BlockSpec((None, SSM_HEADS, SSM_P, SSM_N), lambda b, c: (b, 0, 0, 0)),
        ],
        out_shape=[jax.ShapeDtypeStruct((bsz * n_c * q, SSM_INNER), F32),
                   jax.ShapeDtypeStruct((bsz, SSM_HEADS, SSM_P, SSM_N), F32)],
        scratch_shapes=[pltpu.VMEM((q + SUBLANES, CONV_DIM), F32)],
        compiler_params=_cparams(("parallel", "arbitrary")),
        name="ssd",
    )(xa, dta, cprev, h0, cw, cb, dtb, alog, dsk, e, et, eye, tril)


def _merge_body(x_ref, osb_ref, omb_ref, y_ref, z_ref, g0_ref, g1_ref, g2_ref, gs_ref,
                wsb_ref, wmb_ref, wss_ref, wo_ref, o_ref):
    z = z_ref[...]
    y = _rms(y_ref[...] * (z * _sigmoid(z)), gs_ref[...]).astype(BF16)
    merged = (_sigmoid(g0_ref[...]) * _dot(osb_ref[...], wsb_ref[...])
              + _sigmoid(g1_ref[...]) * _dot(omb_ref[...], wmb_ref[...])
              + _sigmoid(g2_ref[...]) * _dot(y, wss_ref[...]))
    o_ref[...] = x_ref[...] + _dot(merged.astype(BF16), wo_ref[...])


def merge(x, osb, omb, y, u, gs, wsb, wmb, wss, wo, tm=512):
    n, d = x.shape
    db = d // LANES
    row = lambda cb, w: pl.BlockSpec((tm, w), lambda i: (i, cb))
    const = lambda a: pl.BlockSpec(a.shape, lambda i: (0, 0))
    return pl.pallas_call(
        _merge_body,
        grid=(n // tm,),
        in_specs=[row(0, d), row(0, ATT_WIDTH), row(0, ATT_WIDTH), row(0, d),
                  row(CB_Z // db, d), row(CB_GATE // db, d), row(CB_GATE // db + 1, d),
                  row(CB_GATE // db + 2, d),
                  const(gs), const(wsb), const(wmb), const(wss), const(wo)],
        out_specs=row(0, d),
        out_shape=jax.ShapeDtypeStruct((n, d), F32),
        compiler_params=_cparams(("parallel",)),
        name="merge",
    )(x, osb, omb, y, u, u, u, u, gs, wsb, wmb, wss, wo)


def _router_body(x_ref, g_ref, wr_ref, xn_ref, cmb_ref):
    xn = _rms(x_ref[...], g_ref[...])
    xn_ref[...] = xn.astype(BF16)
    wr = wr_ref[...]
    xh, xl = _split2(xn)
    wh, wl = _split2(wr[:-SUBLANES, :])
    logit = _dot(xh, wh) + _dot(xh, wl) + _dot(xl, wh) + wr[-SUBLANES:-SUBLANES + 1, :]
    lane = lax.broadcasted_iota(jnp.int32, logit.shape, 1)
    neg = -jnp.inf
    is_g = jnp.logical_and(lane >= N_EXPERTS, lane < N_EXPERTS + N_GROUPS)
    gl = jnp.where(is_g, logit, neg)
    gmax = jnp.max(gl, axis=-1, keepdims=True)
    grp = jnp.min(jnp.where(gl == gmax, lane, LANES), axis=-1, keepdims=True) - N_EXPERTS
    p_grp = 1.0 / jnp.sum(jnp.exp(gl - gmax), axis=-1, keepdims=True)
    in_grp = jnp.logical_and(lane >= grp * EPG, lane < grp * EPG + EPG)
    el = jnp.where(in_grp, logit, neg)
    v1 = jnp.max(el, axis=-1, keepdims=True)
    i1 = jnp.min(jnp.where(el == v1, lane, LANES), axis=-1, keepdims=True)
    el2 = jnp.where(lane == i1, neg, el)
    v2 = jnp.max(el2, axis=-1, keepdims=True)
    i2 = jnp.min(jnp.where(el2 == v2, lane, LANES), axis=-1, keepdims=True)
    d = jnp.exp(v2 - v1)
    w1 = p_grp / (1.0 + d)
    w2 = p_grp * d / (1.0 + d)
    cmb_ref[...] = jnp.where(lane == i1, w1, 0.0) + jnp.where(lane == i2, w2, 0.0)


def router(x, g, wr, tm=512):
    n, d = x.shape
    return pl.pallas_call(
        _router_body,
        grid=(n // tm,),
        in_specs=[pl.BlockSpec((tm, d), lambda i: (i, 0)),
                  pl.BlockSpec((1, d), lambda i: (0, 0)),
                  pl.BlockSpec(wr.shape, lambda i: (0, 0))],
        out_specs=[pl.BlockSpec((tm, d), lambda i: (i, 0)),
                   pl.BlockSpec((tm, LANES), lambda i: (i, 0))],
        out_shape=[jax.ShapeDtypeStruct((n, d), BF16),
                   jax.ShapeDtypeStruct((n, LANES), F32)],
        compiler_params=_cparams(("parallel",)),
        name="router",
    )(x, g, wr)


def _moe_body(x_ref, xn_ref, cmb_ref, wgu_ref, wd_ref, gf_ref, o_ref, *, final_norm):
    e = pl.program_id(1)

    @pl.when(e == 0)
    def _():
        o_ref[...] = x_ref[...]

    cmb = cmb_ref[...]
    lane = lax.broadcasted_iota(jnp.int32, cmb.shape, 1)
    c = jnp.sum(jnp.where(lane == e, cmb, 0.0), axis=-1, keepdims=True)
    gu = _dot(xn_ref[...], wgu_ref[...])
    gt = gu[:, :EXPERT_FF]
    hid = gt * _sigmoid(gt) * gu[:, EXPERT_FF:]
    o_ref[...] += _dot((c * hid).astype(BF16), wd_ref[...])

    if final_norm:
        @pl.when(e == N_EXPERTS - 1)
        def _():
            o_ref[...] = _rms(o_ref[...], gf_ref[...])


def moe(x, xn, cmb, wgu, wd, gf, final_norm, tm=512):
    n, d = x.shape
    return pl.pallas_call(
        functools.partial(_moe_body, final_norm=final_norm),
        grid=(n // tm, N_EXPERTS),
        in_specs=[pl.BlockSpec((tm, d), lambda i, e: (i, 0)),
                  pl.BlockSpec((tm, d), lambda i, e: (i, 0)),
                  pl.BlockSpec((tm, LANES), lambda i, e: (i, 0)),
                  pl.BlockSpec((None, d, 2 * EXPERT_FF), lambda i, e: (e, 0, 0)),
                  pl.BlockSpec((None, EXPERT_FF, d), lambda i, e: (e, 0, 0)),
                  pl.BlockSpec((1, d), lambda i, e: (0, 0))],
        out_specs=pl.BlockSpec((tm, d), lambda i, e: (i, 0)),
        out_shape=jax.ShapeDtypeStruct((n, d), F32),
        compiler_params=_cparams(("parallel", "arbitrary")),
        name="moe",
    )(x, xn, cmb, wgu, wd, gf)


T_PAD = SUBLANES
QROWS = N_HEADS * T_PAD


def _q_block_diag(q8):
    lane = lax.broadcasted_iota(jnp.int32, q8.shape, 1)
    return jnp.concatenate(
        [jnp.where(lane // HEAD_DIM == h, q8, 0.0) for h in range(N_HEADS)], axis=0)


def _out_from_acc_t(acc_t, gt):
    r = lax.broadcasted_iota(jnp.int32, acc_t.shape, 0)
    c = lax.broadcasted_iota(jnp.int32, acc_t.shape, 1)
    return _dot2_nt(gt, jnp.where(r // HEAD_DIM == c // T_PAD, acc_t, 0.0))


def _sample_consts():
    gt = (np.arange(T_PAD)[:, None] == (np.arange(QROWS) % T_PAD)[None, :]).astype(np.float32)
    return (_tri_ge(PAGE), jnp.asarray(np.eye(PAGE, dtype=np.float32), BF16), jnp.asarray(gt, BF16))


def _sb_sample_body(pt_ref, q_ref, kn_ref, vn_ref, *rest, n_pages, t_new):
    k_refs = rest[:n_pages]
    v_refs = rest[n_pages:2 * n_pages]
    tri_ref, eye_ref, gt_ref, o_ref, kbuf, vbuf, acc_ref, car_ref, flag = rest[2 * n_pages:]
    scale = HEAD_DIM ** -0.5
    qbd = (_q_block_diag(q_ref[...]) * scale).astype(BF16)
    tri = tri_ref[...]
    eye = eye_ref[...]

    def page(kt, vt, mask):
        z = _dot(qbd, kt)
        lk = -_softplus(z)
        if mask is not None:
            lk = jnp.where(mask, lk, 0.0)
        incl = _dot2(lk, tri)
        w = jnp.exp(z + incl + car_ref[...][:, 0:1])
        if mask is not None:
            w = jnp.where(mask, w, 0.0)
        w_t = _dot_nt(eye, w.astype(BF16)).astype(BF16)
        acc_ref[...] += _dot(vt, w_t)
        car = car_ref[...] + jnp.broadcast_to(incl[:, 0:1], car_ref.shape)
        car_ref[...] = car
        flag[0] = (jnp.max(car) >= EXP_ZERO).astype(jnp.int32)

    kbuf[...] = jnp.zeros_like(kbuf)
    vbuf[...] = jnp.zeros_like(vbuf)
    kbuf[0:T_PAD, :] = kn_ref[...]
    vbuf[0:T_PAD, :] = vn_ref[...]
    acc_ref[...] = jnp.zeros_like(acc_ref)
    car_ref[...] = jnp.zeros_like(car_ref)
    r = lax.broadcasted_iota(jnp.int32, (QROWS, PAGE), 0)
    c = lax.broadcasted_iota(jnp.int32, (QROWS, PAGE), 1)
    past = jnp.logical_and(c < r % T_PAD, c < t_new)
    page(kbuf[...].T.astype(BF16), vbuf[...].T.astype(BF16), past)

    for p in reversed(range(n_pages)):
        @pl.when(flag[0] > 0)
        def _():
            page(k_refs[p][...].reshape(ATT_WIDTH, PAGE).astype(BF16),
                 v_refs[p][...].reshape(ATT_WIDTH, PAGE).astype(BF16), None)

    o_ref[...] = _out_from_acc_t(acc_ref[...], gt_ref[...]).astype(o_ref.dtype)


def _page_specs(layer, n_pages):
    return [pl.BlockSpec((None, None, N_HEADS, HEAD_DIM, PAGE),
                         functools.partial(lambda b, pt, p: (layer, pt[b, p], 0, 0, 0), p=p))
            for p in range(n_pages)]


def sb_sample(page_table, us8, cache_k, cache_v, layer, t_new):
    n_seq, n_pages = page_table.shape
    tri, eye, gt = _sample_consts()
    tok = lambda cb: pl.BlockSpec((None, T_PAD, ATT_WIDTH), lambda b, pt: (b, 0, cb // 4))
    const = lambda a: pl.BlockSpec(a.shape, lambda b, pt: (0,) * a.ndim)
    grid_spec = pltpu.PrefetchScalarGridSpec(
        num_scalar_prefetch=1,
        grid=(n_seq,),
        in_specs=[tok(CB_Q_SB), tok(CB_K_SB), tok(CB_V_SB)]
        + _page_specs(layer, n_pages) + _page_specs(layer, n_pages)
        + [const(tri), const(eye), const(gt)],
        out_specs=pl.BlockSpec((None, T_PAD, ATT_WIDTH), lambda b, pt: (b, 0, 0)),
        scratch_shapes=[pltpu.VMEM((PAGE, ATT_WIDTH), F32), pltpu.VMEM((PAGE, ATT_WIDTH), F32),
                        pltpu.VMEM((ATT_WIDTH, QROWS), F32), pltpu.VMEM((QROWS, LANES), F32),
                        pltpu.SMEM((1,), jnp.int32)],
    )
    return pl.pallas_call(
        functools.partial(_sb_sample_body, n_pages=n_pages, t_new=t_new),
        grid_spec=grid_spec,
        out_shape=jax.ShapeDtypeStruct((n_seq, T_PAD, ATT_WIDTH), BF16),
        compiler_params=_cparams(("arbitrary",)),
        name="sb_sample",
    )(page_table, us8, us8, us8, *([cache_k] * n_pages), *([cache_v] * n_pages), tri, eye, gt)


def _moba_sample_body(pt_ref, q_ref, kn_ref, vn_ref, *rest, n_pages, t_new):
    k_refs = rest[:n_pages]
    v_refs = rest[n_pages:2 * n_pages]
    bias_ref, eye_ref, gt_ref, o_ref, kbuf, vbuf, s_ref = rest[2 * n_pages:]
    ppb = MOBA_BLOCK // PAGE
    n_blk = n_pages // ppb
    scale = HEAD_DIM ** -0.5
    qf = _q_block_diag(q_ref[...])
    qs = (qf * scale).astype(BF16)
    eye = eye_ref[...]
    neg = -jnp.inf
    lane = lax.broadcasted_iota(jnp.int32, (QROWS, LANES), 1)
    lane_w = lax.broadcasted_iota(jnp.int32, (ATT_WIDTH, LANES), 1)

    ksum = jnp.zeros((ATT_WIDTH, LANES), F32)
    for n in range(n_blk):
        blk = k_refs[ppb * n][...].reshape(ATT_WIDTH, PAGE)
        for j in range(1, ppb):
            blk = blk + k_refs[ppb * n + j][...].reshape(ATT_WIDTH, PAGE)
        ksum = jnp.where(lane_w == n, jnp.sum(blk, axis=-1, keepdims=True), ksum)
    kmean = ksum * (1.0 / MOBA_BLOCK)
    qh, ql = _split2(qf)
    mh, ml = _split2(kmean)
    route = _dot(qh, mh) + _dot(qh, ml) + _dot(ql, mh)
    sel = _route_topk(route, lane, n_blk, n_blk)

    m = jnp.full((QROWS, 1), neg, F32)
    for p in range(n_pages):
        s = _dot(qs, k_refs[p][...].reshape(ATT_WIDTH, PAGE).astype(BF16)) + bias_ref[p]
        s = jnp.where(sel[:, p // ppb:p // ppb + 1] > 0.5, s, neg)
        s_ref[p] = s
        m = jnp.maximum(m, jnp.max(s, axis=-1, keepdims=True))
    kbuf[...] = jnp.zeros_like(kbuf)
    vbuf[...] = jnp.zeros_like(vbuf)
    kbuf[0:T_PAD, :] = kn_ref[...]
    vbuf[0:T_PAD, :] = vn_ref[...]
    r = lax.broadcasted_iota(jnp.int32, (QROWS, PAGE), 0)
    c = lax.broadcasted_iota(jnp.int32, (QROWS, PAGE), 1)
    causal = jnp.logical_and(c <= r % T_PAD, c < T_PAD)
    s = _dot(qs, kbuf[...].T.astype(BF16)) + bias_ref[n_pages]
    s = jnp.where(causal, s, neg)
    s_ref[n_pages] = s
    m = jnp.maximum(m, jnp.max(s, axis=-1, keepdims=True))

    l = jnp.zeros((QROWS, 1), F32)
    acc_t = jnp.zeros((ATT_WIDTH, QROWS), F32)
    for p in range(n_pages + 1):
        pe = jnp.exp(s_ref[p] - m)
        l = l + jnp.sum(pe, axis=-1, keepdims=True)
        pe_t = _dot_nt(eye, pe.astype(BF16)).astype(BF16)
        vt = (v_refs[p][...].reshape(ATT_WIDTH, PAGE) if p < n_pages else vbuf[...].T).astype(BF16)
        acc_t = acc_t + _dot(vt, pe_t)
    avg = jnp.full((SUBLANES, LANES), 1.0 / LANES, BF16)
    l_row = _dot2_nt(avg, jnp.broadcast_to(l, (QROWS, LANES)))[0:1, :]
    o_ref[...] = _out_from_acc_t(acc_t / l_row, gt_ref[...]).astype(o_ref.dtype)


def moba_sample_bias_idx(n_pages):
    t = np.arange(T_PAD)[:, None]
    s = np.arange(PAGE)[None, :]
    past = [_rel_bucket_np((n_pages - p) * PAGE + t - s) for p in range(n_pages)]
    return np.stack(past + [_rel_bucket_np(t - s)])


def moba_sample(page_table, us8, cache_k, cache_v, bias, layer, t_new):
    n_seq, n_pages = page_table.shape
    _, eye, gt = _sample_consts()
    tok = lambda cb: pl.BlockSpec((None, T_PAD, ATT_WIDTH), lambda b, pt: (b, 0, cb // 4))
    const = lambda a: pl.BlockSpec(a.shape, lambda b, pt: (0,) * a.ndim)
    grid_spec = pltpu.PrefetchScalarGridSpec(
        num_scalar_prefetch=1,
        grid=(n_seq,),
        in_specs=[tok(CB_Q_MB), tok(CB_K_MB), tok(CB_V_MB)]
        + _page_specs(layer, n_pages) + _page_specs(layer, n_pages)
        + [const(bias), const(eye), const(gt)],
        out_specs=pl.BlockSpec((None, T_PAD, ATT_WIDTH), lambda b, pt: (b, 0, 0)),
        scratch_shapes=[pltpu.VMEM((PAGE, ATT_WIDTH), F32), pltpu.VMEM((PAGE, ATT_WIDTH), F32),
                        pltpu.VMEM((n_pages + 1, QROWS, PAGE), F32)],
    )
    return pl.pallas_call(
        functools.partial(_moba_sample_body, n_pages=n_pages, t_new=t_new),
        grid_spec=grid_spec,
        out_shape=jax.ShapeDtypeStruct((n_seq, T_PAD, ATT_WIDTH), BF16),
        compiler_params=_cparams(("arbitrary",)),
        name="moba_sample",
    )(page_table, us8, us8, us8, *([cache_k] * n_pages), *([cache_v] * n_pages), bias, eye, gt)


def _pad_lanes(v):
    return jnp.pad(v, (0, LANES - v.shape[0])).reshape(1, LANES)


def kernel(x_prompt, x_sample, cache_k_sb, cache_v_sb, cache_k_moba, cache_v_moba, page_table, state_ssm, state_conv, rel_bias, g_mix, w_in, conv_w, conv_b, dt_bias, a_log, d_skip, g_ssm, w_branch_sb, w_branch_moba, w_branch_ssm, w_out, g_ffn, w_router_group, b_router_group, w_router_expert, b_router_expert, w_expert_gate, w_expert_up, w_expert_down, g_final):
    bsz, seq, d = x_prompt.shape
    n_seq, t_new, _ = x_sample.shape
    depth = w_in.shape[0]
    n_pages = page_table.shape[1]
    n_p = bsz * seq
    n_s = n_seq * t_new
    assert d == D_MODEL and seq % MOBA_BLOCK == 0 and t_new <= T_PAD
    assert (n_pages * PAGE) % MOBA_BLOCK == 0 and cache_k_sb.shape[2] == PAGE

    x = jnp.concatenate([x_prompt.reshape(n_p, d), x_sample.reshape(n_s, d)], axis=0)
    pages = [c.transpose(0, 1, 3, 4, 2) for c in (cache_k_sb, cache_v_sb, cache_k_moba, cache_v_moba)]
    tq_mb = 128
    bias_p = bias_tiles(rel_bias, jnp.asarray(moba_prompt_bias_idx(tq_mb)))
    bias_s = bias_tiles(rel_bias, jnp.asarray(moba_sample_bias_idx(n_pages))).reshape(
        n_pages + 1, QROWS, PAGE)
    zero_conv = jnp.zeros((bsz, SUBLANES, CONV_DIM), F32)
    zero_state = jnp.zeros((1, bsz, SSM_HEADS, SSM_P, SSM_N), F32)
    n_main = CB_XBC * LANES + CONV_DIM
    pad_t = SSD_CHUNK - t_new

    new_p, new_s = [], []
    for l in range(depth):
        w_main = jnp.concatenate([w_in[l][:, :n_main], w_in[l][:, n_main + SSM_HEADS:]], axis=1).astype(BF16)
        w_dt = jnp.pad(w_in[l][:, n_main:n_main + SSM_HEADS], ((0, 0), (0, LANES - SSM_HEADS))).astype(BF16)
        u, dtr = in_proj(x, g_mix[l].reshape(1, d), w_main, w_dt)
        ssm_par = (conv_w[l], conv_b[l].reshape(1, CONV_DIM), _pad_lanes(dt_bias[l]),
                   _pad_lanes(a_log[l]), _pad_lanes(d_skip[l]))

        osb_p = sb_prompt(u, u, u, CB_Q_SB, CB_K_SB, CB_V_SB, bsz, seq)
        km = kmean_prompt(u, CB_K_MB, bsz, seq)
        omb_p = moba_prompt(u, u, u, km, rel_bias, bias_p, CB_Q_MB, CB_K_MB, CB_V_MB, bsz, seq, tq=tq_mb)
        y_p, ssm_p = ssd(u, CB_XBC, dtr, zero_conv, zero_state, 0, *ssm_par,
                         bsz=bsz, n_c=seq // SSD_CHUNK, t_valid=SSD_CHUNK)

        u_s = u[n_p:].reshape(n_seq, t_new, U_COLS)
        us8 = jnp.pad(u_s, ((0, 0), (0, T_PAD - t_new), (0, 0)))
        osb_s = sb_sample(page_table, us8, pages[0], pages[1], l, t_new)[:, :t_new].reshape(n_s, ATT_WIDTH)
        omb_s = moba_sample(page_table, us8, pages[2], pages[3], bias_s, l, t_new)[:, :t_new].reshape(n_s, ATT_WIDTH)
        xbc_s = jnp.pad(u_s[:, :, CB_XBC * LANES:CB_XBC * LANES + CONV_DIM],
                        ((0, 0), (0, pad_t), (0, 0))).reshape(n_seq * SSD_CHUNK, CONV_DIM)
        dt_s = jnp.pad(dtr[n_p:].reshape(n_seq, t_new, LANES),
                       ((0, 0), (0, pad_t), (0, 0))).reshape(n_seq * SSD_CHUNK, LANES)
        cprev_s = jnp.pad(state_conv[l], ((0, 0), (SUBLANES - (CONV_W - 1), 0), (0, 0)))
        y_s, ssm_s = ssd(xbc_s, 0, dt_s, cprev_s, state_ssm, l, *ssm_par,
                         bsz=n_seq, n_c=1, t_valid=t_new)
        y_s = y_s.reshape(n_seq, SSD_CHUNK, SSM_INNER)[:, :t_new].reshape(n_s, SSM_INNER)

        x1 = merge(x, jnp.concatenate([osb_p, osb_s]), jnp.concatenate([omb_p, omb_s]),
                   jnp.concatenate([y_p, y_s]), u, g_ssm[l].reshape(1, d),
                   w_branch_sb[l].astype(BF16), w_branch_moba[l].astype(BF16),
                   w_branch_ssm[l].astype(BF16), w_out[l].astype(BF16))
        wr = jnp.zeros((d + SUBLANES, LANES), F32)
        wr = wr.at[:d, :N_EXPERTS].set(w_router_expert[l]).at[:d, N_EXPERTS:N_EXPERTS + N_GROUPS].set(w_router_group[l])
        wr = wr.at[d, :N_EXPERTS].set(b_router_expert[l]).at[d, N_EXPERTS:N_EXPERTS + N_GROUPS].set(b_router_group[l])
        xn, cmb = router(x1, g_ffn[l].reshape(1, d), wr)
        wgu = jnp.concatenate([w_expert_gate[l], w_expert_up[l]], axis=-1).astype(BF16)
        x = moe(x1, xn, cmb, wgu, w_expert_down[l].astype(BF16), g_final.reshape(1, d),
                final_norm=(l == depth - 1))

        def col(a, cb, w):
            return a[..., cb * LANES:cb * LANES + w]

        u_p = u[:n_p].reshape(bsz, seq, U_COLS)
        kv = lambda a, cb: col(a, cb, ATT_WIDTH).reshape(a.shape[0], a.shape[1], N_HEADS, HEAD_DIM)
        new_p.append((kv(u_p, CB_K_SB), kv(u_p, CB_V_SB), kv(u_p, CB_K_MB), kv(u_p, CB_V_MB),
                      ssm_p, col(u_p, CB_XBC, CONV_DIM)[:, seq - (CONV_W - 1):]))
        conv_s = jnp.concatenate([state_conv[l], col(u_s, CB_XBC, CONV_DIM)], axis=1)[:, t_new:]
        new_s.append((kv(u_s, CB_K_SB), kv(u_s, CB_V_SB), kv(u_s, CB_K_MB), kv(u_s, CB_V_MB),
                      ssm_s, conv_s))

    stack = lambda states: tuple(jnp.stack([s[i] for s in states]) for i in range(6))
    return ((x[:n_p].reshape(bsz, seq, d), x[n_p:].reshape(n_seq, t_new, d))
            + stack(new_p) + stack(new_s))
```

```python
import functools
import math

import numpy as np
import jax
import jax.numpy as jnp
from jax import lax
from jax.experimental import pallas as pl
from jax.experimental.pallas import tpu as pltpu

F32 = jnp.float32
BF16 = jnp.bfloat16

LANES = 128
SUBLANES = 8
VMEM_LIMIT = 56 * 1024 * 1024

D_MODEL = 1024
HEAD_DIM = 64
N_HEADS = 8
ATT_WIDTH = N_HEADS * HEAD_DIM
MOBA_BLOCK = 256
MOBA_TOPK = 3
PAGE = 128
SSM_HEADS = 16
SSM_P = 64
SSM_GROUPS = 4
SSM_N = 128
SSM_INNER = SSM_HEADS * SSM_P
CONV_W = 4
CONV_DIM = SSM_INNER + 2 * SSM_GROUPS * SSM_N
SSD_CHUNK = 128
REL_BUCKETS = 32
REL_MAX_DIST = 128
N_GROUPS = 4
EPG = 8
N_EXPERTS = N_GROUPS * EPG
EXPERT_FF = 256
EPS = 1e-6
EXP_ZERO = -88.0
NEG_BIG = -1e30

CB_Q_SB, CB_K_SB, CB_V_SB = 0, 4, 8
CB_Q_MB, CB_K_MB, CB_V_MB = 12, 16, 20
CB_Z = 24
CB_XBC = 32
CB_GATE = 48
U_COLS = 72 * LANES


def _cparams(sem):
    return pltpu.CompilerParams(dimension_semantics=sem, vmem_limit_bytes=VMEM_LIMIT)


def _split2(a):
    hi = a.astype(BF16)
    lo = (a - hi.astype(F32)).astype(BF16)
    return hi, lo


def _dot(a, b):
    return jnp.dot(a, b, preferred_element_type=F32)


def _dot_nt(a, b):
    return lax.dot_general(a, b, (((1,), (1,)), ((), ())), preferred_element_type=F32)


def _dot2(a, b_exact):
    hi, lo = _split2(a)
    return _dot(hi, b_exact) + _dot(lo, b_exact)


def _dot2_nt(a_exact, b):
    hi, lo = _split2(b)
    return _dot_nt(a_exact, hi) + _dot_nt(a_exact, lo)


def _dot3_nt(a, b):
    ah, al = _split2(a)
    bh, bl = _split2(b)
    return _dot_nt(ah, bh) + _dot_nt(ah, bl) + _dot_nt(al, bh)


def _sigmoid(x):
    return 1.0 / (1.0 + jnp.exp(-x))


def _softplus(x):
    return jnp.maximum(x, 0.0) + jnp.log(1.0 + jnp.exp(-jnp.abs(x)))


def _rms(x, g):
    ms = jnp.mean(x * x, axis=-1, keepdims=True)
    return x * lax.rsqrt(ms + EPS) * g


def _in_proj_body(x_ref, g_ref, w_ref, wdt_ref, u_ref, dt_ref, qkv_ref, xn_ref, *, n_qkv):
    j = pl.program_id(1)

    @pl.when(j == 0)
    def _():
        xn = _rms(x_ref[...], g_ref[...]).astype(BF16)
        xn_ref[...] = xn
        dt_ref[...] = _dot(xn, wdt_ref[...])

    u = _dot(xn_ref[...], w_ref[...])
    u_ref[...] = u

    @pl.when(j < n_qkv)
    def _():
        qkv_ref[...] = u.astype(BF16)


def in_proj(x, g, w, wdt, tm=512, tn=1024):
    n, d = x.shape
    cols = w.shape[1]
    n_qkv = (CB_Z * LANES) // tn
    return pl.pallas_call(
        functools.partial(_in_proj_body, n_qkv=n_qkv),
        grid=(n // tm, cols // tn),
        in_specs=[
            pl.BlockSpec((tm, d), lambda i, j: (i, 0)),
            pl.BlockSpec((1, d), lambda i, j: (0, 0)),
            pl.BlockSpec((d, tn), lambda i, j: (0, j)),
            pl.BlockSpec((d, LANES), lambda i, j: (0, 0)),
        ],
        out_specs=[
            pl.BlockSpec((tm, tn), lambda i, j: (i, j)),
            pl.BlockSpec((tm, LANES), lambda i, j: (i, 0)),
            pl.BlockSpec((tm, tn), lambda i, j: (i, jnp.minimum(j, n_qkv - 1))),
        ],
        out_shape=[jax.ShapeDtypeStruct((n, cols), F32),
                   jax.ShapeDtypeStruct((n, LANES), F32),
                   jax.ShapeDtypeStruct((n, n_qkv * tn), BF16)],
        scratch_shapes=[pltpu.VMEM((tm, d), BF16)],
        compiler_params=_cparams(("parallel", "arbitrary")),
        name="in_proj",
    )(x, g, w, wdt)


def _suffix_sums(lk, tri2):
    hi, lo = _split2(lk)
    return _dot(jnp.concatenate([hi, lo], axis=1), tri2)


def _sb_tile(qh, k, v, carry, tri2, mask):
    tk = k.shape[0]
    z = _dot_nt(qh, k)
    lk = -_softplus(z)
    if mask is not None:
        lk = jnp.where(mask, lk, 0.0)
    sums = _suffix_sums(lk, tri2)
    w = jnp.exp(z + sums[:, :tk] + carry)
    if mask is not None:
        w = jnp.where(mask, w, 0.0)
    return _dot(w.astype(BF16), v), carry + sums[:, tk:]


def _stack_heads(q, scale):
    lane = lax.broadcasted_iota(jnp.int32, q.shape, 1)
    q = q * scale
    return jnp.concatenate([jnp.where(lane < HEAD_DIM, q, 0.0),
                            jnp.where(lane >= HEAD_DIM, q, 0.0)], axis=0).astype(BF16)


def _unstack_heads(o, tq):
    lane = lax.broadcasted_iota(jnp.int32, (tq, LANES), 1)
    return jnp.where(lane < HEAD_DIM, o[:tq], o[tq:])


def _sb_prompt_body(q_ref, k_ref, v_ref, tri_ref, o_ref, acc_ref, car_ref, *, tk):
    i = pl.program_id(2)
    scale = HEAD_DIM ** -0.5
    q = q_ref[...].astype(F32)
    qa = jnp.concatenate([_stack_heads(q[:tk], scale), _stack_heads(q[tk:], scale)], axis=0)
    hb = 2 * tk
    tri2 = tri_ref[...]
    row = lax.broadcasted_iota(jnp.int32, (2 * hb, tk), 0)
    col = lax.broadcasted_iota(jnp.int32, (2 * hb, tk), 1)
    tri_mask = col < row % tk

    def tile(j, carry, mask):
        kj = pl.multiple_of(j * tk, tk)
        return _sb_tile(qa, k_ref[pl.ds(kj, tk), :], v_ref[pl.ds(kj, tk), :], carry, tri2, mask)

    tiles = [(2 * i + 1, qa[hb:], tri_mask[hb:]),
             (2 * i, qa, jnp.logical_or(row >= hb, tri_mask)),
             (jnp.maximum(2 * i - 1, 0), qa, jnp.broadcast_to(i > 0, (2 * hb, tk)))]
    starts = [pl.multiple_of(j * tk, tk) for j, _, _ in tiles]
    zs = [_dot_nt(qr, k_ref[pl.ds(s, tk), :]) for s, (_, qr, _) in zip(starts, tiles)]
    lks = [jnp.where(mask, -_softplus(z), 0.0) for z, (_, _, mask) in zip(zs, tiles)]
    sums = [_suffix_sums(lk, tri2) for lk in lks]
    ws = []
    c = jnp.zeros((hb, tk), F32)
    for t, (z, sm, (_, _, mask)) in enumerate(zip(zs, sums, tiles)):
        ws.append(jnp.where(mask, jnp.exp(z + sm[:, :tk] + c), 0.0).astype(BF16))
        c = c + sm[:, tk:]
        if t == 0:
            c = jnp.concatenate([jnp.zeros((hb, tk), F32), c], axis=0)
    os_ = [_dot(w, v_ref[pl.ds(s, tk), :]) for w, s in zip(ws, starts)]
    acc_ref[...] = os_[1] + os_[2]
    acc_ref[hb:, :] += os_[0]
    car_ref[...] = c

    def cond(j):
        return jnp.logical_and(j >= 0, jnp.max(car_ref[...]) >= EXP_ZERO)

    def body(j):
        o, c = tile(j, car_ref[...], None)
        acc_ref[...] += o
        car_ref[...] = c
        return j - 1

    lax.while_loop(cond, body, 2 * i - 2)
    o_ref[0:tk, :] = _unstack_heads(acc_ref[0:hb, :], tk).astype(o_ref.dtype)
    o_ref[tk:, :] = _unstack_heads(acc_ref[hb:, :], tk).astype(o_ref.dtype)


def _tri_ge_ones(n):
    t2 = np.concatenate([np.tril(np.ones((n, n), np.float32)), np.ones((n, n), np.float32)], axis=1)
    return jnp.asarray(np.concatenate([t2, t2], axis=0), BF16)


def sb_prompt(qa, ka, va, cbq, cbk, cbv, bsz, t_len, tk=128):
    tq = 2 * tk
    n_q = t_len // tq
    n_hp = N_HEADS // 2
    return pl.pallas_call(
        functools.partial(_sb_prompt_body, tk=tk),
        grid=(bsz, n_hp, n_q),
        in_specs=[
            pl.BlockSpec((tq, LANES), lambda b, p, i: (b * n_q + i, cbq + p)),
            pl.BlockSpec((t_len, LANES), lambda b, p, i: (b, cbk + p)),
            pl.BlockSpec((t_len, LANES), lambda b, p, i: (b, cbv + p)),
            pl.BlockSpec((2 * tk, 2 * tk), lambda b, p, i: (0, 0)),
        ],
        out_specs=pl.BlockSpec((tq, LANES), lambda b, p, i: (b * n_q + i, p)),
        out_shape=jax.ShapeDtypeStruct((bsz * t_len, ATT_WIDTH), BF16),
        scratch_shapes=[pltpu.VMEM((2 * tq, LANES), F32), pltpu.VMEM((2 * tq, tk), F32)],
        compiler_params=_cparams(("parallel", "parallel", "arbitrary")),
        name="sb_prompt",
    )(qa, ka, va, _tri_ge_ones(tk))


def _rel_bucket_np(dist):
    exact = REL_BUCKETS // 2
    d = np.maximum(dist, 0)
    df = np.maximum(d, 1).astype(np.float32)
    large = exact + (np.log(df / np.float32(exact)) / np.float32(math.log(REL_MAX_DIST / exact))
                     * np.float32(REL_BUCKETS - exact)).astype(np.int32)
    return np.where(d < exact, d, np.minimum(large, REL_BUCKETS - 1)).astype(np.int32)


def _bias_body(rb_ref, idx_ref, o_ref):
    h = pl.program_id(1)
    idx = idx_ref[...]
    acc = jnp.zeros(idx.shape, F32)
    for b in range(REL_BUCKETS):
        acc = jnp.where(idx == b, rb_ref[b, h], acc)
    o_ref[...] = acc


def bias_tiles(rel_bias, idx):
    nv, r, c = idx.shape
    return pl.pallas_call(
        _bias_body,
        grid=(nv, N_HEADS),
        in_specs=[pl.BlockSpec(memory_space=pltpu.SMEM),
                  pl.BlockSpec((None, r, c), lambda v, h: (v, 0, 0))],
        out_specs=pl.BlockSpec((None, None, r, c), lambda v, h: (v, h, 0, 0)),
        out_shape=jax.ShapeDtypeStruct((nv, N_HEADS, r, c), F32),
        compiler_params=_cparams(("parallel", "parallel")),
        name="bias_tiles",
    )(rel_bias, idx)


def _kmean_body(k_ref, o_ref):
    n = pl.program_id(1)

    @pl.when(n == 0)
    def _():
        o_ref[...] = jnp.zeros_like(o_ref)

    o_ref[pl.ds(n, 1), :] = jnp.mean(k_ref[...], axis=0, keepdims=True)


def kmean_prompt(ka, cbk, bsz, t_len):
    n_blk = t_len // MOBA_BLOCK
    return pl.pallas_call(
        _kmean_body,
        grid=(bsz, n_blk),
        in_specs=[pl.BlockSpec((MOBA_BLOCK, ATT_WIDTH), lambda b, n: (b * n_blk + n, cbk // 4))],
        out_specs=pl.BlockSpec((None, LANES, ATT_WIDTH), lambda b, n: (b, 0, 0)),
        out_shape=jax.ShapeDtypeStruct((bsz, LANES, ATT_WIDTH), F32),
        compiler_params=_cparams(("parallel", "arbitrary")),
        name="kmean_prompt",
    )(ka)


def _route_topk(route, lane, n_valid, n_ok):
    neg = -jnp.inf
    r = jnp.where(lane < n_valid, route, neg)
    sel = jnp.zeros(route.shape, F32)
    for j in range(MOBA_TOPK):
        m = jnp.max(r, axis=-1, keepdims=True)
        idx = jnp.min(jnp.where(r == m, lane, LANES), axis=-1, keepdims=True)
        hit = lane == idx
        sel = jnp.where(jnp.logical_and(hit, j < n_ok), 1.0, sel)
        r = jnp.where(hit, neg, r)
    return sel


def _moba_prompt_body(rb_ref, q_ref, k_ref, v_ref, e_ref, km_ref, bias_ref, o_ref,
                      qa_ref, mrun_ref, lrun_ref, acc_ref, *, tq):
    p = pl.program_id(1)
    i = pl.program_id(2)
    per = MOBA_BLOCK // tq
    own = i // per
    par = i % per
    q = q_ref[...]
    lane = lax.broadcasted_iota(jnp.int32, q.shape, 1)
    km = km_ref[...]
    pen = []
    for h in range(2):
        qf = jnp.where((lane < HEAD_DIM) if h == 0 else (lane >= HEAD_DIM), q, 0.0)
        sel = _route_topk(_dot3_nt(qf, km), lane, own, own)
        pen.append(jnp.where(jnp.logical_or(sel > 0.5, lane == own), 0.0, NEG_BIG))
    qa_ref[...] = jnp.concatenate(
        [_stack_heads(q, HEAD_DIM ** -0.5), jnp.concatenate(pen, axis=0).astype(BF16)], axis=1)
    qa = qa_ref[...]

    row = lax.broadcasted_iota(jnp.int32, (2 * tq, MOBA_BLOCK), 0)
    col = lax.broadcasted_iota(jnp.int32, (2 * tq, MOBA_BLOCK), 1)
    causal = col <= row % tq + par * tq
    far_bias = jnp.where(row[:, 0:1] < tq, rb_ref[REL_BUCKETS - 1, 2 * p],
                         rb_ref[REL_BUCKETS - 1, 2 * p + 1])

    def logits(n, bias):
        n0 = pl.multiple_of(n * MOBA_BLOCK, MOBA_BLOCK)
        kaug = jnp.concatenate([k_ref[pl.ds(n0, MOBA_BLOCK), :], e_ref[pl.ds(n0, MOBA_BLOCK), :]],
                               axis=1)
        return _dot_nt(qa, kaug) + bias

    prev = jnp.maximum(own - 1, 0)
    n_far = prev

    def near_logits():
        s_own = logits(own, bias_ref[par, 0].reshape(2 * tq, MOBA_BLOCK))
        s_prev = logits(prev, bias_ref[par, 1].reshape(2 * tq, MOBA_BLOCK))
        return [jnp.where(causal, s_own, NEG_BIG), jnp.where(own >= 1, s_prev, NEG_BIG)]

    def far_logits(ns):
        return [logits(n, far_bias) for n in ns]

    def far_loop(step):
        def pair(g, carry):
            step([2 * g, 2 * g + 1])
            return carry

        lax.fori_loop(0, n_far // 2, pair, 0)

        @pl.when(n_far % 2 == 1)
        def _():
            step([n_far - 1])

    def fold(ss, op):
        out = None
        for s in ss:
            h = op(s[:, :LANES], s[:, LANES:])
            out = h if out is None else op(out, h)
        return out

    mrun_ref[...] = fold(near_logits(), jnp.maximum)

    def far_max(ns):
        mrun_ref[...] = jnp.maximum(mrun_ref[...], fold(far_logits(ns), jnp.maximum))

    far_loop(far_max)
    m = jnp.max(mrun_ref[...], axis=-1, keepdims=True)

    def weigh(ss, ns):
        pes = [jnp.exp(s - m) for s in ss]
        o = None
        for pe, n in zip(pes, ns):
            n0 = pl.multiple_of(n * MOBA_BLOCK, MOBA_BLOCK)
            on = _dot(pe.astype(BF16), v_ref[pl.ds(n0, MOBA_BLOCK), :])
            o = on if o is None else o + on
        return fold(pes, jnp.add), o

    lrun_ref[...], acc_ref[...] = weigh(near_logits(), [own, prev])

    def far_acc(ns):
        l, o = weigh(far_logits(ns), ns)
        lrun_ref[...] += l
        acc_ref[...] += o

    far_loop(far_acc)
    o = acc_ref[...] / jnp.sum(lrun_ref[...], axis=-1, keepdims=True)
    o_ref[...] = _unstack_heads(o, tq).astype(o_ref.dtype)


def moba_prompt_bias_idx(tq):
    per = MOBA_BLOCK // tq
    t = np.arange(tq)[:, None]
    s = np.arange(MOBA_BLOCK)[None, :]
    idx = np.stack([np.stack([_rel_bucket_np(par * tq + kind * MOBA_BLOCK + t - s)
                              for kind in range(2)]) for par in range(per)])
    return idx.reshape(per * 2, tq, MOBA_BLOCK)


def moba_prompt(qa, ka, va, kmean, rel_bias, bias, cbq, cbk, cbv, bsz, t_len, tq=128):
    n_q = t_len // tq
    n_hp = N_HEADS // 2
    per = MOBA_BLOCK // tq
    bias = bias.reshape(per, 2, N_HEADS, tq, MOBA_BLOCK)
    blk_of_row = np.arange(t_len) // MOBA_BLOCK
    onehot = jnp.asarray(blk_of_row[:, None] == np.arange(LANES)[None, :], BF16)
    return pl.pallas_call(
        functools.partial(_moba_prompt_body, tq=tq),
        grid=(bsz, n_hp, n_q),
        in_specs=[
            pl.BlockSpec(memory_space=pltpu.SMEM),
            pl.BlockSpec((tq, LANES), lambda b, p, i: (b * n_q + i, cbq + p)),
            pl.BlockSpec((t_len, LANES), lambda b, p, i: (b, cbk + p)),
            pl.BlockSpec((t_len, LANES), lambda b, p, i: (b, cbv + p)),
            pl.BlockSpec((t_len, LANES), lambda b, p, i: (0, 0)),
            pl.BlockSpec((None, LANES, LANES), lambda b, p, i: (b, 0, p)),
            pl.BlockSpec((per, 2, 2, tq, MOBA_BLOCK), lambda b, p, i: (0, 0, p, 0, 0)),
        ],
        out_specs=pl.BlockSpec((tq, LANES), lambda b, p, i: (b * n_q + i, p)),
        out_shape=jax.ShapeDtypeStruct((bsz * t_len, ATT_WIDTH), BF16),
        scratch_shapes=[pltpu.VMEM((2 * tq, 2 * LANES), BF16),
                        pltpu.VMEM((2 * tq, LANES), F32),
                        pltpu.VMEM((2 * tq, LANES), F32),
                        pltpu.VMEM((2 * tq, LANES), F32)],
        compiler_params=_cparams(("parallel", "parallel", "arbitrary")),
        name="moba_prompt",
    )(rel_bias, qa, ka, va, onehot, kmean, bias)


def _ssd_body(xbc_ref, dt_ref, cprev_ref, h0_ref, cw_ref, cb_ref, dtb_ref, alog_ref, dsk_ref,
              e_ref, et_ref, eye_ref, tril_ref, y_ref, h_ref, xf_ref, dtf_ref, *, q):
    c = pl.program_id(1)
    t_in = xbc_ref.shape[0]
    t_out = y_ref.shape[0]
    hp2 = 2 * SSM_P
    gw = (SSM_HEADS // SSM_GROUPS) * SSM_P

    @pl.when(c == 0)
    def _():
        xf_ref[0:SUBLANES, :] = cprev_ref[...]
        h_ref[...] = h0_ref[...]

    if t_in < q:
        xf_ref[SUBLANES:SUBLANES + q, :] = jnp.zeros((q, CONV_DIM), F32)
        dtf_ref[...] = jnp.zeros_like(dtf_ref)
    xf_ref[SUBLANES:SUBLANES + t_in, :] = xbc_ref[...]
    dtf_ref[0:t_in, :] = dt_ref[...]
    base = SUBLANES - (CONV_W - 1)
    conv = cb_ref[...]
    for j in range(CONV_W):
        conv = conv + xf_ref[base + j:base + j + q, :] * cw_ref[j:j + 1, :]
    xf_ref[0:SUBLANES, :] = xf_ref[q:q + SUBLANES, :]
    act = conv * _sigmoid(conv)
    xs = act[:, :SSM_INNER]

    dt = _softplus(dtf_ref[...] + dtb_ref[...])
    if t_in < q:
        trow = lax.broadcasted_iota(jnp.int32, dt.shape, 0)
        dt = jnp.where(trow < t_in, dt, 0.0)
    a = dt * (-jnp.exp(alog_ref[...]))
    a_hi, a_lo = _split2(a)
    tril = tril_ref[...]
    acs = _dot(tril, a_hi) + _dot(tril, a_lo)
    acs_t = _dot2_nt(eye_ref[...], acs)
    e = e_ref[...]
    xr = xs * _dot2(dt, e)
    eacs_e = _dot2(jnp.exp(acs), e)
    xd = xr * _dot2(jnp.exp(acs[q - 1:q, :] - acs), e)
    xd_t = xd.T
    cd = jnp.broadcast_to(jnp.exp(acs_t[:, q - 1:q]), (LANES, LANES))
    cd_hi, cd_lo = _split2(cd)
    et = et_ref[...]
    f = _dot(et, cd_hi) + _dot(et, cd_lo)
    dsk_e = _dot2(jnp.broadcast_to(dsk_ref[...], (SUBLANES, LANES)), e)[0:1, :]

    row = lax.broadcasted_iota(jnp.int32, (q, q), 0)
    col = lax.broadcasted_iota(jnp.int32, (q, q), 1)
    causal = col <= row
    lane = lax.broadcasted_iota(jnp.int32, (q, hp2), 1)
    hpg = SSM_HEADS // SSM_GROUPS
    for g in range(SSM_GROUPS):
        bg = act[:, SSM_INNER + g * SSM_N:SSM_INNER + (g + 1) * SSM_N].astype(BF16)
        cg = act[:, SSM_INNER + (SSM_GROUPS + g) * SSM_N:
                 SSM_INNER + (SSM_GROUPS + g + 1) * SSM_N].astype(BF16)
        cbm = _dot_nt(cg, bg)
        hg = h_ref[g * hpg:(g + 1) * hpg].reshape(gw, SSM_N)
        y_off = _dot_nt(cg, hg.astype(BF16)) * eacs_e[:, g * gw:(g + 1) * gw]
        for pr in range(hpg // 2):
            yd = []
            xr_pair = xr[:, g * gw + pr * hp2:g * gw + (pr + 1) * hp2].astype(BF16)
            for hh in range(2):
                h = g * hpg + pr * 2 + hh
                seg = acs[:, h:h + 1] - acs_t[h:h + 1, :]
                m = (cbm * jnp.where(causal, jnp.exp(seg), 0.0)).astype(BF16)
                yd.append(_dot(m, xr_pair))
            y_pair = (jnp.where(lane < SSM_P, yd[0], yd[1]) + y_off[:, pr * hp2:(pr + 1) * hp2]
                      + xs[:, g * gw + pr * hp2:g * gw + (pr + 1) * hp2]
                      * dsk_e[:, g * gw + pr * hp2:g * gw + (pr + 1) * hp2])
            y_ref[:, g * gw + pr * hp2:g * gw + (pr + 1) * hp2] = y_pair[:t_out]
        st = _dot(xd_t[g * gw:(g + 1) * gw, :].astype(BF16), bg)
        h_ref[g * hpg:(g + 1) * hpg] = (hg * f[g * gw:(g + 1) * gw, :] + st).reshape(hpg, SSM_P, SSM_N)


def _ssd_consts(q):
    hidx = np.arange(SSM_INNER) // SSM_P
    e = (np.arange(LANES)[:, None] == hidx[None, :]).astype(np.float32)
    return (jnp.asarray(e, BF16), jnp.asarray(e.T, BF16),
            jnp.asarray(np.eye(LANES, dtype=np.float32), BF16),
            jnp.asarray(np.tril(np.ones((q, q), np.float32)), BF16))


def ssd(xa, cbx, dta, cprev, h0, layer, cw, cb, dtb, alog, dsk, bsz, n_c, q=SSD_CHUNK):
    e, et, eye, tril = _ssd_consts(q)
    xblk = CONV_DIM // LANES
    t_in = xa.shape[1]
    t_out = q if t_in == q else SUBLANES
    const = lambda shape: pl.BlockSpec(shape, lambda b, c: (0,) * len(shape))
    return pl.pallas_call(
        functools.partial(_ssd_body, q=q),
        grid=(bsz, n_c),
        in_specs=[
            pl.BlockSpec((None, t_in, CONV_DIM), lambda b, c: (b * n_c + c, 0, cbx // xblk)),
            pl.BlockSpec((None, t_in, LANES), lambda b, c: (b * n_c + c, 0, 0)),
            pl.BlockSpec((None, SUBLANES, CONV_DIM), lambda b, c: (b, 0, 0)),
            pl.BlockSpec((None, None, SSM_HEADS, SSM_P, SSM_N), lambda b, c: (layer, b, 0, 0, 0)),
            const((CONV_W, CONV_DIM)), const((1, CONV_DIM)),
            const((1, LANES)), const((1, LANES)), const((1, LANES)),
            const((LANES, SSM_INNER)), const((SSM_INNER, LANES)), const((LANES, LANES)),
            const((q, q)),
        ],
        out_specs=[
            pl.BlockSpec((None, t_out, SSM_INNER), lambda b, c: (b * n_c + c, 0, 0)),
            pl.BlockSpec((None, SSM_HEADS, SSM_P, SSM_N), lambda b, c: (b, 0, 0, 0)),
        ],
        out_shape=[jax.ShapeDtypeStruct((bsz * n_c, t_out, SSM_INNER), F32),
                   jax.ShapeDtypeStruct((bsz, SSM_HEADS, SSM_P, SSM_N), F32)],
        scratch_shapes=[pltpu.VMEM((q + SUBLANES, CONV_DIM), F32), pltpu.VMEM((q, LANES), F32)],
        compiler_params=_cparams(("parallel", "arbitrary")),
        name="ssd",
    )(xa, dta, cprev, h0, cw, cb, dtb, alog, dsk, e, et, eye, tril)


def _merge_body(x_ref, osb_ref, omb_ref, y_ref, z_ref, g0_ref, g1_ref, g2_ref, gs_ref,
                wsb_ref, wmb_ref, wss_ref, wo_ref, o_ref):
    z = z_ref[...]
    y = _rms(y_ref[...] * (z * _sigmoid(z)), gs_ref[...]).astype(BF16)
    merged = (_sigmoid(g0_ref[...]) * _dot(osb_ref[...], wsb_ref[...])
              + _sigmoid(g1_ref[...]) * _dot(omb_ref[...], wmb_ref[...])
              + _sigmoid(g2_ref[...]) * _dot(y, wss_ref[...]))
    o_ref[...] = x_ref[...] + _dot(merged.astype(BF16), wo_ref[...])


def merge(x, osb, omb, y, u, gs, wsb, wmb, wss, wo, tm=512):
    n, d = x.shape
    db = d // LANES
    row = lambda cb, w: pl.BlockSpec((tm, w), lambda i: (i, cb))
    const = lambda a: pl.BlockSpec(a.shape, lambda i: (0, 0))
    return pl.pallas_call(
        _merge_body,
        grid=(n // tm,),
        in_specs=[row(0, d), row(0, ATT_WIDTH), row(0, ATT_WIDTH), row(0, d),
                  row(CB_Z // db, d), row(CB_GATE // db, d), row(CB_GATE // db + 1, d),
                  row(CB_GATE // db + 2, d),
                  const(gs), const(wsb), const(wmb), const(wss), const(wo)],
        out_specs=row(0, d),
        out_shape=jax.ShapeDtypeStruct((n, d), F32),
        compiler_params=_cparams(("parallel",)),
        name="merge",
    )(x, osb, omb, y, u, u, u, u, gs, wsb, wmb, wss, wo)


def _router_body(x_ref, g_ref, wr_ref, xn_ref, cmb_ref):
    xn = _rms(x_ref[...], g_ref[...])
    xn_ref[...] = xn.astype(BF16)
    wr = wr_ref[...]
    xh, xl = _split2(xn)
    wh, wl = _split2(wr[:-SUBLANES, :])
    logit = _dot(xh, wh) + _dot(xh, wl) + _dot(xl, wh) + wr[-SUBLANES:-SUBLANES + 1, :]
    lane = lax.broadcasted_iota(jnp.int32, logit.shape, 1)
    neg = -jnp.inf
    is_g = jnp.logical_and(lane >= N_EXPERTS, lane < N_EXPERTS + N_GROUPS)
    gl = jnp.where(is_g, logit, neg)
    gmax = jnp.max(gl, axis=-1, keepdims=True)
    grp = jnp.min(jnp.where(gl == gmax, lane, LANES), axis=-1, keepdims=True) - N_EXPERTS
    p_grp = 1.0 / jnp.sum(jnp.exp(gl - gmax), axis=-1, keepdims=True)
    in_grp = jnp.logical_and(lane >= grp * EPG, lane < grp * EPG + EPG)
    el = jnp.where(in_grp, logit, neg)
    v1 = jnp.max(el, axis=-1, keepdims=True)
    i1 = jnp.min(jnp.where(el == v1, lane, LANES), axis=-1, keepdims=True)
    el2 = jnp.where(lane == i1, neg, el)
    v2 = jnp.max(el2, axis=-1, keepdims=True)
    i2 = jnp.min(jnp.where(el2 == v2, lane, LANES), axis=-1, keepdims=True)
    d = jnp.exp(v2 - v1)
    w1 = p_grp / (1.0 + d)
    w2 = p_grp * d / (1.0 + d)
    cmb_ref[...] = jnp.where(lane == i1, w1, 0.0) + jnp.where(lane == i2, w2, 0.0)


def router(x, g, wr, tm=512):
    n, d = x.shape
    return pl.pallas_call(
        _router_body,
        grid=(n // tm,),
        in_specs=[pl.BlockSpec((tm, d), lambda i: (i, 0)),
                  pl.BlockSpec((1, d), lambda i: (0, 0)),
                  pl.BlockSpec(wr.shape, lambda i: (0, 0))],
        out_specs=[pl.BlockSpec((tm, d), lambda i: (i, 0)),
                   pl.BlockSpec((tm, LANES), lambda i: (i, 0))],
        out_shape=[jax.ShapeDtypeStruct((n, d), BF16),
                   jax.ShapeDtypeStruct((n, LANES), F32)],
        compiler_params=_cparams(("parallel",)),
        name="router",
    )(x, g, wr)


def _moe_body(x_ref, xn_ref, cmb_ref, wgu_ref, wd_ref, gf_ref, o_ref, *, final_norm):
    e = pl.program_id(1)

    @pl.when(e == 0)
    def _():
        o_ref[...] = x_ref[...]

    cmb = cmb_ref[...]
    lane = lax.broadcasted_iota(jnp.int32, cmb.shape, 1)
    c = jnp.sum(jnp.where(lane == e, cmb, 0.0), axis=-1, keepdims=True)
    gu = _dot(xn_ref[...], wgu_ref[...])
    gt = gu[:, :EXPERT_FF]
    hid = gt * _sigmoid(gt) * gu[:, EXPERT_FF:]
    o_ref[...] += _dot((c * hid).astype(BF16), wd_ref[...])

    if final_norm:
        @pl.when(e == N_EXPERTS - 1)
        def _():
            o_ref[...] = _rms(o_ref[...], gf_ref[...])


def moe(x, xn, cmb, wgu, wd, gf, final_norm, tm=512):
    n, d = x.shape
    return pl.pallas_call(
        functools.partial(_moe_body, final_norm=final_norm),
        grid=(n // tm, N_EXPERTS),
        in_specs=[pl.BlockSpec((tm, d), lambda i, e: (i, 0)),
                  pl.BlockSpec((tm, d), lambda i, e: (i, 0)),
                  pl.BlockSpec((tm, LANES), lambda i, e: (i, 0)),
                  pl.BlockSpec((None, d, 2 * EXPERT_FF), lambda i, e: (e, 0, 0)),
                  pl.BlockSpec((None, EXPERT_FF, d), lambda i, e: (e, 0, 0)),
                  pl.BlockSpec((1, d), lambda i, e: (0, 0))],
        out_specs=pl.BlockSpec((tm, d), lambda i, e: (i, 0)),
        out_shape=jax.ShapeDtypeStruct((n, d), F32),
        compiler_params=_cparams(("parallel", "arbitrary")),
        name="moe",
    )(x, xn, cmb, wgu, wd, gf)


T_PAD = SUBLANES
QROWS = N_HEADS * T_PAD


def _pad_rows(x_ref, buf):
    buf[0:T_PAD, :] = jnp.zeros((T_PAD, buf.shape[1]), buf.dtype)
    buf[0:x_ref.shape[0], :] = x_ref[...]
    return buf[0:T_PAD, :]


def _q_block_diag(q8):
    lane = lax.broadcasted_iota(jnp.int32, q8.shape, 1)
    return jnp.concatenate(
        [jnp.where(lane // HEAD_DIM == h, q8, 0.0) for h in range(N_HEADS)], axis=0)


def _out_from_acc_t(acc_t, gt):
    r = lax.broadcasted_iota(jnp.int32, acc_t.shape, 0)
    c = lax.broadcasted_iota(jnp.int32, acc_t.shape, 1)
    return _dot2_nt(gt, jnp.where(r // HEAD_DIM == c // T_PAD, acc_t, 0.0))


def _sample_consts():
    gt = (np.arange(T_PAD)[:, None] == (np.arange(QROWS) % T_PAD)[None, :]).astype(np.float32)
    return (_tri_ge_ones(PAGE), jnp.asarray(np.eye(PAGE, dtype=np.float32), BF16),
            jnp.asarray(gt, BF16))


def _sb_sample_body(pt_ref, q_ref, kn_ref, vn_ref, *rest, n_pages, t_new):
    k_refs = rest[:n_pages]
    v_refs = rest[n_pages:2 * n_pages]
    tri_ref, eye_ref, gt_ref, o_ref, kbuf, vbuf, acc_ref, car_ref, flag = rest[2 * n_pages:]
    scale = HEAD_DIM ** -0.5
    qbd = (_q_block_diag(_pad_rows(q_ref, kbuf)) * scale).astype(BF16)
    tri = tri_ref[...]
    eye = eye_ref[...]

    def page(kt, vt, mask):
        z = _dot(qbd, kt)
        lk = -_softplus(z)
        if mask is not None:
            lk = jnp.where(mask, lk, 0.0)
        sums = _suffix_sums(lk, tri)
        w = jnp.exp(z + sums[:, :PAGE] + car_ref[...])
        if mask is not None:
            w = jnp.where(mask, w, 0.0)
        w_t = _dot_nt(eye, w.astype(BF16)).astype(BF16)
        acc_ref[...] += _dot(vt, w_t)
        car = car_ref[...] + sums[:, PAGE:]
        car_ref[...] = car
        flag[0] = (jnp.max(car) >= EXP_ZERO).astype(jnp.int32)

    kbuf[...] = jnp.zeros_like(kbuf)
    vbuf[...] = jnp.zeros_like(vbuf)
    kbuf[0:t_new, :] = kn_ref[...]
    vbuf[0:t_new, :] = vn_ref[...]
    acc_ref[...] = jnp.zeros_like(acc_ref)
    car_ref[...] = jnp.zeros_like(car_ref)
    r = lax.broadcasted_iota(jnp.int32, (QROWS, PAGE), 0)
    c = lax.broadcasted_iota(jnp.int32, (QROWS, PAGE), 1)
    past = jnp.logical_and(c < r % T_PAD, c < t_new)
    page(kbuf[...].T.astype(BF16), vbuf[...].T.astype(BF16), past)

    for p in reversed(range(n_pages)):
        @pl.when(flag[0] > 0)
        def _():
            page(k_refs[p][...].reshape(ATT_WIDTH, PAGE).astype(BF16),
                 v_refs[p][...].reshape(ATT_WIDTH, PAGE).astype(BF16), None)

    o_ref[...] = _out_from_acc_t(acc_ref[...], gt_ref[...]).astype(o_ref.dtype)


def _page_specs(layer, n_pages):
    return [pl.BlockSpec((None, None, N_HEADS, HEAD_DIM, PAGE),
                         functools.partial(lambda b, pt, p: (layer, pt[b, p], 0, 0, 0), p=p))
            for p in range(n_pages)]


def sb_sample(page_table, us8, cache_k, cache_v, layer, t_new):
    n_seq, n_pages = page_table.shape
    tri, eye, gt = _sample_consts()
    tok = lambda cb: pl.BlockSpec((None, t_new, ATT_WIDTH), lambda b, pt: (b, 0, cb // 4))
    const = lambda a: pl.BlockSpec(a.shape, lambda b, pt: (0,) * a.ndim)
    grid_spec = pltpu.PrefetchScalarGridSpec(
        num_scalar_prefetch=1,
        grid=(n_seq,),
        in_specs=[tok(CB_Q_SB), tok(CB_K_SB), tok(CB_V_SB)]
        + _page_specs(layer, n_pages) + _page_specs(layer, n_pages)
        + [const(tri), const(eye), const(gt)],
        out_specs=pl.BlockSpec((None, T_PAD, ATT_WIDTH), lambda b, pt: (b, 0, 0)),
        scratch_shapes=[pltpu.VMEM((PAGE, ATT_WIDTH), F32), pltpu.VMEM((PAGE, ATT_WIDTH), F32),
                        pltpu.VMEM((ATT_WIDTH, QROWS), F32), pltpu.VMEM((QROWS, LANES), F32),
                        pltpu.SMEM((1,), jnp.int32)],
    )
    return pl.pallas_call(
        functools.partial(_sb_sample_body, n_pages=n_pages, t_new=t_new),
        grid_spec=grid_spec,
        out_shape=jax.ShapeDtypeStruct((n_seq, T_PAD, ATT_WIDTH), BF16),
        compiler_params=_cparams(("arbitrary",)),
        name="sb_sample",
    )(page_table, us8, us8, us8, *([cache_k] * n_pages), *([cache_v] * n_pages), tri, eye, gt)


def _moba_sample_body(pt_ref, q_ref, kn_ref, vn_ref, *rest, n_pages, t_new):
    k_refs = rest[:n_pages]
    v_refs = rest[n_pages:2 * n_pages]
    bias_ref, eye_ref, gt_ref, o_ref, kbuf, vbuf, s_ref = rest[2 * n_pages:]
    ppb = MOBA_BLOCK // PAGE
    n_blk = n_pages // ppb
    scale = HEAD_DIM ** -0.5
    qf = _q_block_diag(_pad_rows(q_ref, kbuf))
    qs = (qf * scale).astype(BF16)
    eye = eye_ref[...]
    neg = -jnp.inf
    lane = lax.broadcasted_iota(jnp.int32, (QROWS, LANES), 1)
    lane_w = lax.broadcasted_iota(jnp.int32, (ATT_WIDTH, LANES), 1)

    ksum = jnp.zeros((ATT_WIDTH, LANES), F32)
    for n in range(n_blk):
        blk = k_refs[ppb * n][...].reshape(ATT_WIDTH, PAGE)
        for j in range(1, ppb):
            blk = blk + k_refs[ppb * n + j][...].reshape(ATT_WIDTH, PAGE)
        ksum = jnp.where(lane_w == n, jnp.sum(blk, axis=-1, keepdims=True), ksum)
    kmean = ksum * (1.0 / MOBA_BLOCK)
    qh, ql = _split2(qf)
    mh, ml = _split2(kmean)
    route = _dot(qh, mh) + _dot(qh, ml) + _dot(ql, mh)
    sel = _route_topk(route, lane, n_blk, n_blk)

    m = jnp.full((QROWS, 1), neg, F32)
    for p in range(n_pages):
        s = _dot(qs, k_refs[p][...].reshape(ATT_WIDTH, PAGE).astype(BF16)) + bias_ref[p]
        s = jnp.where(sel[:, p // ppb:p // ppb + 1] > 0.5, s, neg)
        s_ref[p] = s
        m = jnp.maximum(m, jnp.max(s, axis=-1, keepdims=True))
    kbuf[...] = jnp.zeros_like(kbuf)
    vbuf[...] = jnp.zeros_like(vbuf)
    kbuf[0:t_new, :] = kn_ref[...]
    vbuf[0:t_new, :] = vn_ref[...]
    r = lax.broadcasted_iota(jnp.int32, (QROWS, PAGE), 0)
    c = lax.broadcasted_iota(jnp.int32, (QROWS, PAGE), 1)
    causal = jnp.logical_and(c <= r % T_PAD, c < T_PAD)
    s = _dot(qs, kbuf[...].T.astype(BF16)) + bias_ref[n_pages]
    s = jnp.where(causal, s, neg)
    s_ref[n_pages] = s
    m = jnp.maximum(m, jnp.max(s, axis=-1, keepdims=True))

    l = jnp.zeros((QROWS, 1), F32)
    acc_t = jnp.zeros((ATT_WIDTH, QROWS), F32)
    for p in range(n_pages + 1):
        pe = jnp.exp(s_ref[p] - m)
        l = l + jnp.sum(pe, axis=-1, keepdims=True)
        pe_t = _dot_nt(eye, pe.astype(BF16)).astype(BF16)
        vt = (v_refs[p][...].reshape(ATT_WIDTH, PAGE) if p < n_pages else vbuf[...].T).astype(BF16)
        acc_t = acc_t + _dot(vt, pe_t)
    avg = jnp.full((SUBLANES, LANES), 1.0 / LANES, BF16)
    l_row = _dot2_nt(avg, jnp.broadcast_to(l, (QROWS, LANES)))[0:1, :]
    o_ref[...] = _out_from_acc_t(acc_t / l_row, gt_ref[...]).astype(o_ref.dtype)


def moba_sample_bias_idx(n_pages):
    t = np.arange(T_PAD)[:, None]
    s = np.arange(PAGE)[None, :]
    past = [_rel_bucket_np((n_pages - p) * PAGE + t - s) for p in range(n_pages)]
    return np.stack(past + [_rel_bucket_np(t - s)])


def moba_sample(page_table, us8, cache_k, cache_v, bias, layer, t_new):
    n_seq, n_pages = page_table.shape
    _, eye, gt = _sample_consts()
    tok = lambda cb: pl.BlockSpec((None, t_new, ATT_WIDTH), lambda b, pt: (b, 0, cb // 4))
    const = lambda a: pl.BlockSpec(a.shape, lambda b, pt: (0,) * a.ndim)
    grid_spec = pltpu.PrefetchScalarGridSpec(
        num_scalar_prefetch=1,
        grid=(n_seq,),
        in_specs=[tok(CB_Q_MB), tok(CB_K_MB), tok(CB_V_MB)]
        + _page_specs(layer, n_pages) + _page_specs(layer, n_pages)
        + [const(bias), const(eye), const(gt)],
        out_specs=pl.BlockSpec((None, T_PAD, ATT_WIDTH), lambda b, pt: (b, 0, 0)),
        scratch_shapes=[pltpu.VMEM((PAGE, ATT_WIDTH), F32), pltpu.VMEM((PAGE, ATT_WIDTH), F32),
                        pltpu.VMEM((n_pages + 1, QROWS, PAGE), F32)],
    )
    return pl.pallas_call(
        functools.partial(_moba_sample_body, n_pages=n_pages, t_new=t_new),
        grid_spec=grid_spec,
        out_shape=jax.ShapeDtypeStruct((n_seq, T_PAD, ATT_WIDTH), BF16),
        compiler_params=_cparams(("arbitrary",)),
        name="moba_sample",
    )(page_table, us8, us8, us8, *([cache_k] * n_pages), *([cache_v] * n_pages), bias, eye, gt)


def _pad_lanes(v):
    return jnp.pad(v, (0, LANES - v.shape[0])).reshape(1, LANES)


def kernel(x_prompt, x_sample, cache_k_sb, cache_v_sb, cache_k_moba, cache_v_moba, page_table, state_ssm, state_conv, rel_bias, g_mix, w_in, conv_w, conv_b, dt_bias, a_log, d_skip, g_ssm, w_branch_sb, w_branch_moba, w_branch_ssm, w_out, g_ffn, w_router_group, b_router_group, w_router_expert, b_router_expert, w_expert_gate, w_expert_up, w_expert_down, g_final):
    bsz, seq, d = x_prompt.shape
    n_seq, t_new, _ = x_sample.shape
    depth = w_in.shape[0]
    n_pages = page_table.shape[1]
    n_p = bsz * seq
    n_s = n_seq * t_new
    assert d == D_MODEL and seq % MOBA_BLOCK == 0 and t_new <= T_PAD
    assert (n_pages * PAGE) % MOBA_BLOCK == 0 and cache_k_sb.shape[2] == PAGE

    x_p = x_prompt.reshape(n_p, d)
    x_s = x_sample.reshape(n_s, d)
    tm_p = 1024 if n_p % 1024 == 0 else 512
    tm_s = min(n_s, 512)
    n_c = seq // SSD_CHUNK
    pages = [c.transpose(0, 1, 3, 4, 2) for c in (cache_k_sb, cache_v_sb, cache_k_moba, cache_v_moba)]
    tq_mb = MOBA_BLOCK
    bias_p = bias_tiles(rel_bias, jnp.asarray(moba_prompt_bias_idx(tq_mb)))
    bias_s = bias_tiles(rel_bias, jnp.asarray(moba_sample_bias_idx(n_pages))).reshape(
        n_pages + 1, QROWS, PAGE)
    zero_conv = jnp.zeros((bsz, SUBLANES, CONV_DIM), F32)
    zero_state = jnp.zeros((1, bsz, SSM_HEADS, SSM_P, SSM_N), F32)
    n_main = CB_XBC * LANES + CONV_DIM

    def col(a, cb, w):
        return a[..., cb * LANES:cb * LANES + w]

    def kv(a, cb):
        return col(a, cb, ATT_WIDTH).reshape(a.shape[0], a.shape[1], N_HEADS, HEAD_DIM)

    new_p, new_s = [], []
    for l in range(depth):
        w_main = jnp.concatenate([w_in[l][:, :n_main], w_in[l][:, n_main + SSM_HEADS:]], axis=1).astype(BF16)
        w_dt = jnp.pad(w_in[l][:, n_main:n_main + SSM_HEADS], ((0, 0), (0, LANES - SSM_HEADS))).astype(BF16)
        gm = g_mix[l].reshape(1, d)
        ssm_par = (conv_w[l], conv_b[l].reshape(1, CONV_DIM), _pad_lanes(dt_bias[l]),
                   _pad_lanes(a_log[l]), _pad_lanes(d_skip[l]))
        wr = jnp.zeros((d + SUBLANES, LANES), F32)
        wr = wr.at[:d, :N_EXPERTS].set(w_router_expert[l]).at[:d, N_EXPERTS:N_EXPERTS + N_GROUPS].set(w_router_group[l])
        wr = wr.at[d, :N_EXPERTS].set(b_router_expert[l]).at[d, N_EXPERTS:N_EXPERTS + N_GROUPS].set(b_router_group[l])
        wgu = jnp.concatenate([w_expert_gate[l], w_expert_up[l]], axis=-1).astype(BF16)
        wd = w_expert_down[l].astype(BF16)
        w_br = (w_branch_sb[l].astype(BF16), w_branch_moba[l].astype(BF16),
                w_branch_ssm[l].astype(BF16), w_out[l].astype(BF16))

        def tail(x, osb, omb, y, u, tm):
            x1 = merge(x, osb, omb, y, u, g_ssm[l].reshape(1, d), *w_br, tm=min(tm, 512))
            xn, cmb = router(x1, g_ffn[l].reshape(1, d), wr, tm=min(tm, 512))
            return moe(x1, xn, cmb, wgu, wd, g_final.reshape(1, d), final_norm=(l == depth - 1), tm=tm)

        u_p, dt_p, qkv_p = in_proj(x_p, gm, w_main, w_dt, tm=tm_p)
        osb_p = sb_prompt(qkv_p, qkv_p, qkv_p, CB_Q_SB, CB_K_SB, CB_V_SB, bsz, seq)
        km = kmean_prompt(u_p, CB_K_MB, bsz, seq)
        omb_p = moba_prompt(u_p, qkv_p, qkv_p, km, rel_bias, bias_p, CB_Q_MB, CB_K_MB, CB_V_MB,
                            bsz, seq, tq=tq_mb)
        y_p, ssm_p = ssd(u_p.reshape(bsz * n_c, SSD_CHUNK, U_COLS), CB_XBC,
                         dt_p.reshape(bsz * n_c, SSD_CHUNK, LANES), zero_conv, zero_state, 0,
                         *ssm_par, bsz=bsz, n_c=n_c)
        x_p = tail(x_p, osb_p, omb_p, y_p.reshape(n_p, SSM_INNER), u_p, tm_p)

        u_s, dt_s, _ = in_proj(x_s, gm, w_main, w_dt, tm=tm_s)
        u_s3 = u_s.reshape(n_seq, t_new, U_COLS)
        osb_s = sb_sample(page_table, u_s3, pages[0], pages[1], l, t_new)[:, :t_new]
        omb_s = moba_sample(page_table, u_s3, pages[2], pages[3], bias_s, l, t_new)[:, :t_new]
        cprev_s = jnp.pad(state_conv[l], ((0, 0), (SUBLANES - (CONV_W - 1), 0), (0, 0)))
        y_s, ssm_s = ssd(u_s3, CB_XBC, dt_s.reshape(n_seq, t_new, LANES), cprev_s, state_ssm, l,
                         *ssm_par, bsz=n_seq, n_c=1)
        x_s = tail(x_s, osb_s.reshape(n_s, ATT_WIDTH), omb_s.reshape(n_s, ATT_WIDTH),
                   y_s[:, :t_new].reshape(n_s, SSM_INNER), u_s, tm_s)

        u_p3 = u_p.reshape(bsz, seq, U_COLS)
        new_p.append((kv(u_p3, CB_K_SB), kv(u_p3, CB_V_SB), kv(u_p3, CB_K_MB), kv(u_p3, CB_V_MB),
                      ssm_p, col(u_p3, CB_XBC, CONV_DIM)[:, seq - (CONV_W - 1):]))
        conv_s = jnp.concatenate([state_conv[l], col(u_s3, CB_XBC, CONV_DIM)], axis=1)[:, t_new:]
        new_s.append((kv(u_s3, CB_K_SB), kv(u_s3, CB_V_SB), kv(u_s3, CB_K_MB), kv(u_s3, CB_V_MB),
                      ssm_s, conv_s))

    stack = lambda states: tuple(jnp.stack([s[i] for s in states]) for i in range(6))
    return ((x_p.reshape(bsz, seq, d), x_s.reshape(n_seq, t_new, d)) + stack(new_p) + stack(new_s))
```

```python
import functools
import math

import numpy as np
import jax
import jax.numpy as jnp
from jax import lax
from jax.experimental import pallas as pl
from jax.experimental.pallas import tpu as pltpu

F32 = jnp.float32
BF16 = jnp.bfloat16

LANES = 128
SUBLANES = 8
VMEM_LIMIT = 56 * 1024 * 1024

D_MODEL = 1024
HEAD_DIM = 64
N_HEADS = 8
ATT_WIDTH = N_HEADS * HEAD_DIM
MOBA_BLOCK = 256
MOBA_TOPK = 3
PAGE = 128
SSM_HEADS = 16
SSM_P = 64
SSM_GROUPS = 4
SSM_N = 128
SSM_INNER = SSM_HEADS * SSM_P
CONV_W = 4
CONV_DIM = SSM_INNER + 2 * SSM_GROUPS * SSM_N
SSD_CHUNK = 128
REL_BUCKETS = 32
REL_MAX_DIST = 128
N_GROUPS = 4
EPG = 8
N_EXPERTS = N_GROUPS * EPG
EXPERT_FF = 256
EPS = 1e-6
EXP_ZERO = -88.0
NEG_BIG = -1e30

CB_Q_SB, CB_K_SB, CB_V_SB = 0, 4, 8
CB_Q_MB, CB_K_MB, CB_V_MB = 12, 16, 20
CB_Z = 24
CB_XBC = 32
CB_GATE = 48
U_COLS = 72 * LANES


def _cparams(sem):
    return pltpu.CompilerParams(dimension_semantics=sem, vmem_limit_bytes=VMEM_LIMIT)


def _split2(a):
    hi = a.astype(BF16)
    lo = (a - hi.astype(F32)).astype(BF16)
    return hi, lo


def _dot(a, b):
    return jnp.dot(a, b, preferred_element_type=F32)


def _dot_nt(a, b):
    return lax.dot_general(a, b, (((1,), (1,)), ((), ())), preferred_element_type=F32)


def _dot2(a, b_exact):
    hi, lo = _split2(a)
    return _dot(hi, b_exact) + _dot(lo, b_exact)


def _dot2_nt(a_exact, b):
    hi, lo = _split2(b)
    return _dot_nt(a_exact, hi) + _dot_nt(a_exact, lo)


def _dot3_nt(a, b):
    ah, al = _split2(a)
    bh, bl = _split2(b)
    return _dot_nt(ah, bh) + _dot_nt(ah, bl) + _dot_nt(al, bh)


def _sigmoid(x):
    return 1.0 / (1.0 + jnp.exp(-x))


def _softplus(x):
    return jnp.maximum(x, 0.0) + jnp.log(1.0 + jnp.exp(-jnp.abs(x)))


def _rms(x, g):
    ms = jnp.mean(x * x, axis=-1, keepdims=True)
    return x * lax.rsqrt(ms + EPS) * g


def _in_proj_body(x_ref, g_ref, w_ref, wdt_ref, u_ref, dt_ref, qkv_ref, xn_ref, *, n_qkv):
    j = pl.program_id(1)

    @pl.when(j == 0)
    def _():
        xn = _rms(x_ref[...], g_ref[...]).astype(BF16)
        xn_ref[...] = xn
        dt_ref[...] = _dot(xn, wdt_ref[...])

    u = _dot(xn_ref[...], w_ref[...])
    u_ref[...] = u

    @pl.when(j < n_qkv)
    def _():
        qkv_ref[...] = u.astype(BF16)


def in_proj(x, g, w, wdt, tm=512, tn=1024):
    n, d = x.shape
    cols = w.shape[1]
    n_qkv = (CB_Z * LANES) // tn
    return pl.pallas_call(
        functools.partial(_in_proj_body, n_qkv=n_qkv),
        grid=(n // tm, cols // tn),
        in_specs=[
            pl.BlockSpec((tm, d), lambda i, j: (i, 0)),
            pl.BlockSpec((1, d), lambda i, j: (0, 0)),
            pl.BlockSpec((d, tn), lambda i, j: (0, j)),
            pl.BlockSpec((d, LANES), lambda i, j: (0, 0)),
        ],
        out_specs=[
            pl.BlockSpec((tm, tn), lambda i, j: (i, j)),
            pl.BlockSpec((tm, LANES), lambda i, j: (i, 0)),
            pl.BlockSpec((tm, tn), lambda i, j: (i, jnp.minimum(j, n_qkv - 1))),
        ],
        out_shape=[jax.ShapeDtypeStruct((n, cols), F32),
                   jax.ShapeDtypeStruct((n, LANES), F32),
                   jax.ShapeDtypeStruct((n, n_qkv * tn), BF16)],
        scratch_shapes=[pltpu.VMEM((tm, d), BF16)],
        compiler_params=_cparams(("parallel", "arbitrary")),
        name="in_proj",
    )(x, g, w, wdt)


def _suffix_sums(lk, tri2):
    hi, lo = _split2(lk)
    return _dot(jnp.concatenate([hi, lo], axis=1), tri2)


def _sb_tile(qh, k, v, carry, tri2, mask):
    tk = k.shape[0]
    z = _dot_nt(qh, k)
    lk = -_softplus(z)
    if mask is not None:
        lk = jnp.where(mask, lk, 0.0)
    sums = _suffix_sums(lk, tri2)
    w = jnp.exp(z + sums[:, :tk] + carry)
    if mask is not None:
        w = jnp.where(mask, w, 0.0)
    return _dot(w.astype(BF16), v), carry + sums[:, tk:]


def _stack_heads(q, scale):
    lane = lax.broadcasted_iota(jnp.int32, q.shape, 1)
    q = q * scale
    return jnp.concatenate([jnp.where(lane < HEAD_DIM, q, 0.0),
                            jnp.where(lane >= HEAD_DIM, q, 0.0)], axis=0).astype(BF16)


def _unstack_heads(o, tq):
    lane = lax.broadcasted_iota(jnp.int32, (tq, LANES), 1)
    return jnp.where(lane < HEAD_DIM, o[:tq], o[tq:])


def _sb_prompt_body(q_ref, k_ref, v_ref, tri_ref, o_ref, acc_ref, car_ref, *, tk):
    i = pl.program_id(2)
    scale = HEAD_DIM ** -0.5
    q = q_ref[...].astype(F32)
    qa = jnp.concatenate([_stack_heads(q[:tk], scale), _stack_heads(q[tk:], scale)], axis=0)
    hb = 2 * tk
    tri2 = tri_ref[...]
    row = lax.broadcasted_iota(jnp.int32, (2 * hb, tk), 0)
    col = lax.broadcasted_iota(jnp.int32, (2 * hb, tk), 1)
    tri_mask = col < row % tk

    def tile(j, carry, mask):
        kj = pl.multiple_of(j * tk, tk)
        return _sb_tile(qa, k_ref[pl.ds(kj, tk), :], v_ref[pl.ds(kj, tk), :], carry, tri2, mask)

    tiles = [(2 * i + 1, qa[hb:], tri_mask[hb:]),
             (2 * i, qa, jnp.logical_or(row >= hb, tri_mask)),
             (jnp.maximum(2 * i - 1, 0), qa, jnp.broadcast_to(i > 0, (2 * hb, tk)))]
    starts = [pl.multiple_of(j * tk, tk) for j, _, _ in tiles]
    zs = [_dot_nt(qr, k_ref[pl.ds(s, tk), :]) for s, (_, qr, _) in zip(starts, tiles)]
    lks = [jnp.where(mask, -_softplus(z), 0.0) for z, (_, _, mask) in zip(zs, tiles)]
    sums = [_suffix_sums(lk, tri2) for lk in lks]
    ws = []
    c = jnp.zeros((hb, tk), F32)
    for t, (z, sm, (_, _, mask)) in enumerate(zip(zs, sums, tiles)):
        ws.append(jnp.where(mask, jnp.exp(z + sm[:, :tk] + c), 0.0).astype(BF16))
        c = c + sm[:, tk:]
        if t == 0:
            c = jnp.concatenate([jnp.zeros((hb, tk), F32), c], axis=0)
    os_ = [_dot(w, v_ref[pl.ds(s, tk), :]) for w, s in zip(ws, starts)]
    acc_ref[...] = os_[1] + os_[2]
    acc_ref[hb:, :] += os_[0]
    car_ref[...] = c

    def cond(j):
        return jnp.logical_and(j >= 0, jnp.max(car_ref[...]) >= EXP_ZERO)

    def body(j):
        o, c = tile(j, car_ref[...], None)
        acc_ref[...] += o
        car_ref[...] = c
        return j - 1

    lax.while_loop(cond, body, 2 * i - 2)
    o_ref[0:tk, :] = _unstack_heads(acc_ref[0:hb, :], tk).astype(o_ref.dtype)
    o_ref[tk:, :] = _unstack_heads(acc_ref[hb:, :], tk).astype(o_ref.dtype)


def _tri_ge_ones(n):
    t2 = np.concatenate([np.tril(np.ones((n, n), np.float32)), np.ones((n, n), np.float32)], axis=1)
    return jnp.asarray(np.concatenate([t2, t2], axis=0), BF16)


def sb_prompt(qa, ka, va, cbq, cbk, cbv, bsz, t_len, tk=128):
    tq = 2 * tk
    n_q = t_len // tq
    n_hp = N_HEADS // 2
    return pl.pallas_call(
        functools.partial(_sb_prompt_body, tk=tk),
        grid=(bsz, n_hp, n_q),
        in_specs=[
            pl.BlockSpec((tq, LANES), lambda b, p, i: (b * n_q + i, cbq + p)),
            pl.BlockSpec((t_len, LANES), lambda b, p, i: (b, cbk + p)),
            pl.BlockSpec((t_len, LANES), lambda b, p, i: (b, cbv + p)),
            pl.BlockSpec((2 * tk, 2 * tk), lambda b, p, i: (0, 0)),
        ],
        out_specs=pl.BlockSpec((tq, LANES), lambda b, p, i: (b * n_q + i, p)),
        out_shape=jax.ShapeDtypeStruct((bsz * t_len, ATT_WIDTH), BF16),
        scratch_shapes=[pltpu.VMEM((2 * tq, LANES), F32), pltpu.VMEM((2 * tq, tk), F32)],
        compiler_params=_cparams(("parallel", "parallel", "arbitrary")),
        name="sb_prompt",
    )(qa, ka, va, _tri_ge_ones(tk))


def _rel_bucket_np(dist):
    exact = REL_BUCKETS // 2
    d = np.maximum(dist, 0)
    df = np.maximum(d, 1).astype(np.float32)
    large = exact + (np.log(df / np.float32(exact)) / np.float32(math.log(REL_MAX_DIST / exact))
                     * np.float32(REL_BUCKETS - exact)).astype(np.int32)
    return np.where(d < exact, d, np.minimum(large, REL_BUCKETS - 1)).astype(np.int32)


def _bias_body(rb_ref, idx_ref, o_ref):
    h = pl.program_id(1)
    idx = idx_ref[...]
    acc = jnp.zeros(idx.shape, F32)
    for b in range(REL_BUCKETS):
        acc = jnp.where(idx == b, rb_ref[b, h], acc)
    o_ref[...] = acc


def bias_tiles(rel_bias, idx):
    nv, r, c = idx.shape
    return pl.pallas_call(
        _bias_body,
        grid=(nv, N_HEADS),
        in_specs=[pl.BlockSpec(memory_space=pltpu.SMEM),
                  pl.BlockSpec((None, r, c), lambda v, h: (v, 0, 0))],
        out_specs=pl.BlockSpec((None, None, r, c), lambda v, h: (v, h, 0, 0)),
        out_shape=jax.ShapeDtypeStruct((nv, N_HEADS, r, c), F32),
        compiler_params=_cparams(("parallel", "parallel")),
        name="bias_tiles",
    )(rel_bias, idx)


def _kmean_body(k_ref, o_ref):
    n = pl.program_id(1)

    @pl.when(n == 0)
    def _():
        o_ref[...] = jnp.zeros_like(o_ref)

    o_ref[pl.ds(n, 1), :] = jnp.mean(k_ref[...], axis=0, keepdims=True)


def kmean_prompt(ka, cbk, bsz, t_len):
    n_blk = t_len // MOBA_BLOCK
    return pl.pallas_call(
        _kmean_body,
        grid=(bsz, n_blk),
        in_specs=[pl.BlockSpec((MOBA_BLOCK, ATT_WIDTH), lambda b, n: (b * n_blk + n, cbk // 4))],
        out_specs=pl.BlockSpec((None, LANES, ATT_WIDTH), lambda b, n: (b, 0, 0)),
        out_shape=jax.ShapeDtypeStruct((bsz, LANES, ATT_WIDTH), F32),
        compiler_params=_cparams(("parallel", "arbitrary")),
        name="kmean_prompt",
    )(ka)


def _route_topk(route, lane, n_valid, n_ok):
    neg = -jnp.inf
    r = jnp.where(lane < n_valid, route, neg)
    sel = jnp.zeros(route.shape, F32)
    for j in range(MOBA_TOPK):
        m = jnp.max(r, axis=-1, keepdims=True)
        idx = jnp.min(jnp.where(r == m, lane, LANES), axis=-1, keepdims=True)
        hit = lane == idx
        sel = jnp.where(jnp.logical_and(hit, j < n_ok), 1.0, sel)
        r = jnp.where(hit, neg, r)
    return sel


def _block_penalty_t(route_t, own):
    blk = lax.broadcasted_iota(jnp.int32, route_t.shape, 0)
    neg = -jnp.inf
    r = jnp.where(blk < own, route_t, neg)
    keep = blk == own
    for j in range(MOBA_TOPK):
        m = jnp.max(r, axis=0, keepdims=True)
        idx = jnp.min(jnp.where(r == m, blk, route_t.shape[0]), axis=0, keepdims=True)
        hit = blk == idx
        keep = jnp.logical_or(keep, jnp.logical_and(hit, j < own))
        r = jnp.where(hit, neg, r)
    return jnp.where(keep, 0.0, NEG_BIG)


def _moba_prompt_body(rb_ref, q_ref, k_ref, v_ref, e_ref, eye_ref, km_ref, bias_ref, o_ref,
                      qa_ref, s_ref, mrun_ref, lrun_ref, acc_ref, *, tq, n_blk):
    p = pl.program_id(1)
    i = pl.program_id(2)
    per = MOBA_BLOCK // tq
    own = i // per
    par = i % per
    q = q_ref[...]
    lane = lax.broadcasted_iota(jnp.int32, q.shape, 1)
    km = km_ref[0:n_blk, :]
    pad = jnp.zeros((LANES - n_blk, tq), F32)
    pen = []
    for h in range(2):
        qf = jnp.where((lane < HEAD_DIM) if h == 0 else (lane >= HEAD_DIM), q, 0.0)
        pen_t = _block_penalty_t(_dot3_nt(km, qf), own)
        pen.append(_dot_nt(eye_ref[...], jnp.concatenate([pen_t, pad], axis=0).astype(BF16)))
    qa_ref[...] = jnp.concatenate(
        [_stack_heads(q, HEAD_DIM ** -0.5), jnp.concatenate(pen, axis=0).astype(BF16)], axis=1)
    qa = qa_ref[...]

    row = lax.broadcasted_iota(jnp.int32, (2 * tq, MOBA_BLOCK), 0)
    col = lax.broadcasted_iota(jnp.int32, (2 * tq, MOBA_BLOCK), 1)
    causal = col <= row % tq + par * tq
    far_bias = jnp.where(row[:, 0:1] < tq, rb_ref[REL_BUCKETS - 1, 2 * p],
                         rb_ref[REL_BUCKETS - 1, 2 * p + 1])

    def logits(n, bias):
        n0 = pl.multiple_of(n * MOBA_BLOCK, MOBA_BLOCK)
        kaug = jnp.concatenate([k_ref[pl.ds(n0, MOBA_BLOCK), :], e_ref[pl.ds(n0, MOBA_BLOCK), :]],
                               axis=1)
        return _dot_nt(qa, kaug) + bias

    prev = jnp.maximum(own - 1, 0)
    prev_slot = jnp.where(own >= 1, own - 1, n_blk)
    n_far = prev

    def far_loop(step, group):
        def grouped(g, carry):
            step([group * g + j for j in range(group)])
            return carry

        lax.fori_loop(0, n_far // group, grouped, 0)
        base = (n_far // group) * group
        size = group // 2
        while size >= 1:
            take = (n_far - base) >= size

            @pl.when(take)
            def _(base=base, size=size):
                step([base + j for j in range(size)])

            base = base + jnp.where(take, size, 0)
            size //= 2

    def fold(ss, op):
        out = None
        for s in ss:
            h = op(s[:, :LANES], s[:, LANES:])
            out = h if out is None else op(out, h)
        return out

    s_own = jnp.where(causal, logits(own, bias_ref[par, 0].reshape(2 * tq, MOBA_BLOCK)), NEG_BIG)
    s_prev = jnp.where(own >= 1, logits(prev, bias_ref[par, 1].reshape(2 * tq, MOBA_BLOCK)), NEG_BIG)
    s_ref[own] = s_own
    s_ref[prev_slot] = s_prev
    mrun_ref[...] = fold([s_own, s_prev], jnp.maximum)

    def far_max(ns):
        ss = [logits(n, far_bias) for n in ns]
        for n, s in zip(ns, ss):
            s_ref[n] = s
        mrun_ref[...] = jnp.maximum(mrun_ref[...], fold(ss, jnp.maximum))

    far_loop(far_max, 4)
    m = jnp.max(mrun_ref[...], axis=-1, keepdims=True)

    def weigh(slots, ns):
        pes = [jnp.exp(s_ref[sl] - m) for sl in slots]
        o = None
        for pe, n in zip(pes, ns):
            n0 = pl.multiple_of(n * MOBA_BLOCK, MOBA_BLOCK)
            on = _dot(pe.astype(BF16), v_ref[pl.ds(n0, MOBA_BLOCK), :])
            o = on if o is None else o + on
        return fold(pes, jnp.add), o

    lrun_ref[...], acc_ref[...] = weigh([own, prev_slot], [own, prev])

    def far_acc(ns):
        l, o = weigh(ns, ns)
        lrun_ref[...] += l
        acc_ref[...] += o

    far_loop(far_acc, 4)
    o = acc_ref[...] / jnp.sum(lrun_ref[...], axis=-1, keepdims=True)
    o_ref[...] = _unstack_heads(o, tq).astype(o_ref.dtype)


def moba_prompt_bias_idx(tq):
    per = MOBA_BLOCK // tq
    t = np.arange(tq)[:, None]
    s = np.arange(MOBA_BLOCK)[None, :]
    idx = np.stack([np.stack([_rel_bucket_np(par * tq + kind * MOBA_BLOCK + t - s)
                              for kind in range(2)]) for par in range(per)])
    return idx.reshape(per * 2, tq, MOBA_BLOCK)


def moba_prompt(qa, ka, va, kmean, rel_bias, bias, cbq, cbk, cbv, bsz, t_len, tq=128):
    n_q = t_len // tq
    n_hp = N_HEADS // 2
    per = MOBA_BLOCK // tq
    bias = bias.reshape(per, 2, N_HEADS, tq, MOBA_BLOCK)
    n_blk = t_len // MOBA_BLOCK
    assert n_blk <= LANES
    blk_of_row = np.arange(t_len) // MOBA_BLOCK
    onehot = jnp.asarray(blk_of_row[:, None] == np.arange(LANES)[None, :], BF16)
    eye = jnp.asarray(np.eye(tq, dtype=np.float32), BF16)
    return pl.pallas_call(
        functools.partial(_moba_prompt_body, tq=tq, n_blk=n_blk),
        grid=(bsz, n_hp, n_q),
        in_specs=[
            pl.BlockSpec(memory_space=pltpu.SMEM),
            pl.BlockSpec((tq, LANES), lambda b, p, i: (b * n_q + i, cbq + p)),
            pl.BlockSpec((t_len, LANES), lambda b, p, i: (b, cbk + p)),
            pl.BlockSpec((t_len, LANES), lambda b, p, i: (b, cbv + p)),
            pl.BlockSpec((t_len, LANES), lambda b, p, i: (0, 0)),
            pl.BlockSpec((tq, tq), lambda b, p, i: (0, 0)),
            pl.BlockSpec((None, LANES, LANES), lambda b, p, i: (b, 0, p)),
            pl.BlockSpec((per, 2, 2, tq, MOBA_BLOCK), lambda b, p, i: (0, 0, p, 0, 0)),
        ],
        out_specs=pl.BlockSpec((tq, LANES), lambda b, p, i: (b * n_q + i, p)),
        out_shape=jax.ShapeDtypeStruct((bsz * t_len, ATT_WIDTH), BF16),
        scratch_shapes=[pltpu.VMEM((2 * tq, 2 * LANES), BF16),
                        pltpu.VMEM((n_blk + 1, 2 * tq, MOBA_BLOCK), F32),
                        pltpu.VMEM((2 * tq, LANES), F32),
                        pltpu.VMEM((2 * tq, LANES), F32),
                        pltpu.VMEM((2 * tq, LANES), F32)],
        compiler_params=_cparams(("parallel", "parallel", "arbitrary")),
        name="moba_prompt",
    )(rel_bias, qa, ka, va, onehot, eye, kmean, bias)


def _ssd_body(xbc_ref, dt_ref, cprev_ref, h0_ref, cw_ref, cb_ref, dtb_ref, alog_ref, dsk_ref,
              e_ref, et_ref, eye_ref, tril_ref, y_ref, h_ref, xf_ref, dtf_ref, *, q):
    c = pl.program_id(1)
    t_in = xbc_ref.shape[0]
    t_out = y_ref.shape[0]
    hp2 = 2 * SSM_P
    gw = (SSM_HEADS // SSM_GROUPS) * SSM_P

    @pl.when(c == 0)
    def _():
        xf_ref[0:SUBLANES, :] = cprev_ref[...]
        h_ref[...] = h0_ref[...]

    if t_in < q:
        xf_ref[SUBLANES:SUBLANES + q, :] = jnp.zeros((q, CONV_DIM), F32)
        dtf_ref[...] = jnp.zeros_like(dtf_ref)
    xf_ref[SUBLANES:SUBLANES + t_in, :] = xbc_ref[...]
    dtf_ref[0:t_in, :] = dt_ref[...]
    base = SUBLANES - (CONV_W - 1)
    conv = cb_ref[...]
    for j in range(CONV_W):
        conv = conv + xf_ref[base + j:base + j + q, :] * cw_ref[j:j + 1, :]
    xf_ref[0:SUBLANES, :] = xf_ref[q:q + SUBLANES, :]
    act = conv * _sigmoid(conv)
    xs = act[:, :SSM_INNER]

    dt = _softplus(dtf_ref[...] + dtb_ref[...])
    if t_in < q:
        trow = lax.broadcasted_iota(jnp.int32, dt.shape, 0)
        dt = jnp.where(trow < t_in, dt, 0.0)
    a = dt * (-jnp.exp(alog_ref[...]))
    a_hi, a_lo = _split2(a)
    tril = tril_ref[...]
    acs = _dot(tril, a_hi) + _dot(tril, a_lo)
    acs_t = _dot2_nt(eye_ref[...], acs)
    e = e_ref[...]
    xr = xs * _dot2(dt, e)
    eacs_e = _dot2(jnp.exp(acs), e)
    xd = xr * _dot2(jnp.exp(acs[q - 1:q, :] - acs), e)
    xd_t = xd.T
    cd = jnp.broadcast_to(jnp.exp(acs_t[:, q - 1:q]), (LANES, LANES))
    cd_hi, cd_lo = _split2(cd)
    et = et_ref[...]
    f = _dot(et, cd_hi) + _dot(et, cd_lo)
    dsk_e = _dot2(jnp.broadcast_to(dsk_ref[...], (SUBLANES, LANES)), e)[0:1, :]

    row = lax.broadcasted_iota(jnp.int32, (q, q), 0)
    col = lax.broadcasted_iota(jnp.int32, (q, q), 1)
    causal = col <= row
    lane = lax.broadcasted_iota(jnp.int32, (q, hp2), 1)
    hpg = SSM_HEADS // SSM_GROUPS
    for g in range(SSM_GROUPS):
        bg = act[:, SSM_INNER + g * SSM_N:SSM_INNER + (g + 1) * SSM_N].astype(BF16)
        cg = act[:, SSM_INNER + (SSM_GROUPS + g) * SSM_N:
                 SSM_INNER + (SSM_GROUPS + g + 1) * SSM_N].astype(BF16)
        cbm = _dot_nt(cg, bg)
        hg = h_ref[g * hpg:(g + 1) * hpg].reshape(gw, SSM_N)
        y_off = _dot_nt(cg, hg.astype(BF16)) * eacs_e[:, g * gw:(g + 1) * gw]
        for pr in range(hpg // 2):
            yd = []
            xr_pair = xr[:, g * gw + pr * hp2:g * gw + (pr + 1) * hp2].astype(BF16)
            for hh in range(2):
                h = g * hpg + pr * 2 + hh
                seg = acs[:, h:h + 1] - acs_t[h:h + 1, :]
                m = (cbm * jnp.where(causal, jnp.exp(seg), 0.0)).astype(BF16)
                yd.append(_dot(m, xr_pair))
            y_pair = (jnp.where(lane < SSM_P, yd[0], yd[1]) + y_off[:, pr * hp2:(pr + 1) * hp2]
                      + xs[:, g * gw + pr * hp2:g * gw + (pr + 1) * hp2]
                      * dsk_e[:, g * gw + pr * hp2:g * gw + (pr + 1) * hp2])
            y_ref[:, g * gw + pr * hp2:g * gw + (pr + 1) * hp2] = y_pair[:t_out]
        st = _dot(xd_t[g * gw:(g + 1) * gw, :].astype(BF16), bg)
        h_ref[g * hpg:(g + 1) * hpg] = (hg * f[g * gw:(g + 1) * gw, :] + st).reshape(hpg, SSM_P, SSM_N)


def _ssd_consts(q):
    hidx = np.arange(SSM_INNER) // SSM_P
    e = (np.arange(LANES)[:, None] == hidx[None, :]).astype(np.float32)
    return (jnp.asarray(e, BF16), jnp.asarray(e.T, BF16),
            jnp.asarray(np.eye(LANES, dtype=np.float32), BF16),
            jnp.asarray(np.tril(np.ones((q, q), np.float32)), BF16))


def ssd(xa, cbx, dta, cprev, h0, layer, cw, cb, dtb, alog, dsk, bsz, n_c, q=SSD_CHUNK):
    e, et, eye, tril = _ssd_consts(q)
    xblk = CONV_DIM // LANES
    t_in = xa.shape[1]
    t_out = q if t_in == q else SUBLANES
    const = lambda shape: pl.BlockSpec(shape, lambda b, c: (0,) * len(shape))
    return pl.pallas_call(
        functools.partial(_ssd_body, q=q),
        grid=(bsz, n_c),
        in_specs=[
            pl.BlockSpec((None, t_in, CONV_DIM), lambda b, c: (b * n_c + c, 0, cbx // xblk)),
            pl.BlockSpec((None, t_in, LANES), lambda b, c: (b * n_c + c, 0, 0)),
            pl.BlockSpec((None, SUBLANES, CONV_DIM), lambda b, c: (b, 0, 0)),
            pl.BlockSpec((None, None, SSM_HEADS, SSM_P, SSM_N), lambda b, c: (layer, b, 0, 0, 0)),
            const((CONV_W, CONV_DIM)), const((1, CONV_DIM)),
            const((1, LANES)), const((1, LANES)), const((1, LANES)),
            const((LANES, SSM_INNER)), const((SSM_INNER, LANES)), const((LANES, LANES)),
            const((q, q)),
        ],
        out_specs=[
            pl.BlockSpec((None, t_out, SSM_INNER), lambda b, c: (b * n_c + c, 0, 0)),
            pl.BlockSpec((None, SSM_HEADS, SSM_P, SSM_N), lambda b, c: (b, 0, 0, 0)),
        ],
        out_shape=[jax.ShapeDtypeStruct((bsz * n_c, t_out, SSM_INNER), F32),
                   jax.ShapeDtypeStruct((bsz, SSM_HEADS, SSM_P, SSM_N), F32)],
        scratch_shapes=[pltpu.VMEM((q + SUBLANES, CONV_DIM), F32), pltpu.VMEM((q, LANES), F32)],
        compiler_params=_cparams(("parallel", "arbitrary")),
        name="ssd",
    )(xa, dta, cprev, h0, cw, cb, dtb, alog, dsk, e, et, eye, tril)


def _merge_body(x_ref, osb_ref, omb_ref, y_ref, z_ref, g0_ref, g1_ref, g2_ref, gs_ref,
                wsb_ref, wmb_ref, wss_ref, wo_ref, o_ref):
    z = z_ref[...]
    y = _rms(y_ref[...] * (z * _sigmoid(z)), gs_ref[...]).astype(BF16)
    merged = (_sigmoid(g0_ref[...]) * _dot(osb_ref[...], wsb_ref[...])
              + _sigmoid(g1_ref[...]) * _dot(omb_ref[...], wmb_ref[...])
              + _sigmoid(g2_ref[...]) * _dot(y, wss_ref[...]))
    o_ref[...] = x_ref[...] + _dot(merged.astype(BF16), wo_ref[...])


def merge(x, osb, omb, y, u, gs, wsb, wmb, wss, wo, tm=512):
    n, d = x.shape
    db = d // LANES
    row = lambda cb, w: pl.BlockSpec((tm, w), lambda i: (i, cb))
    const = lambda a: pl.BlockSpec(a.shape, lambda i: (0, 0))
    return pl.pallas_call(
        _merge_body,
        grid=(n // tm,),
        in_specs=[row(0, d), row(0, ATT_WIDTH), row(0, ATT_WIDTH), row(0, d),
                  row(CB_Z // db, d), row(CB_GATE // db, d), row(CB_GATE // db + 1, d),
                  row(CB_GATE // db + 2, d),
                  const(gs), const(wsb), const(wmb), const(wss), const(wo)],
        out_specs=row(0, d),
        out_shape=jax.ShapeDtypeStruct((n, d), F32),
        compiler_params=_cparams(("parallel",)),
        name="merge",
    )(x, osb, omb, y, u, u, u, u, gs, wsb, wmb, wss, wo)


def _router_body(x_ref, g_ref, wr_ref, xn_ref, cmb_ref):
    xn = _rms(x_ref[...], g_ref[...])
    xn_ref[...] = xn.astype(BF16)
    wr = wr_ref[...]
    xh, xl = _split2(xn)
    wh, wl = _split2(wr[:-SUBLANES, :])
    logit = _dot(xh, wh) + _dot(xh, wl) + _dot(xl, wh) + wr[-SUBLANES:-SUBLANES + 1, :]
    lane = lax.broadcasted_iota(jnp.int32, logit.shape, 1)
    neg = -jnp.inf
    is_g = jnp.logical_and(lane >= N_EXPERTS, lane < N_EXPERTS + N_GROUPS)
    gl = jnp.where(is_g, logit, neg)
    gmax = jnp.max(gl, axis=-1, keepdims=True)
    grp = jnp.min(jnp.where(gl == gmax, lane, LANES), axis=-1, keepdims=True) - N_EXPERTS
    p_grp = 1.0 / jnp.sum(jnp.exp(gl - gmax), axis=-1, keepdims=True)
    in_grp = jnp.logical_and(lane >= grp * EPG, lane < grp * EPG + EPG)
    el = jnp.where(in_grp, logit, neg)
    v1 = jnp.max(el, axis=-1, keepdims=True)
    i1 = jnp.min(jnp.where(el == v1, lane, LANES), axis=-1, keepdims=True)
    el2 = jnp.where(lane == i1, neg, el)
    v2 = jnp.max(el2, axis=-1, keepdims=True)
    i2 = jnp.min(jnp.where(el2 == v2, lane, LANES), axis=-1, keepdims=True)
    d = jnp.exp(v2 - v1)
    w1 = p_grp / (1.0 + d)
    w2 = p_grp * d / (1.0 + d)
    cmb_ref[...] = jnp.where(lane == i1, w1, 0.0) + jnp.where(lane == i2, w2, 0.0)


def router(x, g, wr, tm=512):
    n, d = x.shape
    return pl.pallas_call(
        _router_body,
        grid=(n // tm,),
        in_specs=[pl.BlockSpec((tm, d), lambda i: (i, 0)),
                  pl.BlockSpec((1, d), lambda i: (0, 0)),
                  pl.BlockSpec(wr.shape, lambda i: (0, 0))],
        out_specs=[pl.BlockSpec((tm, d), lambda i: (i, 0)),
                   pl.BlockSpec((tm, LANES), lambda i: (i, 0))],
        out_shape=[jax.ShapeDtypeStruct((n, d), BF16),
                   jax.ShapeDtypeStruct((n, LANES), F32)],
        compiler_params=_cparams(("parallel",)),
        name="router",
    )(x, g, wr)


def _moe_body(x_ref, xn_ref, cmb_ref, wgu_ref, wd_ref, gf_ref, o_ref, *, final_norm):
    e = pl.program_id(1)

    @pl.when(e == 0)
    def _():
        o_ref[...] = x_ref[...]

    cmb = cmb_ref[...]
    lane = lax.broadcasted_iota(jnp.int32, cmb.shape, 1)
    c = jnp.sum(jnp.where(lane == e, cmb, 0.0), axis=-1, keepdims=True)
    gu = _dot(xn_ref[...], wgu_ref[...])
    gt = gu[:, :EXPERT_FF]
    hid = gt * _sigmoid(gt) * gu[:, EXPERT_FF:]
    o_ref[...] += _dot((c * hid).astype(BF16), wd_ref[...])

    if final_norm:
        @pl.when(e == N_EXPERTS - 1)
        def _():
            o_ref[...] = _rms(o_ref[...], gf_ref[...])


def moe(x, xn, cmb, wgu, wd, gf, final_norm, tm=512):
    n, d = x.shape
    return pl.pallas_call(
        functools.partial(_moe_body, final_norm=final_norm),
        grid=(n // tm, N_EXPERTS),
        in_specs=[pl.BlockSpec((tm, d), lambda i, e: (i, 0)),
                  pl.BlockSpec((tm, d), lambda i, e: (i, 0)),
                  pl.BlockSpec((tm, LANES), lambda i, e: (i, 0)),
                  pl.BlockSpec((None, d, 2 * EXPERT_FF), lambda i, e: (e, 0, 0)),
                  pl.BlockSpec((None, EXPERT_FF, d), lambda i, e: (e, 0, 0)),
                  pl.BlockSpec((1, d), lambda i, e: (0, 0))],
        out_specs=pl.BlockSpec((tm, d), lambda i, e: (i, 0)),
        out_shape=jax.ShapeDtypeStruct((n, d), F32),
        compiler_params=_cparams(("parallel", "arbitrary")),
        name="moe",
    )(x, xn, cmb, wgu, wd, gf)


T_PAD = SUBLANES
QROWS = N_HEADS * T_PAD


def _pad_rows(x_ref, buf):
    buf[0:T_PAD, :] = jnp.zeros((T_PAD, buf.shape[1]), buf.dtype)
    buf[0:x_ref.shape[0], :] = x_ref[...]
    return buf[0:T_PAD, :]


def _q_block_diag(q8):
    lane = lax.broadcasted_iota(jnp.int32, q8.shape, 1)
    return jnp.concatenate(
        [jnp.where(lane // HEAD_DIM == h, q8, 0.0) for h in range(N_HEADS)], axis=0)


def _out_from_acc(acc):
    lane = lax.broadcasted_iota(jnp.int32, (T_PAD, ATT_WIDTH), 1)
    out = jnp.zeros((T_PAD, ATT_WIDTH), F32)
    for h in range(N_HEADS):
        out = jnp.where(lane // HEAD_DIM == h, acc[h * T_PAD:(h + 1) * T_PAD, :], out)
    return out


N_AHEAD = 2


def _sb_sample_body(pt_ref, q_ref, kn_ref, vn_ref, ck_ref, cv_ref, tri_ref, o_ref,
                    kbuf, vbuf, acc_ref, car_ref, flag, kpg, vpg, kx, vx, sem_pg, sem_x,
                    *, n_pages, t_new, layer):
    b = pl.program_id(0)
    n_seq = pl.num_programs(0)
    slot = b % 2
    scale = HEAD_DIM ** -0.5
    qbd = (_q_block_diag(_pad_rows(q_ref, kbuf)) * scale).astype(BF16)
    tri = tri_ref[...]

    def ahead_copies(seq, sl):
        cps = []
        for j in range(N_AHEAD):
            pg = pt_ref[seq, n_pages - 1 - j]
            cps.append(pltpu.make_async_copy(ck_ref.at[layer, pg], kpg.at[sl, j], sem_pg.at[sl, j, 0]))
            cps.append(pltpu.make_async_copy(cv_ref.at[layer, pg], vpg.at[sl, j], sem_pg.at[sl, j, 1]))
        return cps

    def demand_copies(p):
        pg = pt_ref[b, p]
        return [pltpu.make_async_copy(ck_ref.at[layer, pg], kx, sem_x.at[0]),
                pltpu.make_async_copy(cv_ref.at[layer, pg], vx, sem_x.at[1])]

    @pl.when(b == 0)
    def _():
        for cp in ahead_copies(0, 0):
            cp.start()

    @pl.when(b + 1 < n_seq)
    def _():
        for cp in ahead_copies(b + 1, 1 - slot):
            cp.start()

    def page(z, weigh, mask):
        lk = -_softplus(z)
        if mask is not None:
            lk = jnp.where(mask, lk, 0.0)
        sums = _suffix_sums(lk, tri)
        w = jnp.exp(z + sums[:, :PAGE] + car_ref[...])
        if mask is not None:
            w = jnp.where(mask, w, 0.0)
        acc_ref[...] += weigh(w.astype(BF16))
        car = car_ref[...] + sums[:, PAGE:]
        car_ref[...] = car
        flag[0] = (jnp.max(car) >= EXP_ZERO).astype(jnp.int32)

    kbuf[...] = jnp.zeros_like(kbuf)
    vbuf[...] = jnp.zeros_like(vbuf)
    kbuf[0:t_new, :] = kn_ref[...]
    vbuf[0:t_new, :] = vn_ref[...]
    acc_ref[...] = jnp.zeros_like(acc_ref)
    car_ref[...] = jnp.zeros_like(car_ref)
    r = lax.broadcasted_iota(jnp.int32, (QROWS, PAGE), 0)
    c = lax.broadcasted_iota(jnp.int32, (QROWS, PAGE), 1)
    past = jnp.logical_and(c < r % T_PAD, c < t_new)
    page(_dot_nt(qbd, kbuf[...].astype(BF16)), lambda w: _dot(w, vbuf[...].astype(BF16)), past)

    def cache_page(k_page, v_page):
        vt = v_page.reshape(ATT_WIDTH, PAGE).astype(BF16)
        page(_dot(qbd, k_page.reshape(ATT_WIDTH, PAGE).astype(BF16)), lambda w: _dot_nt(w, vt), None)

    for cp in ahead_copies(b, slot):
        cp.wait()
    for j in range(N_AHEAD):
        @pl.when(flag[0] > 0)
        def _():
            cache_page(kpg[slot, j], vpg[slot, j])

    for p in reversed(range(n_pages - N_AHEAD)):
        @pl.when(flag[0] > 0)
        def _():
            cps = demand_copies(p)
            for cp in cps:
                cp.start()
            for cp in cps:
                cp.wait()
            cache_page(kx[...], vx[...])

    o_ref[...] = _out_from_acc(acc_ref[...]).astype(o_ref.dtype)


def _page_specs(layer, n_pages):
    return [pl.BlockSpec((None, None, N_HEADS, HEAD_DIM, PAGE),
                         functools.partial(lambda b, pt, p: (layer, pt[b, p], 0, 0, 0), p=p))
            for p in range(n_pages)]


def sb_sample(page_table, us8, cache_k, cache_v, layer, t_new):
    n_seq, n_pages = page_table.shape
    assert n_pages >= N_AHEAD
    tri = _tri_ge_ones(PAGE)
    tok = lambda cb: pl.BlockSpec((None, t_new, ATT_WIDTH), lambda b, pt: (b, 0, cb // 4))
    const = lambda a: pl.BlockSpec(a.shape, lambda b, pt: (0,) * a.ndim)
    page_shape = (N_HEADS, HEAD_DIM, PAGE)
    grid_spec = pltpu.PrefetchScalarGridSpec(
        num_scalar_prefetch=1,
        grid=(n_seq,),
        in_specs=[tok(CB_Q_SB), tok(CB_K_SB), tok(CB_V_SB),
                  pl.BlockSpec(memory_space=pl.ANY), pl.BlockSpec(memory_space=pl.ANY),
                  const(tri)],
        out_specs=pl.BlockSpec((None, T_PAD, ATT_WIDTH), lambda b, pt: (b, 0, 0)),
        scratch_shapes=[pltpu.VMEM((PAGE, ATT_WIDTH), F32), pltpu.VMEM((PAGE, ATT_WIDTH), F32),
                        pltpu.VMEM((QROWS, ATT_WIDTH), F32), pltpu.VMEM((QROWS, LANES), F32),
                        pltpu.SMEM((1,), jnp.int32),
                        pltpu.VMEM((2, N_AHEAD) + page_shape, F32),
                        pltpu.VMEM((2, N_AHEAD) + page_shape, F32),
                        pltpu.VMEM(page_shape, F32), pltpu.VMEM(page_shape, F32),
                        pltpu.SemaphoreType.DMA((2, N_AHEAD, 2)), pltpu.SemaphoreType.DMA((2,))],
    )
    return pl.pallas_call(
        functools.partial(_sb_sample_body, n_pages=n_pages, t_new=t_new, layer=layer),
        grid_spec=grid_spec,
        out_shape=jax.ShapeDtypeStruct((n_seq, T_PAD, ATT_WIDTH), BF16),
        compiler_params=_cparams(("arbitrary",)),
        name="sb_sample",
    )(page_table, us8, us8, us8, cache_k, cache_v, tri)


def _moba_sample_body(pt_ref, q_ref, kn_ref, vn_ref, *rest, n_pages, t_new):
    k_refs = rest[:n_pages]
    v_refs = rest[n_pages:2 * n_pages]
    bias_ref, o_ref, kbuf, vbuf, s_ref = rest[2 * n_pages:]
    ppb = MOBA_BLOCK // PAGE
    n_blk = n_pages // ppb
    scale = HEAD_DIM ** -0.5
    qf = _q_block_diag(_pad_rows(q_ref, kbuf))
    qs = (qf * scale).astype(BF16)
    neg = -jnp.inf
    lane = lax.broadcasted_iota(jnp.int32, (QROWS, LANES), 1)
    lane_w = lax.broadcasted_iota(jnp.int32, (ATT_WIDTH, LANES), 1)

    ksum = jnp.zeros((ATT_WIDTH, LANES), F32)
    for n in range(n_blk):
        blk = k_refs[ppb * n][...].reshape(ATT_WIDTH, PAGE)
        for j in range(1, ppb):
            blk = blk + k_refs[ppb * n + j][...].reshape(ATT_WIDTH, PAGE)
        ksum = jnp.where(lane_w == n, jnp.sum(blk, axis=-1, keepdims=True), ksum)
    kmean = ksum * (1.0 / MOBA_BLOCK)
    qh, ql = _split2(qf)
    mh, ml = _split2(kmean)
    route = _dot(qh, mh) + _dot(qh, ml) + _dot(ql, mh)
    sel = _route_topk(route, lane, n_blk, n_blk)

    mrun = jnp.full((QROWS, PAGE), neg, F32)
    for p in range(n_pages):
        s = _dot(qs, k_refs[p][...].reshape(ATT_WIDTH, PAGE).astype(BF16)) + bias_ref[p]
        s = jnp.where(sel[:, p // ppb:p // ppb + 1] > 0.5, s, neg)
        s_ref[p] = s
        mrun = jnp.maximum(mrun, s)
    kbuf[...] = jnp.zeros_like(kbuf)
    vbuf[...] = jnp.zeros_like(vbuf)
    kbuf[0:t_new, :] = kn_ref[...]
    vbuf[0:t_new, :] = vn_ref[...]
    r = lax.broadcasted_iota(jnp.int32, (QROWS, PAGE), 0)
    c = lax.broadcasted_iota(jnp.int32, (QROWS, PAGE), 1)
    causal = jnp.logical_and(c <= r % T_PAD, c < T_PAD)
    s = _dot_nt(qs, kbuf[...].astype(BF16)) + bias_ref[n_pages]
    s = jnp.where(causal, s, neg)
    s_ref[n_pages] = s
    m = jnp.max(jnp.maximum(mrun, s), axis=-1, keepdims=True)

    lsum = jnp.zeros((QROWS, PAGE), F32)
    acc = jnp.zeros((QROWS, ATT_WIDTH), F32)
    for p in range(n_pages + 1):
        pe = jnp.exp(s_ref[p] - m)
        lsum = lsum + pe
        pe = pe.astype(BF16)
        if p < n_pages:
            acc = acc + _dot_nt(pe, v_refs[p][...].reshape(ATT_WIDTH, PAGE).astype(BF16))
        else:
            acc = acc + _dot(pe, vbuf[...].astype(BF16))
    o_ref[...] = _out_from_acc(acc / jnp.sum(lsum, axis=-1, keepdims=True)).astype(o_ref.dtype)


def moba_sample_bias_idx(n_pages):
    t = np.arange(T_PAD)[:, None]
    s = np.arange(PAGE)[None, :]
    past = [_rel_bucket_np((n_pages - p) * PAGE + t - s) for p in range(n_pages)]
    return np.stack(past + [_rel_bucket_np(t - s)])


def moba_sample(page_table, us8, cache_k, cache_v, bias, layer, t_new):
    n_seq, n_pages = page_table.shape
    tok = lambda cb: pl.BlockSpec((None, t_new, ATT_WIDTH), lambda b, pt: (b, 0, cb // 4))
    const = lambda a: pl.BlockSpec(a.shape, lambda b, pt: (0,) * a.ndim)
    grid_spec = pltpu.PrefetchScalarGridSpec(
        num_scalar_prefetch=1,
        grid=(n_seq,),
        in_specs=[tok(CB_Q_MB), tok(CB_K_MB), tok(CB_V_MB)]
        + _page_specs(layer, n_pages) + _page_specs(layer, n_pages)
        + [const(bias)],
        out_specs=pl.BlockSpec((None, T_PAD, ATT_WIDTH), lambda b, pt: (b, 0, 0)),
        scratch_shapes=[pltpu.VMEM((PAGE, ATT_WIDTH), F32), pltpu.VMEM((PAGE, ATT_WIDTH), F32),
                        pltpu.VMEM((n_pages + 1, QROWS, PAGE), F32)],
    )
    return pl.pallas_call(
        functools.partial(_moba_sample_body, n_pages=n_pages, t_new=t_new),
        grid_spec=grid_spec,
        out_shape=jax.ShapeDtypeStruct((n_seq, T_PAD, ATT_WIDTH), BF16),
        compiler_params=_cparams(("arbitrary",)),
        name="moba_sample",
    )(page_table, us8, us8, us8, *([cache_k] * n_pages), *([cache_v] * n_pages), bias)


def _pad_lanes(v):
    return jnp.pad(v, (0, LANES - v.shape[0])).reshape(1, LANES)


def kernel(x_prompt, x_sample, cache_k_sb, cache_v_sb, cache_k_moba, cache_v_moba, page_table, state_ssm, state_conv, rel_bias, g_mix, w_in, conv_w, conv_b, dt_bias, a_log, d_skip, g_ssm, w_branch_sb, w_branch_moba, w_branch_ssm, w_out, g_ffn, w_router_group, b_router_group, w_router_expert, b_router_expert, w_expert_gate, w_expert_up, w_expert_down, g_final):
    bsz, seq, d = x_prompt.shape
    n_seq, t_new, _ = x_sample.shape
    depth = w_in.shape[0]
    n_pages = page_table.shape[1]
    n_p = bsz * seq
    n_s = n_seq * t_new
    assert d == D_MODEL and seq % MOBA_BLOCK == 0 and t_new <= T_PAD
    assert (n_pages * PAGE) % MOBA_BLOCK == 0 and cache_k_sb.shape[2] == PAGE

    x_p = x_prompt.reshape(n_p, d)
    x_s = x_sample.reshape(n_s, d)
    tm_p = 1024 if n_p % 1024 == 0 else 512
    tm_s = min(n_s, 512)
    n_c = seq // SSD_CHUNK
    pages = [c.transpose(0, 1, 3, 4, 2) for c in (cache_k_sb, cache_v_sb, cache_k_moba, cache_v_moba)]
    tq_mb = MOBA_BLOCK
    bias_p = bias_tiles(rel_bias, jnp.asarray(moba_prompt_bias_idx(tq_mb)))
    bias_s = bias_tiles(rel_bias, jnp.asarray(moba_sample_bias_idx(n_pages))).reshape(
        n_pages + 1, QROWS, PAGE)
    zero_conv = jnp.zeros((bsz, SUBLANES, CONV_DIM), F32)
    zero_state = jnp.zeros((1, bsz, SSM_HEADS, SSM_P, SSM_N), F32)
    n_main = CB_XBC * LANES + CONV_DIM

    def col(a, cb, w):
        return a[..., cb * LANES:cb * LANES + w]

    def kv(a, cb):
        return col(a, cb, ATT_WIDTH).reshape(a.shape[0], a.shape[1], N_HEADS, HEAD_DIM)

    new_p, new_s = [], []
    for l in range(depth):
        w_main = jnp.concatenate([w_in[l][:, :n_main], w_in[l][:, n_main + SSM_HEADS:]], axis=1).astype(BF16)
        w_dt = jnp.pad(w_in[l][:, n_main:n_main + SSM_HEADS], ((0, 0), (0, LANES - SSM_HEADS))).astype(BF16)
        gm = g_mix[l].reshape(1, d)
        ssm_par = (conv_w[l], conv_b[l].reshape(1, CONV_DIM), _pad_lanes(dt_bias[l]),
                   _pad_lanes(a_log[l]), _pad_lanes(d_skip[l]))
        wr = jnp.zeros((d + SUBLANES, LANES), F32)
        wr = wr.at[:d, :N_EXPERTS].set(w_router_expert[l]).at[:d, N_EXPERTS:N_EXPERTS + N_GROUPS].set(w_router_group[l])
        wr = wr.at[d, :N_EXPERTS].set(b_router_expert[l]).at[d, N_EXPERTS:N_EXPERTS + N_GROUPS].set(b_router_group[l])
        wgu = jnp.concatenate([w_expert_gate[l], w_expert_up[l]], axis=-1).astype(BF16)
        wd = w_expert_down[l].astype(BF16)
        w_br = (w_branch_sb[l].astype(BF16), w_branch_moba[l].astype(BF16),
                w_branch_ssm[l].astype(BF16), w_out[l].astype(BF16))

        def tail(x, osb, omb, y, u, tm):
            x1 = merge(x, osb, omb, y, u, g_ssm[l].reshape(1, d), *w_br, tm=min(tm, 512))
            xn, cmb = router(x1, g_ffn[l].reshape(1, d), wr, tm=min(tm, 512))
            return moe(x1, xn, cmb, wgu, wd, g_final.reshape(1, d), final_norm=(l == depth - 1), tm=tm)

        u_p, dt_p, qkv_p = in_proj(x_p, gm, w_main, w_dt, tm=tm_p)
        osb_p = sb_prompt(qkv_p, qkv_p, qkv_p, CB_Q_SB, CB_K_SB, CB_V_SB, bsz, seq)
        km = kmean_prompt(u_p, CB_K_MB, bsz, seq)
        omb_p = moba_prompt(u_p, qkv_p, qkv_p, km, rel_bias, bias_p, CB_Q_MB, CB_K_MB, CB_V_MB,
                            bsz, seq, tq=tq_mb)
        y_p, ssm_p = ssd(u_p.reshape(bsz * n_c, SSD_CHUNK, U_COLS), CB_XBC,
                         dt_p.reshape(bsz * n_c, SSD_CHUNK, LANES), zero_conv, zero_state, 0,
                         *ssm_par, bsz=bsz, n_c=n_c)
        x_p = tail(x_p, osb_p, omb_p, y_p.reshape(n_p, SSM_INNER), u_p, tm_p)

        u_s, dt_s, _ = in_proj(x_s, gm, w_main, w_dt, tm=tm_s)
        u_s3 = u_s.reshape(n_seq, t_new, U_COLS)
        osb_s = sb_sample(page_table, u_s3, pages[0], pages[1], l, t_new)[:, :t_new]
        omb_s = moba_sample(page_table, u_s3, pages[2], pages[3], bias_s, l, t_new)[:, :t_new]
        cprev_s = jnp.pad(state_conv[l], ((0, 0), (SUBLANES - (CONV_W - 1), 0), (0, 0)))
        y_s, ssm_s = ssd(u_s3, CB_XBC, dt_s.reshape(n_seq, t_new, LANES), cprev_s, state_ssm, l,
                         *ssm_par, bsz=n_seq, n_c=1, q=T_PAD)
        x_s = tail(x_s, osb_s.reshape(n_s, ATT_WIDTH), omb_s.reshape(n_s, ATT_WIDTH),
                   y_s[:, :t_new].reshape(n_s, SSM_INNER), u_s, tm_s)

        u_p3 = u_p.reshape(bsz, seq, U_COLS)
        new_p.append((kv(u_p3, CB_K_SB), kv(u_p3, CB_V_SB), kv(u_p3, CB_K_MB), kv(u_p3, CB_V_MB),
                      ssm_p, col(u_p3, CB_XBC, CONV_DIM)[:, seq - (CONV_W - 1):]))
        conv_s = jnp.concatenate([state_conv[l], col(u_s3, CB_XBC, CONV_DIM)], axis=1)[:, t_new:]
        new_s.append((kv(u_s3, CB_K_SB), kv(u_s3, CB_V_SB), kv(u_s3, CB_K_MB), kv(u_s3, CB_V_MB),
                      ssm_s, conv_s))

    stack = lambda states: tuple(jnp.stack([s[i] for s in states]) for i in range(6))
    return ((x_p.reshape(bsz, seq, d), x_s.reshape(n_seq, t_new, d)) + stack(new_p) + stack(new_s))
```

```python
import functools
import math

import numpy as np
import jax
import jax.numpy as jnp
from jax import lax
from jax.experimental import pallas as pl
from jax.experimental.pallas import tpu as pltpu

F32 = jnp.float32
BF16 = jnp.bfloat16

LANES = 128
SUBLANES = 8
VMEM_LIMIT = 56 * 1024 * 1024

D_MODEL = 1024
HEAD_DIM = 64
N_HEADS = 8
ATT_WIDTH = N_HEADS * HEAD_DIM
MOBA_BLOCK = 256
MOBA_TOPK = 3
PAGE = 128
SSM_HEADS = 16
SSM_P = 64
SSM_GROUPS = 4
SSM_N = 128
SSM_INNER = SSM_HEADS * SSM_P
CONV_W = 4
CONV_DIM = SSM_INNER + 2 * SSM_GROUPS * SSM_N
SSD_CHUNK = 128
REL_BUCKETS = 32
REL_MAX_DIST = 128
N_GROUPS = 4
EPG = 8
N_EXPERTS = N_GROUPS * EPG
EXPERT_FF = 256
EPS = 1e-6
EXP_ZERO = -88.0
NEG_BIG = -1e30

CB_Q_SB, CB_K_SB, CB_V_SB = 0, 4, 8
CB_Q_MB, CB_K_MB, CB_V_MB = 12, 16, 20
CB_Z = 24
CB_XBC = 32
CB_GATE = 48
U_COLS = 72 * LANES


def _cparams(sem):
    return pltpu.CompilerParams(dimension_semantics=sem, vmem_limit_bytes=VMEM_LIMIT)


def _split2(a):
    hi = a.astype(BF16)
    lo = (a - hi.astype(F32)).astype(BF16)
    return hi, lo


def _dot(a, b):
    return jnp.dot(a, b, preferred_element_type=F32)


def _dot_nt(a, b):
    return lax.dot_general(a, b, (((1,), (1,)), ((), ())), preferred_element_type=F32)


def _dot2(a, b_exact):
    hi, lo = _split2(a)
    return _dot(hi, b_exact) + _dot(lo, b_exact)


def _dot2_nt(a_exact, b):
    hi, lo = _split2(b)
    return _dot_nt(a_exact, hi) + _dot_nt(a_exact, lo)


def _dot3_nt(a, b):
    ah, al = _split2(a)
    bh, bl = _split2(b)
    return _dot_nt(ah, bh) + _dot_nt(ah, bl) + _dot_nt(al, bh)


def _sigmoid(x):
    return 1.0 / (1.0 + jnp.exp(-x))


def _softplus(x):
    return jnp.maximum(x, 0.0) + jnp.log(1.0 + jnp.exp(-jnp.abs(x)))


def _rms(x, g):
    ms = jnp.mean(x * x, axis=-1, keepdims=True)
    return x * lax.rsqrt(ms + EPS) * g


def _in_proj_body(x_ref, g_ref, w_ref, wdt_ref, u_ref, dt_ref, qkv_ref, xn_ref, *, n_qkv):
    j = pl.program_id(1)

    @pl.when(j == 0)
    def _():
        xn = _rms(x_ref[...], g_ref[...]).astype(BF16)
        xn_ref[...] = xn
        dt_ref[...] = _dot(xn, wdt_ref[...])

    u = _dot(xn_ref[...], w_ref[...])
    u_ref[...] = u

    @pl.when(j < n_qkv)
    def _():
        qkv_ref[...] = u.astype(BF16)


def in_proj(x, g, w, wdt, layer, tm=512, tn=1024):
    n, d = x.shape
    cols = w.shape[2]
    n_qkv = (CB_Z * LANES) // tn
    return pl.pallas_call(
        functools.partial(_in_proj_body, n_qkv=n_qkv),
        grid=(n // tm, cols // tn),
        in_specs=[
            pl.BlockSpec((tm, d), lambda i, j: (i, 0)),
            pl.BlockSpec((None, 1, d), lambda i, j: (layer, 0, 0)),
            pl.BlockSpec((None, d, tn), lambda i, j: (layer, 0, j)),
            pl.BlockSpec((None, d, LANES), lambda i, j: (layer, 0, 0)),
        ],
        out_specs=[
            pl.BlockSpec((tm, tn), lambda i, j: (i, j)),
            pl.BlockSpec((tm, LANES), lambda i, j: (i, 0)),
            pl.BlockSpec((tm, tn), lambda i, j: (i, jnp.minimum(j, n_qkv - 1))),
        ],
        out_shape=[jax.ShapeDtypeStruct((n, cols), F32),
                   jax.ShapeDtypeStruct((n, LANES), F32),
                   jax.ShapeDtypeStruct((n, n_qkv * tn), BF16)],
        scratch_shapes=[pltpu.VMEM((tm, d), BF16)],
        compiler_params=_cparams(("parallel", "arbitrary")),
        name="in_proj",
    )(x, g, w, wdt)


def _suffix_sums(lk, tri2):
    hi, lo = _split2(lk)
    return _dot(jnp.concatenate([hi, lo], axis=1), tri2)


def _sb_tile(qh, k, v, carry, tri2, mask):
    tk = k.shape[0]
    z = _dot_nt(qh, k)
    lk = -_softplus(z)
    if mask is not None:
        lk = jnp.where(mask, lk, 0.0)
    sums = _suffix_sums(lk, tri2)
    w = jnp.exp(z + sums[:, :tk] + carry)
    if mask is not None:
        w = jnp.where(mask, w, 0.0)
    return _dot(w.astype(BF16), v), carry + sums[:, tk:]


def _stack_heads(q, scale):
    lane = lax.broadcasted_iota(jnp.int32, q.shape, 1)
    q = q * scale
    return jnp.concatenate([jnp.where(lane < HEAD_DIM, q, 0.0),
                            jnp.where(lane >= HEAD_DIM, q, 0.0)], axis=0).astype(BF16)


def _unstack_heads(o, tq):
    lane = lax.broadcasted_iota(jnp.int32, (tq, LANES), 1)
    return jnp.where(lane < HEAD_DIM, o[:tq], o[tq:])


def _sb_prompt_body(q_ref, k_ref, v_ref, tri_ref, o_ref, acc_ref, car_ref, *, tk):
    i = pl.program_id(2)
    scale = HEAD_DIM ** -0.5
    q = q_ref[...].astype(F32)
    qa = jnp.concatenate([_stack_heads(q[:tk], scale), _stack_heads(q[tk:], scale)], axis=0)
    hb = 2 * tk
    tri2 = tri_ref[...]
    row = lax.broadcasted_iota(jnp.int32, (2 * hb, tk), 0)
    col = lax.broadcasted_iota(jnp.int32, (2 * hb, tk), 1)
    tri_mask = col < row % tk

    def tile(j, carry, mask):
        kj = pl.multiple_of(j * tk, tk)
        return _sb_tile(qa, k_ref[pl.ds(kj, tk), :], v_ref[pl.ds(kj, tk), :], carry, tri2, mask)

    tiles = [(2 * i + 1, qa[hb:], tri_mask[hb:]),
             (2 * i, qa, jnp.logical_or(row >= hb, tri_mask)),
             (jnp.maximum(2 * i - 1, 0), qa, jnp.broadcast_to(i > 0, (2 * hb, tk)))]
    starts = [pl.multiple_of(j * tk, tk) for j, _, _ in tiles]
    zs = [_dot_nt(qr, k_ref[pl.ds(s, tk), :]) for s, (_, qr, _) in zip(starts, tiles)]
    lks = [jnp.where(mask, -_softplus(z), 0.0) for z, (_, _, mask) in zip(zs, tiles)]
    sums = [_suffix_sums(lk, tri2) for lk in lks]
    ws = []
    c = jnp.zeros((hb, tk), F32)
    for t, (z, sm, (_, _, mask)) in enumerate(zip(zs, sums, tiles)):
        ws.append(jnp.where(mask, jnp.exp(z + sm[:, :tk] + c), 0.0).astype(BF16))
        c = c + sm[:, tk:]
        if t == 0:
            c = jnp.concatenate([jnp.zeros((hb, tk), F32), c], axis=0)
    os_ = [_dot(w, v_ref[pl.ds(s, tk), :]) for w, s in zip(ws, starts)]
    acc_ref[...] = os_[1] + os_[2]
    acc_ref[hb:, :] += os_[0]
    car_ref[...] = c

    def cond(j):
        return jnp.logical_and(j >= 0, jnp.max(car_ref[...]) >= EXP_ZERO)

    def body(j):
        o, c = tile(j, car_ref[...], None)
        acc_ref[...] += o
        car_ref[...] = c
        return j - 1

    lax.while_loop(cond, body, 2 * i - 2)
    o_ref[0:tk, :] = _unstack_heads(acc_ref[0:hb, :], tk).astype(o_ref.dtype)
    o_ref[tk:, :] = _unstack_heads(acc_ref[hb:, :], tk).astype(o_ref.dtype)


def _tri_ge_ones(n):
    t2 = np.concatenate([np.tril(np.ones((n, n), np.float32)), np.ones((n, n), np.float32)], axis=1)
    return jnp.asarray(np.concatenate([t2, t2], axis=0), BF16)


def sb_prompt(qa, ka, va, cbq, cbk, cbv, bsz, t_len, tk=128):
    tq = 2 * tk
    n_q = t_len // tq
    n_hp = N_HEADS // 2
    return pl.pallas_call(
        functools.partial(_sb_prompt_body, tk=tk),
        grid=(bsz, n_hp, n_q),
        in_specs=[
            pl.BlockSpec((tq, LANES), lambda b, p, i: (b * n_q + i, cbq + p)),
            pl.BlockSpec((t_len, LANES), lambda b, p, i: (b, cbk + p)),
            pl.BlockSpec((t_len, LANES), lambda b, p, i: (b, cbv + p)),
            pl.BlockSpec((2 * tk, 2 * tk), lambda b, p, i: (0, 0)),
        ],
        out_specs=pl.BlockSpec((tq, LANES), lambda b, p, i: (b * n_q + i, p)),
        out_shape=jax.ShapeDtypeStruct((bsz * t_len, ATT_WIDTH), BF16),
        scratch_shapes=[pltpu.VMEM((2 * tq, LANES), F32), pltpu.VMEM((2 * tq, tk), F32)],
        compiler_params=_cparams(("parallel", "parallel", "arbitrary")),
        name="sb_prompt",
    )(qa, ka, va, _tri_ge_ones(tk))


def _rel_bucket_np(dist):
    exact = REL_BUCKETS // 2
    d = np.maximum(dist, 0)
    df = np.maximum(d, 1).astype(np.float32)
    large = exact + (np.log(df / np.float32(exact)) / np.float32(math.log(REL_MAX_DIST / exact))
                     * np.float32(REL_BUCKETS - exact)).astype(np.int32)
    return np.where(d < exact, d, np.minimum(large, REL_BUCKETS - 1)).astype(np.int32)


def _bias_body(rb_ref, idx_ref, o_ref):
    idx = idx_ref[...]
    for h in range(N_HEADS):
        acc = jnp.zeros(idx.shape, F32)
        for b in range(REL_BUCKETS):
            acc = jnp.where(idx == b, rb_ref[b, h], acc)
        o_ref[h] = acc


def bias_tiles(rel_bias, idx):
    nv, r, c = idx.shape
    return pl.pallas_call(
        _bias_body,
        grid=(nv,),
        in_specs=[pl.BlockSpec(memory_space=pltpu.SMEM),
                  pl.BlockSpec((None, r, c), lambda v: (v, 0, 0))],
        out_specs=pl.BlockSpec((None, N_HEADS, r, c), lambda v: (v, 0, 0, 0)),
        out_shape=jax.ShapeDtypeStruct((nv, N_HEADS, r, c), F32),
        compiler_params=_cparams(("parallel",)),
        name="bias_tiles",
    )(rel_bias, idx)


def _kmean_body(k_ref, o_ref):
    n = pl.program_id(1)

    @pl.when(n == 0)
    def _():
        o_ref[...] = jnp.zeros_like(o_ref)

    o_ref[pl.ds(n, 1), :] = jnp.mean(k_ref[...], axis=0, keepdims=True)


def kmean_prompt(ka, cbk, bsz, t_len):
    n_blk = t_len // MOBA_BLOCK
    return pl.pallas_call(
        _kmean_body,
        grid=(bsz, n_blk),
        in_specs=[pl.BlockSpec((MOBA_BLOCK, ATT_WIDTH), lambda b, n: (b * n_blk + n, cbk // 4))],
        out_specs=pl.BlockSpec((None, LANES, ATT_WIDTH), lambda b, n: (b, 0, 0)),
        out_shape=jax.ShapeDtypeStruct((bsz, LANES, ATT_WIDTH), F32),
        compiler_params=_cparams(("parallel", "arbitrary")),
        name="kmean_prompt",
    )(ka)


def _route_topk(route, lane, n_valid, n_ok):
    neg = -jnp.inf
    r = jnp.where(lane < n_valid, route, neg)
    sel = jnp.zeros(route.shape, F32)
    for j in range(MOBA_TOPK):
        m = jnp.max(r, axis=-1, keepdims=True)
        idx = jnp.min(jnp.where(r == m, lane, LANES), axis=-1, keepdims=True)
        hit = lane == idx
        sel = jnp.where(jnp.logical_and(hit, j < n_ok), 1.0, sel)
        r = jnp.where(hit, neg, r)
    return sel


def _block_penalty_t(route_t, own):
    blk = lax.broadcasted_iota(jnp.int32, route_t.shape, 0)
    neg = -jnp.inf
    r = jnp.where(blk < own, route_t, neg)
    keep = blk == own
    for j in range(MOBA_TOPK):
        m = jnp.max(r, axis=0, keepdims=True)
        idx = jnp.min(jnp.where(r == m, blk, route_t.shape[0]), axis=0, keepdims=True)
        hit = blk == idx
        keep = jnp.logical_or(keep, jnp.logical_and(hit, j < own))
        r = jnp.where(hit, neg, r)
    return jnp.where(keep, 0.0, NEG_BIG)


def _moba_prompt_body(rb_ref, q_ref, k_ref, v_ref, e_ref, eye_ref, km_ref, bias_ref, o_ref,
                      qa_ref, s_ref, mrun_ref, lrun_ref, acc_ref, *, tq, n_blk):
    p = pl.program_id(1)
    i = pl.program_id(2)
    per = MOBA_BLOCK // tq
    own = i // per
    par = i % per
    q = q_ref[...]
    lane = lax.broadcasted_iota(jnp.int32, q.shape, 1)
    km = km_ref[0:n_blk, :]
    pad = jnp.zeros((LANES - n_blk, tq), F32)
    pen = []
    for h in range(2):
        qf = jnp.where((lane < HEAD_DIM) if h == 0 else (lane >= HEAD_DIM), q, 0.0)
        pen_t = _block_penalty_t(_dot3_nt(km, qf), own)
        pen.append(_dot_nt(eye_ref[...], jnp.concatenate([pen_t, pad], axis=0).astype(BF16)))
    qa_ref[...] = jnp.concatenate(
        [_stack_heads(q, HEAD_DIM ** -0.5), jnp.concatenate(pen, axis=0).astype(BF16)], axis=1)
    qa = qa_ref[...]

    row = lax.broadcasted_iota(jnp.int32, (2 * tq, MOBA_BLOCK), 0)
    col = lax.broadcasted_iota(jnp.int32, (2 * tq, MOBA_BLOCK), 1)
    causal = col <= row % tq + par * tq
    far_bias = jnp.where(row[:, 0:1] < tq, rb_ref[REL_BUCKETS - 1, 2 * p],
                         rb_ref[REL_BUCKETS - 1, 2 * p + 1])

    def logits(n, bias):
        n0 = pl.multiple_of(n * MOBA_BLOCK, MOBA_BLOCK)
        kaug = jnp.concatenate([k_ref[pl.ds(n0, MOBA_BLOCK), :], e_ref[pl.ds(n0, MOBA_BLOCK), :]],
                               axis=1)
        return _dot_nt(qa, kaug) + bias

    prev = jnp.maximum(own - 1, 0)
    prev_slot = jnp.where(own >= 1, own - 1, n_blk)
    n_far = prev

    def far_loop(step, group):
        def grouped(g, carry):
            step([group * g + j for j in range(group)])
            return carry

        lax.fori_loop(0, n_far // group, grouped, 0)
        base = (n_far // group) * group
        size = group // 2
        while size >= 1:
            take = (n_far - base) >= size

            @pl.when(take)
            def _(base=base, size=size):
                step([base + j for j in range(size)])

            base = base + jnp.where(take, size, 0)
            size //= 2

    def fold(ss, op):
        out = None
        for s in ss:
            h = op(s[:, :LANES], s[:, LANES:])
            out = h if out is None else op(out, h)
        return out

    s_own = jnp.where(causal, logits(own, bias_ref[par, 0].reshape(2 * tq, MOBA_BLOCK)), NEG_BIG)
    s_prev = jnp.where(own >= 1, logits(prev, bias_ref[par, 1].reshape(2 * tq, MOBA_BLOCK)), NEG_BIG)
    s_ref[own] = s_own
    s_ref[prev_slot] = s_prev
    mrun_ref[...] = fold([s_own, s_prev], jnp.maximum)

    def far_max(ns):
        ss = [logits(n, far_bias) for n in ns]
        for n, s in zip(ns, ss):
            s_ref[n] = s
        mrun_ref[...] = jnp.maximum(mrun_ref[...], fold(ss, jnp.maximum))

    far_loop(far_max, 4)
    m = jnp.max(mrun_ref[...], axis=-1, keepdims=True)

    def weigh(slots, ns):
        pes = [jnp.exp(s_ref[sl] - m) for sl in slots]
        o = None
        for pe, n in zip(pes, ns):
            n0 = pl.multiple_of(n * MOBA_BLOCK, MOBA_BLOCK)
            on = _dot(pe.astype(BF16), v_ref[pl.ds(n0, MOBA_BLOCK), :])
            o = on if o is None else o + on
        return fold(pes, jnp.add), o

    lrun_ref[...], acc_ref[...] = weigh([own, prev_slot], [own, prev])

    def far_acc(ns):
        l, o = weigh(ns, ns)
        lrun_ref[...] += l
        acc_ref[...] += o

    far_loop(far_acc, 4)
    o = acc_ref[...] / jnp.sum(lrun_ref[...], axis=-1, keepdims=True)
    o_ref[...] = _unstack_heads(o, tq).astype(o_ref.dtype)


def moba_prompt_bias_idx(tq):
    per = MOBA_BLOCK // tq
    t = np.arange(tq)[:, None]
    s = np.arange(MOBA_BLOCK)[None, :]
    idx = np.stack([np.stack([_rel_bucket_np(par * tq + kind * MOBA_BLOCK + t - s)
                              for kind in range(2)]) for par in range(per)])
    return idx.reshape(per * 2, tq, MOBA_BLOCK)


def moba_prompt(qa, ka, va, kmean, rel_bias, bias, cbq, cbk, cbv, bsz, t_len, tq=128):
    n_q = t_len // tq
    n_hp = N_HEADS // 2
    per = MOBA_BLOCK // tq
    bias = bias.reshape(per, 2, N_HEADS, tq, MOBA_BLOCK)
    n_blk = t_len // MOBA_BLOCK
    assert n_blk <= LANES
    blk_of_row = np.arange(t_len) // MOBA_BLOCK
    onehot = jnp.asarray(blk_of_row[:, None] == np.arange(LANES)[None, :], BF16)
    eye = jnp.asarray(np.eye(tq, dtype=np.float32), BF16)
    return pl.pallas_call(
        functools.partial(_moba_prompt_body, tq=tq, n_blk=n_blk),
        grid=(bsz, n_hp, n_q),
        in_specs=[
            pl.BlockSpec(memory_space=pltpu.SMEM),
            pl.BlockSpec((tq, LANES), lambda b, p, i: (b * n_q + i, cbq + p)),
            pl.BlockSpec((t_len, LANES), lambda b, p, i: (b, cbk + p)),
            pl.BlockSpec((t_len, LANES), lambda b, p, i: (b, cbv + p)),
            pl.BlockSpec((t_len, LANES), lambda b, p, i: (0, 0)),
            pl.BlockSpec((tq, tq), lambda b, p, i: (0, 0)),
            pl.BlockSpec((None, LANES, LANES), lambda b, p, i: (b, 0, p)),
            pl.BlockSpec((per, 2, 2, tq, MOBA_BLOCK), lambda b, p, i: (0, 0, p, 0, 0)),
        ],
        out_specs=pl.BlockSpec((tq, LANES), lambda b, p, i: (b * n_q + i, p)),
        out_shape=jax.ShapeDtypeStruct((bsz * t_len, ATT_WIDTH), BF16),
        scratch_shapes=[pltpu.VMEM((2 * tq, 2 * LANES), BF16),
                        pltpu.VMEM((n_blk + 1, 2 * tq, MOBA_BLOCK), F32),
                        pltpu.VMEM((2 * tq, LANES), F32),
                        pltpu.VMEM((2 * tq, LANES), F32),
                        pltpu.VMEM((2 * tq, LANES), F32)],
        compiler_params=_cparams(("parallel", "parallel", "arbitrary")),
        name="moba_prompt",
    )(rel_bias, qa, ka, va, onehot, eye, kmean, bias)


def _ssd_body(xbc_ref, dt_ref, cprev_ref, h0_ref, cw_ref, cb_ref, dtb_ref, alog_ref, dsk_ref,
              e_ref, et_ref, eye_ref, tril_ref, y_ref, h_ref, xf_ref, dtf_ref, *, q):
    c = pl.program_id(1)
    t_in = xbc_ref.shape[0]
    t_out = y_ref.shape[0]
    hp2 = 2 * SSM_P
    gw = (SSM_HEADS // SSM_GROUPS) * SSM_P

    @pl.when(c == 0)
    def _():
        xf_ref[0:SUBLANES, :] = cprev_ref[...]
        h_ref[...] = h0_ref[...]

    if t_in < q:
        xf_ref[SUBLANES:SUBLANES + q, :] = jnp.zeros((q, CONV_DIM), F32)
        dtf_ref[...] = jnp.zeros_like(dtf_ref)
    xf_ref[SUBLANES:SUBLANES + t_in, :] = xbc_ref[...]
    dtf_ref[0:t_in, :] = dt_ref[...]
    base = SUBLANES - (CONV_W - 1)
    conv = cb_ref[...]
    for j in range(CONV_W):
        conv = conv + xf_ref[base + j:base + j + q, :] * cw_ref[j:j + 1, :]
    xf_ref[0:SUBLANES, :] = xf_ref[q:q + SUBLANES, :]
    act = conv * _sigmoid(conv)
    xs = act[:, :SSM_INNER]

    dt = _softplus(dtf_ref[...] + dtb_ref[...])
    if t_in < q:
        trow = lax.broadcasted_iota(jnp.int32, dt.shape, 0)
        dt = jnp.where(trow < t_in, dt, 0.0)
    a = dt * (-jnp.exp(alog_ref[...]))
    a_hi, a_lo = _split2(a)
    tril = tril_ref[...]
    acs = _dot(tril, a_hi) + _dot(tril, a_lo)
    acs_t = _dot2_nt(eye_ref[...], acs)
    e = e_ref[...]
    xr = xs * _dot2(dt, e)
    eacs_e = _dot2(jnp.exp(acs), e)
    xd = xr * _dot2(jnp.exp(acs[q - 1:q, :] - acs), e)
    xd_t = xd.T
    cd = jnp.broadcast_to(jnp.exp(acs_t[:, q - 1:q]), (LANES, LANES))
    cd_hi, cd_lo = _split2(cd)
    et = et_ref[...]
    f = _dot(et, cd_hi) + _dot(et, cd_lo)
    dsk_e = _dot2(jnp.broadcast_to(dsk_ref[...], (SUBLANES, LANES)), e)[0:1, :]

    row = lax.broadcasted_iota(jnp.int32, (q, q), 0)
    col = lax.broadcasted_iota(jnp.int32, (q, q), 1)
    causal = col <= row
    lane = lax.broadcasted_iota(jnp.int32, (q, hp2), 1)
    hpg = SSM_HEADS // SSM_GROUPS
    for g in range(SSM_GROUPS):
        bg = act[:, SSM_INNER + g * SSM_N:SSM_INNER + (g + 1) * SSM_N].astype(BF16)
        cg = act[:, SSM_INNER + (SSM_GROUPS + g) * SSM_N:
                 SSM_INNER + (SSM_GROUPS + g + 1) * SSM_N].astype(BF16)
        cbm = _dot_nt(cg, bg)
        hg = h_ref[g * hpg:(g + 1) * hpg].reshape(gw, SSM_N)
        y_off = _dot_nt(cg, hg.astype(BF16)) * eacs_e[:, g * gw:(g + 1) * gw]
        for pr in range(hpg // 2):
            yd = []
            xr_pair = xr[:, g * gw + pr * hp2:g * gw + (pr + 1) * hp2].astype(BF16)
            for hh in range(2):
                h = g * hpg + pr * 2 + hh
                seg = acs[:, h:h + 1] - acs_t[h:h + 1, :]
                m = (cbm * jnp.where(causal, jnp.exp(seg), 0.0)).astype(BF16)
                yd.append(_dot(m, xr_pair))
            y_pair = (jnp.where(lane < SSM_P, yd[0], yd[1]) + y_off[:, pr * hp2:(pr + 1) * hp2]
                      + xs[:, g * gw + pr * hp2:g * gw + (pr + 1) * hp2]
                      * dsk_e[:, g * gw + pr * hp2:g * gw + (pr + 1) * hp2])
            y_ref[:, g * gw + pr * hp2:g * gw + (pr + 1) * hp2] = y_pair[:t_out]
        st = _dot(xd_t[g * gw:(g + 1) * gw, :].astype(BF16), bg)
        h_ref[g * hpg:(g + 1) * hpg] = (hg * f[g * gw:(g + 1) * gw, :] + st).reshape(hpg, SSM_P, SSM_N)


def _ssd_consts(q):
    hidx = np.arange(SSM_INNER) // SSM_P
    e = (np.arange(LANES)[:, None] == hidx[None, :]).astype(np.float32)
    return (jnp.asarray(e, BF16), jnp.asarray(e.T, BF16),
            jnp.asarray(np.eye(LANES, dtype=np.float32), BF16),
            jnp.asarray(np.tril(np.ones((q, q), np.float32)), BF16))


def ssd(xa, cbx, dta, cprev, h0, state_layer, cw, cb, dtb, alog, dsk, layer, bsz, n_c, q=SSD_CHUNK):
    e, et, eye, tril = _ssd_consts(q)
    xblk = CONV_DIM // LANES
    t_in = xa.shape[1]
    t_out = q if t_in == q else SUBLANES
    const = lambda shape: pl.BlockSpec(shape, lambda b, c: (0,) * len(shape))
    par = lambda shape: pl.BlockSpec((None,) + shape, lambda b, c: (layer, 0, 0))
    return pl.pallas_call(
        functools.partial(_ssd_body, q=q),
        grid=(bsz, n_c),
        in_specs=[
            pl.BlockSpec((None, t_in, CONV_DIM), lambda b, c: (b * n_c + c, 0, cbx // xblk)),
            pl.BlockSpec((None, t_in, LANES), lambda b, c: (b * n_c + c, 0, 0)),
            pl.BlockSpec((None, None, SUBLANES, CONV_DIM), lambda b, c: (state_layer, b, 0, 0)),
            pl.BlockSpec((None, None, SSM_HEADS, SSM_P, SSM_N), lambda b, c: (state_layer, b, 0, 0, 0)),
            par((CONV_W, CONV_DIM)), par((1, CONV_DIM)),
            par((1, LANES)), par((1, LANES)), par((1, LANES)),
            const((LANES, SSM_INNER)), const((SSM_INNER, LANES)), const((LANES, LANES)),
            const((q, q)),
        ],
        out_specs=[
            pl.BlockSpec((None, t_out, SSM_INNER), lambda b, c: (b * n_c + c, 0, 0)),
            pl.BlockSpec((None, SSM_HEADS, SSM_P, SSM_N), lambda b, c: (b, 0, 0, 0)),
        ],
        out_shape=[jax.ShapeDtypeStruct((bsz * n_c, t_out, SSM_INNER), F32),
                   jax.ShapeDtypeStruct((bsz, SSM_HEADS, SSM_P, SSM_N), F32)],
        scratch_shapes=[pltpu.VMEM((q + SUBLANES, CONV_DIM), F32), pltpu.VMEM((q, LANES), F32)],
        compiler_params=_cparams(("parallel", "arbitrary")),
        name="ssd",
    )(xa, dta, cprev, h0, cw, cb, dtb, alog, dsk, e, et, eye, tril)


def _merge_body(x_ref, osb_ref, omb_ref, y_ref, z_ref, g0_ref, g1_ref, g2_ref, gs_ref,
                wsb_ref, wmb_ref, wss_ref, wo_ref, o_ref):
    z = z_ref[...]
    y = _rms(y_ref[...] * (z * _sigmoid(z)), gs_ref[...]).astype(BF16)
    merged = (_sigmoid(g0_ref[...]) * _dot(osb_ref[...], wsb_ref[...])
              + _sigmoid(g1_ref[...]) * _dot(omb_ref[...], wmb_ref[...])
              + _sigmoid(g2_ref[...]) * _dot(y, wss_ref[...]))
    o_ref[...] = x_ref[...] + _dot(merged.astype(BF16), wo_ref[...])


def merge(x, osb, omb, y, u, gs, wsb, wmb, wss, wo, layer, tm=512):
    n, d = x.shape
    db = d // LANES
    row = lambda cb, w: pl.BlockSpec((tm, w), lambda i: (i, cb))
    const = lambda a: pl.BlockSpec((None,) + a.shape[1:], lambda i: (layer, 0, 0))
    return pl.pallas_call(
        _merge_body,
        grid=(n // tm,),
        in_specs=[row(0, d), row(0, ATT_WIDTH), row(0, ATT_WIDTH), row(0, d),
                  row(CB_Z // db, d), row(CB_GATE // db, d), row(CB_GATE // db + 1, d),
                  row(CB_GATE // db + 2, d),
                  const(gs), const(wsb), const(wmb), const(wss), const(wo)],
        out_specs=row(0, d),
        out_shape=jax.ShapeDtypeStruct((n, d), F32),
        compiler_params=_cparams(("parallel",)),
        name="merge",
    )(x, osb, omb, y, u, u, u, u, gs, wsb, wmb, wss, wo)


def _router_body(x_ref, g_ref, wr_ref, xn_ref, cmb_ref):
    xn = _rms(x_ref[...], g_ref[...])
    xn_ref[...] = xn.astype(BF16)
    wr = wr_ref[...]
    xh, xl = _split2(xn)
    wh, wl = _split2(wr[:-SUBLANES, :])
    logit = _dot(xh, wh) + _dot(xh, wl) + _dot(xl, wh) + wr[-SUBLANES:-SUBLANES + 1, :]
    lane = lax.broadcasted_iota(jnp.int32, logit.shape, 1)
    neg = -jnp.inf
    is_g = jnp.logical_and(lane >= N_EXPERTS, lane < N_EXPERTS + N_GROUPS)
    gl = jnp.where(is_g, logit, neg)
    gmax = jnp.max(gl, axis=-1, keepdims=True)
    grp = jnp.min(jnp.where(gl == gmax, lane, LANES), axis=-1, keepdims=True) - N_EXPERTS
    p_grp = 1.0 / jnp.sum(jnp.exp(gl - gmax), axis=-1, keepdims=True)
    in_grp = jnp.logical_and(lane >= grp * EPG, lane < grp * EPG + EPG)
    el = jnp.where(in_grp, logit, neg)
    v1 = jnp.max(el, axis=-1, keepdims=True)
    i1 = jnp.min(jnp.where(el == v1, lane, LANES), axis=-1, keepdims=True)
    el2 = jnp.where(lane == i1, neg, el)
    v2 = jnp.max(el2, axis=-1, keepdims=True)
    i2 = jnp.min(jnp.where(el2 == v2, lane, LANES), axis=-1, keepdims=True)
    d = jnp.exp(v2 - v1)
    w1 = p_grp / (1.0 + d)
    w2 = p_grp * d / (1.0 + d)
    cmb_ref[...] = jnp.where(lane == i1, w1, 0.0) + jnp.where(lane == i2, w2, 0.0)


def router(x, g, wr, layer, tm=512):
    n, d = x.shape
    return pl.pallas_call(
        _router_body,
        grid=(n // tm,),
        in_specs=[pl.BlockSpec((tm, d), lambda i: (i, 0)),
                  pl.BlockSpec((None, 1, d), lambda i: (layer, 0, 0)),
                  pl.BlockSpec((None,) + wr.shape[1:], lambda i: (layer, 0, 0))],
        out_specs=[pl.BlockSpec((tm, d), lambda i: (i, 0)),
                   pl.BlockSpec((tm, LANES), lambda i: (i, 0))],
        out_shape=[jax.ShapeDtypeStruct((n, d), BF16),
                   jax.ShapeDtypeStruct((n, LANES), F32)],
        compiler_params=_cparams(("parallel",)),
        name="router",
    )(x, g, wr)


MOE_EXPERTS_PER_STEP = 4


def _moe_body(x_ref, xn_ref, cmb_ref, wgu_ref, wd_ref, gf_ref, o_ref, *, final_norm):
    g = pl.program_id(1)
    eps = MOE_EXPERTS_PER_STEP

    @pl.when(g == 0)
    def _():
        o_ref[...] = x_ref[...]

    cmb = cmb_ref[...]
    xn = xn_ref[...]
    lane = lax.broadcasted_iota(jnp.int32, cmb.shape, 1)
    hids = []
    for j in range(eps):
        c = jnp.sum(jnp.where(lane == g * eps + j, cmb, 0.0), axis=-1, keepdims=True)
        gu = _dot(xn, wgu_ref[j])
        gt = gu[:, :EXPERT_FF]
        hids.append((c * (gt * _sigmoid(gt) * gu[:, EXPERT_FF:])).astype(BF16))
    wd = wd_ref[...].reshape(eps * EXPERT_FF, wd_ref.shape[-1])
    o_ref[...] += _dot(jnp.concatenate(hids, axis=1), wd)

    if final_norm:
        @pl.when(g == pl.num_programs(1) - 1)
        def _():
            o_ref[...] = _rms(o_ref[...], gf_ref[...])


def moe(x, xn, cmb, wgu, wd, gf, layer, final_norm, tm=512):
    n, d = x.shape
    eps = MOE_EXPERTS_PER_STEP
    return pl.pallas_call(
        functools.partial(_moe_body, final_norm=final_norm),
        grid=(n // tm, N_EXPERTS // eps),
        in_specs=[pl.BlockSpec((tm, d), lambda i, e: (i, 0)),
                  pl.BlockSpec((tm, d), lambda i, e: (i, 0)),
                  pl.BlockSpec((tm, LANES), lambda i, e: (i, 0)),
                  pl.BlockSpec((None, eps, d, 2 * EXPERT_FF), lambda i, e: (layer, e, 0, 0)),
                  pl.BlockSpec((None, eps, EXPERT_FF, d), lambda i, e: (layer, e, 0, 0)),
                  pl.BlockSpec((1, d), lambda i, e: (0, 0))],
        out_specs=pl.BlockSpec((tm, d), lambda i, e: (i, 0)),
        out_shape=jax.ShapeDtypeStruct((n, d), F32),
        compiler_params=_cparams(("parallel", "arbitrary")),
        name="moe",
    )(x, xn, cmb, wgu, wd, gf)


T_PAD = SUBLANES
QROWS = N_HEADS * T_PAD


def _pad_rows(x_ref, buf):
    buf[0:T_PAD, :] = jnp.zeros((T_PAD, buf.shape[1]), buf.dtype)
    buf[0:x_ref.shape[0], :] = x_ref[...]
    return buf[0:T_PAD, :]


def _q_block_diag(q8):
    lane = lax.broadcasted_iota(jnp.int32, q8.shape, 1)
    return jnp.concatenate(
        [jnp.where(lane // HEAD_DIM == h, q8, 0.0) for h in range(N_HEADS)], axis=0)


def _out_from_acc(acc):
    lane = lax.broadcasted_iota(jnp.int32, (T_PAD, ATT_WIDTH), 1)
    out = jnp.zeros((T_PAD, ATT_WIDTH), F32)
    for h in range(N_HEADS):
        out = jnp.where(lane // HEAD_DIM == h, acc[h * T_PAD:(h + 1) * T_PAD, :], out)
    return out


N_AHEAD = 2


def _sb_sample_body(pt_ref, q_ref, kn_ref, vn_ref, ck_ref, cv_ref, tri_ref, o_ref,
                    kbuf, vbuf, acc_ref, car_ref, flag, kpg, vpg, kx, vx, sem_pg, sem_x,
                    *, n_pages, t_new, layer):
    b = pl.program_id(0)
    n_seq = pl.num_programs(0)
    slot = b % 2
    scale = HEAD_DIM ** -0.5
    qbd = (_q_block_diag(_pad_rows(q_ref, kbuf)) * scale).astype(BF16)
    tri = tri_ref[...]

    def ahead_copies(seq, sl):
        cps = []
        for j in range(N_AHEAD):
            pg = pt_ref[seq, n_pages - 1 - j]
            cps.append(pltpu.make_async_copy(ck_ref.at[layer, pg], kpg.at[sl, j], sem_pg.at[sl, j, 0]))
            cps.append(pltpu.make_async_copy(cv_ref.at[layer, pg], vpg.at[sl, j], sem_pg.at[sl, j, 1]))
        return cps

    def demand_copies(p):
        pg = pt_ref[b, p]
        return [pltpu.make_async_copy(ck_ref.at[layer, pg], kx, sem_x.at[0]),
                pltpu.make_async_copy(cv_ref.at[layer, pg], vx, sem_x.at[1])]

    @pl.when(b == 0)
    def _():
        for cp in ahead_copies(0, 0):
            cp.start()

    @pl.when(b + 1 < n_seq)
    def _():
        for cp in ahead_copies(b + 1, 1 - slot):
            cp.start()

    def page(z, weigh, mask):
        lk = -_softplus(z)
        if mask is not None:
            lk = jnp.where(mask, lk, 0.0)
        sums = _suffix_sums(lk, tri)
        w = jnp.exp(z + sums[:, :PAGE] + car_ref[...])
        if mask is not None:
            w = jnp.where(mask, w, 0.0)
        acc_ref[...] += weigh(w.astype(BF16))
        car = car_ref[...] + sums[:, PAGE:]
        car_ref[...] = car
        flag[0] = (jnp.max(car) >= EXP_ZERO).astype(jnp.int32)

    kbuf[...] = jnp.zeros_like(kbuf)
    vbuf[...] = jnp.zeros_like(vbuf)
    kbuf[0:t_new, :] = kn_ref[...]
    vbuf[0:t_new, :] = vn_ref[...]
    acc_ref[...] = jnp.zeros_like(acc_ref)
    car_ref[...] = jnp.zeros_like(car_ref)
    r = lax.broadcasted_iota(jnp.int32, (QROWS, PAGE), 0)
    c = lax.broadcasted_iota(jnp.int32, (QROWS, PAGE), 1)
    past = jnp.logical_and(c < r % T_PAD, c < t_new)
    page(_dot_nt(qbd, kbuf[...].astype(BF16)), lambda w: _dot(w, vbuf[...].astype(BF16)), past)

    def cache_page(k_page, v_page):
        vt = v_page.reshape(ATT_WIDTH, PAGE).astype(BF16)
        page(_dot(qbd, k_page.reshape(ATT_WIDTH, PAGE).astype(BF16)), lambda w: _dot_nt(w, vt), None)

    for cp in ahead_copies(b, slot):
        cp.wait()
    for j in range(N_AHEAD):
        @pl.when(flag[0] > 0)
        def _():
            cache_page(kpg[slot, j], vpg[slot, j])

    for p in reversed(range(n_pages - N_AHEAD)):
        @pl.when(flag[0] > 0)
        def _():
            cps = demand_copies(p)
            for cp in cps:
                cp.start()
            for cp in cps:
                cp.wait()
            cache_page(kx[...], vx[...])

    o_ref[...] = _out_from_acc(acc_ref[...]).astype(o_ref.dtype)


def _page_specs(layer, n_pages):
    return [pl.BlockSpec((None, None, N_HEADS, HEAD_DIM, PAGE),
                         functools.partial(lambda b, pt, p: (layer, pt[b, p], 0, 0, 0), p=p))
            for p in range(n_pages)]


def sb_sample(page_table, us8, cache_k, cache_v, layer, t_new):
    n_seq, n_pages = page_table.shape
    assert n_pages >= N_AHEAD
    tri = _tri_ge_ones(PAGE)
    tok = lambda cb: pl.BlockSpec((None, t_new, ATT_WIDTH), lambda b, pt: (b, 0, cb // 4))
    const = lambda a: pl.BlockSpec(a.shape, lambda b, pt: (0,) * a.ndim)
    page_shape = (N_HEADS, HEAD_DIM, PAGE)
    grid_spec = pltpu.PrefetchScalarGridSpec(
        num_scalar_prefetch=1,
        grid=(n_seq,),
        in_specs=[tok(CB_Q_SB), tok(CB_K_SB), tok(CB_V_SB),
                  pl.BlockSpec(memory_space=pl.ANY), pl.BlockSpec(memory_space=pl.ANY),
                  const(tri)],
        out_specs=pl.BlockSpec((None, T_PAD, ATT_WIDTH), lambda b, pt: (b, 0, 0)),
        scratch_shapes=[pltpu.VMEM((PAGE, ATT_WIDTH), F32), pltpu.VMEM((PAGE, ATT_WIDTH), F32),
                        pltpu.VMEM((QROWS, ATT_WIDTH), F32), pltpu.VMEM((QROWS, LANES), F32),
                        pltpu.SMEM((1,), jnp.int32),
                        pltpu.VMEM((2, N_AHEAD) + page_shape, F32),
                        pltpu.VMEM((2, N_AHEAD) + page_shape, F32),
                        pltpu.VMEM(page_shape, F32), pltpu.VMEM(page_shape, F32),
                        pltpu.SemaphoreType.DMA((2, N_AHEAD, 2)), pltpu.SemaphoreType.DMA((2,))],
    )
    return pl.pallas_call(
        functools.partial(_sb_sample_body, n_pages=n_pages, t_new=t_new, layer=layer),
        grid_spec=grid_spec,
        out_shape=jax.ShapeDtypeStruct((n_seq, T_PAD, ATT_WIDTH), BF16),
        compiler_params=_cparams(("arbitrary",)),
        name="sb_sample",
    )(page_table, us8, us8, us8, cache_k, cache_v, tri)


def _moba_sample_body(pt_ref, q_ref, kn_ref, vn_ref, *rest, n_pages, t_new):
    k_refs = rest[:n_pages]
    v_refs = rest[n_pages:2 * n_pages]
    bias_ref, o_ref, kbuf, vbuf, s_ref = rest[2 * n_pages:]
    ppb = MOBA_BLOCK // PAGE
    n_blk = n_pages // ppb
    scale = HEAD_DIM ** -0.5
    qf = _q_block_diag(_pad_rows(q_ref, kbuf))
    qs = (qf * scale).astype(BF16)
    neg = -jnp.inf
    lane = lax.broadcasted_iota(jnp.int32, (QROWS, LANES), 1)
    lane_w = lax.broadcasted_iota(jnp.int32, (ATT_WIDTH, LANES), 1)

    ksum = jnp.zeros((ATT_WIDTH, LANES), F32)
    for n in range(n_blk):
        blk = k_refs[ppb * n][...].reshape(ATT_WIDTH, PAGE)
        for j in range(1, ppb):
            blk = blk + k_refs[ppb * n + j][...].reshape(ATT_WIDTH, PAGE)
        ksum = jnp.where(lane_w == n, jnp.sum(blk, axis=-1, keepdims=True), ksum)
    kmean = ksum * (1.0 / MOBA_BLOCK)
    qh, ql = _split2(qf)
    mh, ml = _split2(kmean)
    route = _dot(qh, mh) + _dot(qh, ml) + _dot(ql, mh)
    sel = _route_topk(route, lane, n_blk, n_blk)

    mrun = jnp.full((QROWS, PAGE), neg, F32)
    for p in range(n_pages):
        s = _dot(qs, k_refs[p][...].reshape(ATT_WIDTH, PAGE).astype(BF16)) + bias_ref[p]
        s = jnp.where(sel[:, p // ppb:p // ppb + 1] > 0.5, s, neg)
        s_ref[p] = s
        mrun = jnp.maximum(mrun, s)
    kbuf[...] = jnp.zeros_like(kbuf)
    vbuf[...] = jnp.zeros_like(vbuf)
    kbuf[0:t_new, :] = kn_ref[...]
    vbuf[0:t_new, :] = vn_ref[...]
    r = lax.broadcasted_iota(jnp.int32, (QROWS, PAGE), 0)
    c = lax.broadcasted_iota(jnp.int32, (QROWS, PAGE), 1)
    causal = jnp.logical_and(c <= r % T_PAD, c < T_PAD)
    s = _dot_nt(qs, kbuf[...].astype(BF16)) + bias_ref[n_pages]
    s = jnp.where(causal, s, neg)
    s_ref[n_pages] = s
    m = jnp.max(jnp.maximum(mrun, s), axis=-1, keepdims=True)

    lsum = jnp.zeros((QROWS, PAGE), F32)
    acc = jnp.zeros((QROWS, ATT_WIDTH), F32)
    for p in range(n_pages + 1):
        pe = jnp.exp(s_ref[p] - m)
        lsum = lsum + pe
        pe = pe.astype(BF16)
        if p < n_pages:
            acc = acc + _dot_nt(pe, v_refs[p][...].reshape(ATT_WIDTH, PAGE).astype(BF16))
        else:
            acc = acc + _dot(pe, vbuf[...].astype(BF16))
    o_ref[...] = _out_from_acc(acc / jnp.sum(lsum, axis=-1, keepdims=True)).astype(o_ref.dtype)


def moba_sample_bias_idx(n_pages):
    t = np.arange(T_PAD)[:, None]
    s = np.arange(PAGE)[None, :]
    past = [_rel_bucket_np((n_pages - p) * PAGE + t - s) for p in range(n_pages)]
    return np.stack(past + [_rel_bucket_np(t - s)])


def moba_sample(page_table, us8, cache_k, cache_v, bias, layer, t_new):
    n_seq, n_pages = page_table.shape
    tok = lambda cb: pl.BlockSpec((None, t_new, ATT_WIDTH), lambda b, pt: (b, 0, cb // 4))
    const = lambda a: pl.BlockSpec(a.shape, lambda b, pt: (0,) * a.ndim)
    grid_spec = pltpu.PrefetchScalarGridSpec(
        num_scalar_prefetch=1,
        grid=(n_seq,),
        in_specs=[tok(CB_Q_MB), tok(CB_K_MB), tok(CB_V_MB)]
        + _page_specs(layer, n_pages) + _page_specs(layer, n_pages)
        + [const(bias)],
        out_specs=pl.BlockSpec((None, T_PAD, ATT_WIDTH), lambda b, pt: (b, 0, 0)),
        scratch_shapes=[pltpu.VMEM((PAGE, ATT_WIDTH), F32), pltpu.VMEM((PAGE, ATT_WIDTH), F32),
                        pltpu.VMEM((n_pages + 1, QROWS, PAGE), F32)],
    )
    return pl.pallas_call(
        functools.partial(_moba_sample_body, n_pages=n_pages, t_new=t_new),
        grid_spec=grid_spec,
        out_shape=jax.ShapeDtypeStruct((n_seq, T_PAD, ATT_WIDTH), BF16),
        compiler_params=_cparams(("arbitrary",)),
        name="moba_sample",
    )(page_table, us8, us8, us8, *([cache_k] * n_pages), *([cache_v] * n_pages), bias)


def kernel(x_prompt, x_sample, cache_k_sb, cache_v_sb, cache_k_moba, cache_v_moba, page_table, state_ssm, state_conv, rel_bias, g_mix, w_in, conv_w, conv_b, dt_bias, a_log, d_skip, g_ssm, w_branch_sb, w_branch_moba, w_branch_ssm, w_out, g_ffn, w_router_group, b_router_group, w_router_expert, b_router_expert, w_expert_gate, w_expert_up, w_expert_down, g_final):
    bsz, seq, d = x_prompt.shape
    n_seq, t_new, _ = x_sample.shape
    depth = w_in.shape[0]
    n_pages = page_table.shape[1]
    n_p = bsz * seq
    n_s = n_seq * t_new
    assert d == D_MODEL and seq % MOBA_BLOCK == 0 and t_new <= T_PAD
    assert (n_pages * PAGE) % MOBA_BLOCK == 0 and cache_k_sb.shape[2] == PAGE

    x_p = x_prompt.reshape(n_p, d)
    x_s = x_sample.reshape(n_s, d)
    tm_p = 1024 if n_p % 1024 == 0 else 512
    tm_s = min(n_s, 512)
    n_c = seq // SSD_CHUNK
    pages = [c.transpose(0, 1, 3, 4, 2) for c in (cache_k_sb, cache_v_sb, cache_k_moba, cache_v_moba)]
    tq_mb = MOBA_BLOCK
    bias_p = bias_tiles(rel_bias, jnp.asarray(moba_prompt_bias_idx(tq_mb)))
    bias_s = bias_tiles(rel_bias, jnp.asarray(moba_sample_bias_idx(n_pages))).reshape(
        n_pages + 1, QROWS, PAGE)
    zero_conv = jnp.zeros((1, bsz, SUBLANES, CONV_DIM), F32)
    zero_state = jnp.zeros((1, bsz, SSM_HEADS, SSM_P, SSM_N), F32)
    n_main = CB_XBC * LANES + CONV_DIM

    def col(a, cb, w):
        return a[..., cb * LANES:cb * LANES + w]

    def kv(a, cb):
        return col(a, cb, ATT_WIDTH).reshape(a.shape[0], a.shape[1], N_HEADS, HEAD_DIM)

    def lane_pad(v):
        return jnp.pad(v, ((0, 0), (0, LANES - v.shape[1])))[:, None, :]

    w_main = jnp.concatenate([w_in[:, :, :n_main], w_in[:, :, n_main + SSM_HEADS:]], axis=2).astype(BF16)
    w_dt = jnp.pad(w_in[:, :, n_main:n_main + SSM_HEADS],
                   ((0, 0), (0, 0), (0, LANES - SSM_HEADS))).astype(BF16)
    gm, gs, gf = g_mix[:, None, :], g_ssm[:, None, :], g_ffn[:, None, :]
    ssm_par = (conv_w, conv_b[:, None, :], lane_pad(dt_bias), lane_pad(a_log), lane_pad(d_skip))
    lane_fill = LANES - N_EXPERTS - N_GROUPS
    wr = jnp.concatenate([
        jnp.pad(jnp.concatenate([w_router_expert, w_router_group], axis=2),
                ((0, 0), (0, 0), (0, lane_fill))),
        jnp.pad(jnp.concatenate([b_router_expert, b_router_group], axis=1)[:, None, :],
                ((0, 0), (0, SUBLANES - 1), (0, lane_fill)))], axis=1)
    wgu = jnp.concatenate([w_expert_gate, w_expert_up], axis=-1).astype(BF16)
    wd = w_expert_down.astype(BF16)
    w_br = (w_branch_sb.astype(BF16), w_branch_moba.astype(BF16),
            w_branch_ssm.astype(BF16), w_out.astype(BF16))
    cprev_s = jnp.pad(state_conv, ((0, 0), (0, 0), (SUBLANES - (CONV_W - 1), 0), (0, 0)))
    g_fin = g_final.reshape(1, d)

    new_p, new_s = [], []
    for l in range(depth):
        def tail(x, osb, omb, y, u, tm):
            x1 = merge(x, osb, omb, y, u, gs, *w_br, l, tm=min(tm, 512))
            xn, cmb = router(x1, gf, wr, l, tm=min(tm, 512))
            return moe(x1, xn, cmb, wgu, wd, g_fin, l, final_norm=(l == depth - 1), tm=tm)

        u_p, dt_p, qkv_p = in_proj(x_p, gm, w_main, w_dt, l, tm=tm_p)
        osb_p = sb_prompt(qkv_p, qkv_p, qkv_p, CB_Q_SB, CB_K_SB, CB_V_SB, bsz, seq)
        km = kmean_prompt(u_p, CB_K_MB, bsz, seq)
        omb_p = moba_prompt(u_p, qkv_p, qkv_p, km, rel_bias, bias_p, CB_Q_MB, CB_K_MB, CB_V_MB,
                            bsz, seq, tq=tq_mb)
        y_p, ssm_p = ssd(u_p.reshape(bsz * n_c, SSD_CHUNK, U_COLS), CB_XBC,
                         dt_p.reshape(bsz * n_c, SSD_CHUNK, LANES), zero_conv, zero_state, 0,
                         *ssm_par, l, bsz=bsz, n_c=n_c)
        x_p = tail(x_p, osb_p, omb_p, y_p.reshape(n_p, SSM_INNER), u_p, tm_p)

        u_s, dt_s, _ = in_proj(x_s, gm, w_main, w_dt, l, tm=tm_s)
        u_s3 = u_s.reshape(n_seq, t_new, U_COLS)
        osb_s = sb_sample(page_table, u_s3, pages[0], pages[1], l, t_new)[:, :t_new]
        omb_s = moba_sample(page_table, u_s3, pages[2], pages[3], bias_s, l, t_new)[:, :t_new]
        y_s, ssm_s = ssd(u_s3, CB_XBC, dt_s.reshape(n_seq, t_new, LANES), cprev_s, state_ssm, l,
                         *ssm_par, l, bsz=n_seq, n_c=1, q=T_PAD)
        x_s = tail(x_s, osb_s.reshape(n_s, ATT_WIDTH), omb_s.reshape(n_s, ATT_WIDTH),
                   y_s[:, :t_new].reshape(n_s, SSM_INNER), u_s, tm_s)

        u_p3 = u_p.reshape(bsz, seq, U_COLS)
        new_p.append((kv(u_p3, CB_K_SB), kv(u_p3, CB_V_SB), kv(u_p3, CB_K_MB), kv(u_p3, CB_V_MB),
                      ssm_p, col(u_p3, CB_XBC, CONV_DIM)[:, seq - (CONV_W - 1):]))
        conv_s = jnp.concatenate([state_conv[l], col(u_s3, CB_XBC, CONV_DIM)], axis=1)[:, t_new:]
        new_s.append((kv(u_s3, CB_K_SB), kv(u_s3, CB_V_SB), kv(u_s3, CB_K_MB), kv(u_s3, CB_V_MB),
                      ssm_s, conv_s))

    stack = lambda states: tuple(jnp.stack([s[i] for s in states]) for i in range(6))
    return ((x_p.reshape(bsz, seq, d), x_s.reshape(n_seq, t_new, d)) + stack(new_p) + stack(new_s))
```

```python
import functools
import math

import numpy as np
import jax
import jax.numpy as jnp
from jax import lax
from jax.experimental import pallas as pl
from jax.experimental.pallas import tpu as pltpu

F32 = jnp.float32
BF16 = jnp.bfloat16

LANES = 128
SUBLANES = 8
VMEM_LIMIT = 56 * 1024 * 1024

D_MODEL = 1024
HEAD_DIM = 64
N_HEADS = 8
ATT_WIDTH = N_HEADS * HEAD_DIM
MOBA_BLOCK = 256
MOBA_TOPK = 3
PAGE = 128
SSM_HEADS = 16
SSM_P = 64
SSM_GROUPS = 4
SSM_N = 128
SSM_INNER = SSM_HEADS * SSM_P
CONV_W = 4
CONV_DIM = SSM_INNER + 2 * SSM_GROUPS * SSM_N
SSD_CHUNK = 128
REL_BUCKETS = 32
REL_MAX_DIST = 128
N_GROUPS = 4
EPG = 8
N_EXPERTS = N_GROUPS * EPG
EXPERT_FF = 256
EPS = 1e-6
EXP_ZERO = -88.0
NEG_BIG = -1e30

CB_Q_SB, CB_K_SB, CB_V_SB = 0, 4, 8
CB_Q_MB, CB_K_MB, CB_V_MB = 12, 16, 20
CB_Z = 24
CB_XBC = 32
CB_GATE = 48
U_COLS = 72 * LANES


def _cparams(sem):
    return pltpu.CompilerParams(dimension_semantics=sem, vmem_limit_bytes=VMEM_LIMIT)


def _split2(a):
    hi = a.astype(BF16)
    lo = (a - hi.astype(F32)).astype(BF16)
    return hi, lo


def _dot(a, b):
    return jnp.dot(a, b, preferred_element_type=F32)


def _dot_nt(a, b):
    return lax.dot_general(a, b, (((1,), (1,)), ((), ())), preferred_element_type=F32)


def _dot2(a, b_exact):
    hi, lo = _split2(a)
    return _dot(hi, b_exact) + _dot(lo, b_exact)


def _dot2_nt(a_exact, b):
    hi, lo = _split2(b)
    return _dot_nt(a_exact, hi) + _dot_nt(a_exact, lo)


def _dot3_nt(a, b):
    ah, al = _split2(a)
    bh, bl = _split2(b)
    return _dot_nt(ah, bh) + _dot_nt(ah, bl) + _dot_nt(al, bh)


def _sigmoid(x):
    return 1.0 / (1.0 + jnp.exp(-x))


def _softplus(x):
    return jnp.maximum(x, 0.0) + jnp.log(1.0 + jnp.exp(-jnp.abs(x)))


def _rms(x, g):
    ms = jnp.mean(x * x, axis=-1, keepdims=True)
    return x * lax.rsqrt(ms + EPS) * g


def _in_proj_body(x_ref, g_ref, w_ref, wdt_ref, u_ref, dt_ref, qkv_ref, xn_ref, *, n_qkv):
    j = pl.program_id(1)

    @pl.when(j == 0)
    def _():
        xn = _rms(x_ref[...], g_ref[...]).astype(BF16)
        xn_ref[...] = xn
        dt_ref[...] = _dot(xn, wdt_ref[...])

    u = _dot(xn_ref[...], w_ref[...])
    u_ref[...] = u

    @pl.when(j < n_qkv)
    def _():
        qkv_ref[...] = u.astype(BF16)


def in_proj(x, g, w, wdt, layer, tm=512, tn=1024):
    n, d = x.shape
    cols = w.shape[2]
    n_qkv = (CB_Z * LANES) // tn
    return pl.pallas_call(
        functools.partial(_in_proj_body, n_qkv=n_qkv),
        grid=(n // tm, cols // tn),
        in_specs=[
            pl.BlockSpec((tm, d), lambda i, j: (i, 0)),
            pl.BlockSpec((None, 1, d), lambda i, j: (layer, 0, 0)),
            pl.BlockSpec((None, d, tn), lambda i, j: (layer, 0, j)),
            pl.BlockSpec((None, d, LANES), lambda i, j: (layer, 0, 0)),
        ],
        out_specs=[
            pl.BlockSpec((tm, tn), lambda i, j: (i, j)),
            pl.BlockSpec((tm, LANES), lambda i, j: (i, 0)),
            pl.BlockSpec((tm, tn), lambda i, j: (i, jnp.minimum(j, n_qkv - 1))),
        ],
        out_shape=[jax.ShapeDtypeStruct((n, cols), F32),
                   jax.ShapeDtypeStruct((n, LANES), F32),
                   jax.ShapeDtypeStruct((n, n_qkv * tn), BF16)],
        scratch_shapes=[pltpu.VMEM((tm, d), BF16)],
        compiler_params=_cparams(("parallel", "arbitrary")),
        name="in_proj",
    )(x, g, w, wdt)


def _suffix_sums(lk, tri2):
    hi, lo = _split2(lk)
    return _dot(jnp.concatenate([hi, lo], axis=1), tri2)


def _sb_tile(qh, k, v, carry, tri2, mask):
    tk = k.shape[0]
    z = _dot_nt(qh, k)
    lk = -_softplus(z)
    if mask is not None:
        lk = jnp.where(mask, lk, 0.0)
    sums = _suffix_sums(lk, tri2)
    w = jnp.exp(z + sums[:, :tk] + carry)
    if mask is not None:
        w = jnp.where(mask, w, 0.0)
    return _dot(w.astype(BF16), v), carry + sums[:, tk:]


def _stack_heads(q, scale):
    lane = lax.broadcasted_iota(jnp.int32, q.shape, 1)
    q = q * scale
    return jnp.concatenate([jnp.where(lane < HEAD_DIM, q, 0.0),
                            jnp.where(lane >= HEAD_DIM, q, 0.0)], axis=0).astype(BF16)


def _unstack_heads(o, tq):
    lane = lax.broadcasted_iota(jnp.int32, (tq, LANES), 1)
    return jnp.where(lane < HEAD_DIM, o[:tq], o[tq:])


def _sb_prompt_body(q_ref, k_ref, v_ref, tri_ref, o_ref, acc_ref, car_ref, *, tk):
    i = pl.program_id(2)
    scale = HEAD_DIM ** -0.5
    q = q_ref[...].astype(F32)
    qa = jnp.concatenate([_stack_heads(q[:tk], scale), _stack_heads(q[tk:], scale)], axis=0)
    hb = 2 * tk
    tri2 = tri_ref[...]
    row = lax.broadcasted_iota(jnp.int32, (2 * hb, tk), 0)
    col = lax.broadcasted_iota(jnp.int32, (2 * hb, tk), 1)
    tri_mask = col < row % tk

    def tile(j, carry, mask):
        kj = pl.multiple_of(j * tk, tk)
        return _sb_tile(qa, k_ref[pl.ds(kj, tk), :], v_ref[pl.ds(kj, tk), :], carry, tri2, mask)

    tiles = [(2 * i + 1, qa[hb:], tri_mask[hb:]),
             (2 * i, qa, jnp.logical_or(row >= hb, tri_mask)),
             (jnp.maximum(2 * i - 1, 0), qa, jnp.broadcast_to(i > 0, (2 * hb, tk)))]
    starts = [pl.multiple_of(j * tk, tk) for j, _, _ in tiles]
    zs = [_dot_nt(qr, k_ref[pl.ds(s, tk), :]) for s, (_, qr, _) in zip(starts, tiles)]
    lks = [jnp.where(mask, -_softplus(z), 0.0) for z, (_, _, mask) in zip(zs, tiles)]
    sums = [_suffix_sums(lk, tri2) for lk in lks]
    ws = []
    c = jnp.zeros((hb, tk), F32)
    for t, (z, sm, (_, _, mask)) in enumerate(zip(zs, sums, tiles)):
        ws.append(jnp.where(mask, jnp.exp(z + sm[:, :tk] + c), 0.0).astype(BF16))
        c = c + sm[:, tk:]
        if t == 0:
            c = jnp.concatenate([jnp.zeros((hb, tk), F32), c], axis=0)
    os_ = [_dot(w, v_ref[pl.ds(s, tk), :]) for w, s in zip(ws, starts)]
    acc_ref[...] = os_[1] + os_[2]
    acc_ref[hb:, :] += os_[0]
    car_ref[...] = c

    def cond(j):
        return jnp.logical_and(j >= 0, jnp.max(car_ref[...]) >= EXP_ZERO)

    def body(j):
        o, c = tile(j, car_ref[...], None)
        acc_ref[...] += o
        car_ref[...] = c
        return j - 1

    lax.while_loop(cond, body, 2 * i - 2)
    o_ref[0:tk, :] = _unstack_heads(acc_ref[0:hb, :], tk).astype(o_ref.dtype)
    o_ref[tk:, :] = _unstack_heads(acc_ref[hb:, :], tk).astype(o_ref.dtype)


def _tri_ge_ones(n):
    t2 = np.concatenate([np.tril(np.ones((n, n), np.float32)), np.ones((n, n), np.float32)], axis=1)
    return jnp.asarray(np.concatenate([t2, t2], axis=0), BF16)


def sb_prompt(qa, ka, va, cbq, cbk, cbv, bsz, t_len, tk=128):
    tq = 2 * tk
    n_q = t_len // tq
    n_hp = N_HEADS // 2
    return pl.pallas_call(
        functools.partial(_sb_prompt_body, tk=tk),
        grid=(bsz, n_hp, n_q),
        in_specs=[
            pl.BlockSpec((tq, LANES), lambda b, p, i: (b * n_q + i, cbq + p)),
            pl.BlockSpec((t_len, LANES), lambda b, p, i: (b, cbk + p)),
            pl.BlockSpec((t_len, LANES), lambda b, p, i: (b, cbv + p)),
            pl.BlockSpec((2 * tk, 2 * tk), lambda b, p, i: (0, 0)),
        ],
        out_specs=pl.BlockSpec((tq, LANES), lambda b, p, i: (b * n_q + i, p)),
        out_shape=jax.ShapeDtypeStruct((bsz * t_len, ATT_WIDTH), BF16),
        scratch_shapes=[pltpu.VMEM((2 * tq, LANES), F32), pltpu.VMEM((2 * tq, tk), F32)],
        compiler_params=_cparams(("parallel", "parallel", "arbitrary")),
        name="sb_prompt",
    )(qa, ka, va, _tri_ge_ones(tk))


def _rel_bucket_np(dist):
    exact = REL_BUCKETS // 2
    d = np.maximum(dist, 0)
    df = np.maximum(d, 1).astype(np.float32)
    large = exact + (np.log(df / np.float32(exact)) / np.float32(math.log(REL_MAX_DIST / exact))
                     * np.float32(REL_BUCKETS - exact)).astype(np.int32)
    return np.where(d < exact, d, np.minimum(large, REL_BUCKETS - 1)).astype(np.int32)


def _bias_body(rb_ref, idx_ref, o_ref):
    idx = idx_ref[...]
    for h in range(N_HEADS):
        acc = jnp.zeros(idx.shape, F32)
        for b in range(REL_BUCKETS):
            acc = jnp.where(idx == b, rb_ref[b, h], acc)
        o_ref[h] = acc


def bias_tiles(rel_bias, idx):
    nv, r, c = idx.shape
    return pl.pallas_call(
        _bias_body,
        grid=(nv,),
        in_specs=[pl.BlockSpec(memory_space=pltpu.SMEM),
                  pl.BlockSpec((None, r, c), lambda v: (v, 0, 0))],
        out_specs=pl.BlockSpec((None, N_HEADS, r, c), lambda v: (v, 0, 0, 0)),
        out_shape=jax.ShapeDtypeStruct((nv, N_HEADS, r, c), F32),
        compiler_params=_cparams(("parallel",)),
        name="bias_tiles",
    )(rel_bias, idx)


def _kmean_body(k_ref, o_ref):
    n = pl.program_id(1)

    @pl.when(n == 0)
    def _():
        o_ref[...] = jnp.zeros_like(o_ref)

    o_ref[pl.ds(n, 1), :] = jnp.mean(k_ref[...], axis=0, keepdims=True)


def kmean_prompt(ka, cbk, bsz, t_len):
    n_blk = t_len // MOBA_BLOCK
    return pl.pallas_call(
        _kmean_body,
        grid=(bsz, n_blk),
        in_specs=[pl.BlockSpec((MOBA_BLOCK, ATT_WIDTH), lambda b, n: (b * n_blk + n, cbk // 4))],
        out_specs=pl.BlockSpec((None, LANES, ATT_WIDTH), lambda b, n: (b, 0, 0)),
        out_shape=jax.ShapeDtypeStruct((bsz, LANES, ATT_WIDTH), F32),
        compiler_params=_cparams(("parallel", "arbitrary")),
        name="kmean_prompt",
    )(ka)


def _route_topk(route, lane, n_valid, n_ok):
    neg = -jnp.inf
    r = jnp.where(lane < n_valid, route, neg)
    sel = jnp.zeros(route.shape, F32)
    for j in range(MOBA_TOPK):
        m = jnp.max(r, axis=-1, keepdims=True)
        idx = jnp.min(jnp.where(r == m, lane, LANES), axis=-1, keepdims=True)
        hit = lane == idx
        sel = jnp.where(jnp.logical_and(hit, j < n_ok), 1.0, sel)
        r = jnp.where(hit, neg, r)
    return sel


def _block_penalty_t(route_t, own):
    blk = lax.broadcasted_iota(jnp.int32, route_t.shape, 0)
    neg = -jnp.inf
    r = jnp.where(blk < own, route_t, neg)
    keep = blk == own
    for j in range(MOBA_TOPK):
        m = jnp.max(r, axis=0, keepdims=True)
        idx = jnp.min(jnp.where(r == m, blk, route_t.shape[0]), axis=0, keepdims=True)
        hit = blk == idx
        keep = jnp.logical_or(keep, jnp.logical_and(hit, j < own))
        r = jnp.where(hit, neg, r)
    return jnp.where(keep, 0.0, NEG_BIG)


ROUTE_TILES = 8


def _moba_route_body(q_ref, eye_ref, km_ref, pen_ref, *, tq, n_blk):
    g = pl.program_id(2)
    per = MOBA_BLOCK // tq
    km = km_ref[0:n_blk, :]
    q = q_ref[...]
    lane = lax.broadcasted_iota(jnp.int32, (tq, LANES), 1)
    qf = jnp.concatenate(
        [jnp.where((lane < HEAD_DIM) if h == 0 else (lane >= HEAD_DIM), q[t * tq:(t + 1) * tq], 0.0)
         for t in range(ROUTE_TILES) for h in range(2)], axis=0)
    n_col = 2 * ROUTE_TILES * tq
    tile_of_col = lax.broadcasted_iota(jnp.int32, (1, n_col), 1) // (2 * tq)
    own = (g * ROUTE_TILES + tile_of_col) // per
    pen_t = _block_penalty_t(_dot3_nt(km, qf), own)
    pen_t = jnp.concatenate([pen_t, jnp.zeros((LANES - n_blk, n_col), F32)], axis=0).astype(BF16)
    for t in range(ROUTE_TILES):
        for h in range(2):
            c0 = (2 * t + h) * tq
            pen = _dot_nt(eye_ref[...], pen_t[:, c0:c0 + tq])
            pen_ref[t, h * tq:(h + 1) * tq, :] = pen.astype(BF16)


def moba_route(qa, kmean, cbq, bsz, t_len, tq):
    n_q = t_len // tq
    n_g = n_q // ROUTE_TILES
    n_hp = N_HEADS // 2
    n_blk = t_len // MOBA_BLOCK
    assert n_blk <= LANES and n_q % ROUTE_TILES == 0
    eye = jnp.asarray(np.eye(tq, dtype=np.float32), BF16)
    return pl.pallas_call(
        functools.partial(_moba_route_body, tq=tq, n_blk=n_blk),
        grid=(bsz, n_hp, n_g),
        in_specs=[
            pl.BlockSpec((ROUTE_TILES * tq, LANES), lambda b, p, g: (b * n_g + g, cbq + p)),
            pl.BlockSpec((tq, tq), lambda b, p, g: (0, 0)),
            pl.BlockSpec((None, LANES, LANES), lambda b, p, g: (b, 0, p)),
        ],
        out_specs=pl.BlockSpec((ROUTE_TILES, 2 * tq, LANES),
                               lambda b, p, g: ((b * n_hp + p) * n_g + g, 0, 0)),
        out_shape=jax.ShapeDtypeStruct((bsz * n_hp * n_q, 2 * tq, LANES), BF16),
        compiler_params=_cparams(("parallel", "parallel", "parallel")),
        name="moba_route",
    )(qa, eye, kmean)


def _moba_prompt_body(rb_ref, q_ref, pen_ref, k_ref, v_ref, e_ref, bias_ref, o_ref,
                      qa_ref, s_ref, mrun_ref, lrun_ref, acc_ref, *, tq, n_blk):
    p = pl.program_id(1)
    i = pl.program_id(2)
    per = MOBA_BLOCK // tq
    own = i // per
    par = i % per
    qa_ref[...] = jnp.concatenate(
        [_stack_heads(q_ref[...].astype(F32), HEAD_DIM ** -0.5), pen_ref[...]], axis=1)
    qa = qa_ref[...]

    row = lax.broadcasted_iota(jnp.int32, (2 * tq, MOBA_BLOCK), 0)
    col = lax.broadcasted_iota(jnp.int32, (2 * tq, MOBA_BLOCK), 1)
    causal = col <= row % tq + par * tq
    far_bias = jnp.where(row[:, 0:1] < tq, rb_ref[REL_BUCKETS - 1, 2 * p],
                         rb_ref[REL_BUCKETS - 1, 2 * p + 1])

    def logits(n, bias):
        n0 = pl.multiple_of(n * MOBA_BLOCK, MOBA_BLOCK)
        kaug = jnp.concatenate([k_ref[pl.ds(n0, MOBA_BLOCK), :], e_ref[pl.ds(n0, MOBA_BLOCK), :]],
                               axis=1)
        return _dot_nt(qa, kaug) + bias

    prev = jnp.maximum(own - 1, 0)
    prev_slot = jnp.where(own >= 1, own - 1, n_blk)
    n_far = prev

    def far_loop(step, group):
        def grouped(g, carry):
            step([group * g + j for j in range(group)])
            return carry

        lax.fori_loop(0, n_far // group, grouped, 0)
        base = (n_far // group) * group
        size = group // 2
        while size >= 1:
            take = (n_far - base) >= size

            @pl.when(take)
            def _(base=base, size=size):
                step([base + j for j in range(size)])

            base = base + jnp.where(take, size, 0)
            size //= 2

    def fold(ss, op):
        out = None
        for s in ss:
            h = op(s[:, :LANES], s[:, LANES:])
            out = h if out is None else op(out, h)
        return out

    s_own = jnp.where(causal, logits(own, bias_ref[par, 0].reshape(2 * tq, MOBA_BLOCK)), NEG_BIG)
    s_prev = jnp.where(own >= 1, logits(prev, bias_ref[par, 1].reshape(2 * tq, MOBA_BLOCK)), NEG_BIG)
    s_ref[own] = s_own
    s_ref[prev_slot] = s_prev
    mrun_ref[...] = fold([s_own, s_prev], jnp.maximum)

    def far_max(ns):
        ss = [logits(n, far_bias) for n in ns]
        for n, s in zip(ns, ss):
            s_ref[n] = s
        mrun_ref[...] = jnp.maximum(mrun_ref[...], fold(ss, jnp.maximum))

    far_loop(far_max, 4)
    m = jnp.max(mrun_ref[...], axis=-1, keepdims=True)

    def weigh(slots, ns):
        pes = [jnp.exp(s_ref[sl] - m) for sl in slots]
        o = None
        for pe, n in zip(pes, ns):
            n0 = pl.multiple_of(n * MOBA_BLOCK, MOBA_BLOCK)
            on = _dot(pe.astype(BF16), v_ref[pl.ds(n0, MOBA_BLOCK), :])
            o = on if o is None else o + on
        return fold(pes, jnp.add), o

    lrun_ref[...], acc_ref[...] = weigh([own, prev_slot], [own, prev])

    def far_acc(ns):
        l, o = weigh(ns, ns)
        lrun_ref[...] += l
        acc_ref[...] += o

    far_loop(far_acc, 4)
    o = acc_ref[...] / jnp.sum(lrun_ref[...], axis=-1, keepdims=True)
    o_ref[...] = _unstack_heads(o, tq).astype(o_ref.dtype)


def moba_prompt_bias_idx(tq):
    per = MOBA_BLOCK // tq
    t = np.arange(tq)[:, None]
    s = np.arange(MOBA_BLOCK)[None, :]
    idx = np.stack([np.stack([_rel_bucket_np(par * tq + kind * MOBA_BLOCK + t - s)
                              for kind in range(2)]) for par in range(per)])
    return idx.reshape(per * 2, tq, MOBA_BLOCK)


def moba_prompt(qa, ka, va, pen, rel_bias, bias, cbq, cbk, cbv, bsz, t_len, tq=128):
    n_q = t_len // tq
    n_hp = N_HEADS // 2
    per = MOBA_BLOCK // tq
    bias = bias.reshape(per, 2, N_HEADS, tq, MOBA_BLOCK)
    n_blk = t_len // MOBA_BLOCK
    blk_of_row = np.arange(t_len) // MOBA_BLOCK
    onehot = jnp.asarray(blk_of_row[:, None] == np.arange(LANES)[None, :], BF16)
    return pl.pallas_call(
        functools.partial(_moba_prompt_body, tq=tq, n_blk=n_blk),
        grid=(bsz, n_hp, n_q),
        in_specs=[
            pl.BlockSpec(memory_space=pltpu.SMEM),
            pl.BlockSpec((tq, LANES), lambda b, p, i: (b * n_q + i, cbq + p)),
            pl.BlockSpec((None, 2 * tq, LANES), lambda b, p, i: ((b * n_hp + p) * n_q + i, 0, 0)),
            pl.BlockSpec((t_len, LANES), lambda b, p, i: (b, cbk + p)),
            pl.BlockSpec((t_len, LANES), lambda b, p, i: (b, cbv + p)),
            pl.BlockSpec((t_len, LANES), lambda b, p, i: (0, 0)),
            pl.BlockSpec((per, 2, 2, tq, MOBA_BLOCK), lambda b, p, i: (0, 0, p, 0, 0)),
        ],
        out_specs=pl.BlockSpec((tq, LANES), lambda b, p, i: (b * n_q + i, p)),
        out_shape=jax.ShapeDtypeStruct((bsz * t_len, ATT_WIDTH), BF16),
        scratch_shapes=[pltpu.VMEM((2 * tq, 2 * LANES), BF16),
                        pltpu.VMEM((n_blk + 1, 2 * tq, MOBA_BLOCK), F32),
                        pltpu.VMEM((2 * tq, LANES), F32),
                        pltpu.VMEM((2 * tq, LANES), F32),
                        pltpu.VMEM((2 * tq, LANES), F32)],
        compiler_params=_cparams(("parallel", "parallel", "arbitrary")),
        name="moba_prompt",
    )(rel_bias, qa, pen, ka, va, onehot, bias)


def _ssd_body(xbc_ref, dt_ref, cprev_ref, h0_ref, cw_ref, cb_ref, dtb_ref, alog_ref, dsk_ref,
              e_ref, et_ref, eye_ref, tril_ref, y_ref, h_ref, xf_ref, dtf_ref, *, q):
    c = pl.program_id(1)
    t_in = xbc_ref.shape[0]
    t_out = y_ref.shape[0]
    hp2 = 2 * SSM_P
    gw = (SSM_HEADS // SSM_GROUPS) * SSM_P

    @pl.when(c == 0)
    def _():
        xf_ref[0:SUBLANES, :] = cprev_ref[...]
        h_ref[...] = h0_ref[...]

    if t_in < q:
        xf_ref[SUBLANES:SUBLANES + q, :] = jnp.zeros((q, CONV_DIM), F32)
        dtf_ref[...] = jnp.zeros_like(dtf_ref)
    xf_ref[SUBLANES:SUBLANES + t_in, :] = xbc_ref[...]
    dtf_ref[0:t_in, :] = dt_ref[...]
    base = SUBLANES - (CONV_W - 1)
    conv = cb_ref[...]
    for j in range(CONV_W):
        conv = conv + xf_ref[base + j:base + j + q, :] * cw_ref[j:j + 1, :]
    xf_ref[0:SUBLANES, :] = xf_ref[q:q + SUBLANES, :]
    act = conv * _sigmoid(conv)
    xs = act[:, :SSM_INNER]

    dt = _softplus(dtf_ref[...] + dtb_ref[...])
    if t_in < q:
        trow = lax.broadcasted_iota(jnp.int32, dt.shape, 0)
        dt = jnp.where(trow < t_in, dt, 0.0)
    a = dt * (-jnp.exp(alog_ref[...]))
    a_hi, a_lo = _split2(a)
    tril = tril_ref[...]
    acs = _dot(tril, a_hi) + _dot(tril, a_lo)
    acs_t = _dot2_nt(eye_ref[...], acs)
    e = e_ref[...]
    xr = xs * _dot2(dt, e)
    eacs_e = _dot2(jnp.exp(acs), e)
    xd = xr * _dot2(jnp.exp(acs[q - 1:q, :] - acs), e)
    xd_t = xd.T
    cd = jnp.broadcast_to(jnp.exp(acs_t[:, q - 1:q]), (LANES, LANES))
    cd_hi, cd_lo = _split2(cd)
    et = et_ref[...]
    f = _dot(et, cd_hi) + _dot(et, cd_lo)
    dsk_e = _dot2(jnp.broadcast_to(dsk_ref[...], (SUBLANES, LANES)), e)[0:1, :]

    row = lax.broadcasted_iota(jnp.int32, (q, q), 0)
    col = lax.broadcasted_iota(jnp.int32, (q, q), 1)
    causal = col <= row
    lane = lax.broadcasted_iota(jnp.int32, (q, hp2), 1)
    hpg = SSM_HEADS // SSM_GROUPS
    for g in range(SSM_GROUPS):
        bg = act[:, SSM_INNER + g * SSM_N:SSM_INNER + (g + 1) * SSM_N].astype(BF16)
        cg = act[:, SSM_INNER + (SSM_GROUPS + g) * SSM_N:
                 SSM_INNER + (SSM_GROUPS + g + 1) * SSM_N].astype(BF16)
        cbm = _dot_nt(cg, bg)
        hg = h_ref[g * hpg:(g + 1) * hpg].reshape(gw, SSM_N)
        y_off = _dot_nt(cg, hg.astype(BF16)) * eacs_e[:, g * gw:(g + 1) * gw]
        for pr in range(hpg // 2):
            yd = []
            xr_pair = xr[:, g * gw + pr * hp2:g * gw + (pr + 1) * hp2].astype(BF16)
            for hh in range(2):
                h = g * hpg + pr * 2 + hh
                seg = acs[:, h:h + 1] - acs_t[h:h + 1, :]
                m = (cbm * jnp.where(causal, jnp.exp(seg), 0.0)).astype(BF16)
                yd.append(_dot(m, xr_pair))
            y_pair = (jnp.where(lane < SSM_P, yd[0], yd[1]) + y_off[:, pr * hp2:(pr + 1) * hp2]
                      + xs[:, g * gw + pr * hp2:g * gw + (pr + 1) * hp2]
                      * dsk_e[:, g * gw + pr * hp2:g * gw + (pr + 1) * hp2])
            y_ref[:, g * gw + pr * hp2:g * gw + (pr + 1) * hp2] = y_pair[:t_out]
        st = _dot(xd_t[g * gw:(g + 1) * gw, :].astype(BF16), bg)
        h_ref[g * hpg:(g + 1) * hpg] = (hg * f[g * gw:(g + 1) * gw, :] + st).reshape(hpg, SSM_P, SSM_N)


def _ssd_consts(q):
    hidx = np.arange(SSM_INNER) // SSM_P
    e = (np.arange(LANES)[:, None] == hidx[None, :]).astype(np.float32)
    return (jnp.asarray(e, BF16), jnp.asarray(e.T, BF16),
            jnp.asarray(np.eye(LANES, dtype=np.float32), BF16),
            jnp.asarray(np.tril(np.ones((q, q), np.float32)), BF16))


def ssd(xa, cbx, dta, cprev, h0, state_layer, cw, cb, dtb, alog, dsk, layer, bsz, n_c, q=SSD_CHUNK):
    e, et, eye, tril = _ssd_consts(q)
    xblk = CONV_DIM // LANES
    t_in = xa.shape[1]
    t_out = q if t_in == q else SUBLANES
    const = lambda shape: pl.BlockSpec(shape, lambda b, c: (0,) * len(shape))
    par = lambda shape: pl.BlockSpec((None,) + shape, lambda b, c: (layer, 0, 0))
    return pl.pallas_call(
        functools.partial(_ssd_body, q=q),
        grid=(bsz, n_c),
        in_specs=[
            pl.BlockSpec((None, t_in, CONV_DIM), lambda b, c: (b * n_c + c, 0, cbx // xblk)),
            pl.BlockSpec((None, t_in, LANES), lambda b, c: (b * n_c + c, 0, 0)),
            pl.BlockSpec((None, None, SUBLANES, CONV_DIM), lambda b, c: (state_layer, b, 0, 0)),
            pl.BlockSpec((None, None, SSM_HEADS, SSM_P, SSM_N), lambda b, c: (state_layer, b, 0, 0, 0)),
            par((CONV_W, CONV_DIM)), par((1, CONV_DIM)),
            par((1, LANES)), par((1, LANES)), par((1, LANES)),
            const((LANES, SSM_INNER)), const((SSM_INNER, LANES)), const((LANES, LANES)),
            const((q, q)),
        ],
        out_specs=[
            pl.BlockSpec((None, t_out, SSM_INNER), lambda b, c: (b * n_c + c, 0, 0)),
            pl.BlockSpec((None, SSM_HEADS, SSM_P, SSM_N), lambda b, c: (b, 0, 0, 0)),
        ],
        out_shape=[jax.ShapeDtypeStruct((bsz * n_c, t_out, SSM_INNER), F32),
                   jax.ShapeDtypeStruct((bsz, SSM_HEADS, SSM_P, SSM_N), F32)],
        scratch_shapes=[pltpu.VMEM((q + SUBLANES, CONV_DIM), F32), pltpu.VMEM((q, LANES), F32)],
        compiler_params=_cparams(("parallel", "arbitrary")),
        name="ssd",
    )(xa, dta, cprev, h0, cw, cb, dtb, alog, dsk, e, et, eye, tril)


def _merge_body(x_ref, osb_ref, omb_ref, y_ref, z_ref, g0_ref, g1_ref, g2_ref, gs_ref,
                wsb_ref, wmb_ref, wss_ref, wo_ref, o_ref):
    z = z_ref[...]
    y = _rms(y_ref[...] * (z * _sigmoid(z)), gs_ref[...]).astype(BF16)
    merged = (_sigmoid(g0_ref[...]) * _dot(osb_ref[...], wsb_ref[...])
              + _sigmoid(g1_ref[...]) * _dot(omb_ref[...], wmb_ref[...])
              + _sigmoid(g2_ref[...]) * _dot(y, wss_ref[...]))
    o_ref[...] = x_ref[...] + _dot(merged.astype(BF16), wo_ref[...])


def merge(x, osb, omb, y, u, gs, wsb, wmb, wss, wo, layer, tm=512):
    n, d = x.shape
    db = d // LANES
    row = lambda cb, w: pl.BlockSpec((tm, w), lambda i: (i, cb))
    const = lambda a: pl.BlockSpec((None,) + a.shape[1:], lambda i: (layer, 0, 0))
    return pl.pallas_call(
        _merge_body,
        grid=(n // tm,),
        in_specs=[row(0, d), row(0, ATT_WIDTH), row(0, ATT_WIDTH), row(0, d),
                  row(CB_Z // db, d), row(CB_GATE // db, d), row(CB_GATE // db + 1, d),
                  row(CB_GATE // db + 2, d),
                  const(gs), const(wsb), const(wmb), const(wss), const(wo)],
        out_specs=row(0, d),
        out_shape=jax.ShapeDtypeStruct((n, d), F32),
        compiler_params=_cparams(("parallel",)),
        name="merge",
    )(x, osb, omb, y, u, u, u, u, gs, wsb, wmb, wss, wo)


def _router_body(x_ref, g_ref, wr_ref, xn_ref, cmb_ref):
    xn = _rms(x_ref[...], g_ref[...])
    xn_ref[...] = xn.astype(BF16)
    wr = wr_ref[...]
    xh, xl = _split2(xn)
    wh, wl = _split2(wr[:-SUBLANES, :])
    logit = _dot(xh, wh) + _dot(xh, wl) + _dot(xl, wh) + wr[-SUBLANES:-SUBLANES + 1, :]
    lane = lax.broadcasted_iota(jnp.int32, logit.shape, 1)
    neg = -jnp.inf
    is_g = jnp.logical_and(lane >= N_EXPERTS, lane < N_EXPERTS + N_GROUPS)
    gl = jnp.where(is_g, logit, neg)
    gmax = jnp.max(gl, axis=-1, keepdims=True)
    grp = jnp.min(jnp.where(gl == gmax, lane, LANES), axis=-1, keepdims=True) - N_EXPERTS
    p_grp = 1.0 / jnp.sum(jnp.exp(gl - gmax), axis=-1, keepdims=True)
    in_grp = jnp.logical_and(lane >= grp * EPG, lane < grp * EPG + EPG)
    el = jnp.where(in_grp, logit, neg)
    v1 = jnp.max(el, axis=-1, keepdims=True)
    i1 = jnp.min(jnp.where(el == v1, lane, LANES), axis=-1, keepdims=True)
    el2 = jnp.where(lane == i1, neg, el)
    v2 = jnp.max(el2, axis=-1, keepdims=True)
    i2 = jnp.min(jnp.where(el2 == v2, lane, LANES), axis=-1, keepdims=True)
    d = jnp.exp(v2 - v1)
    w1 = p_grp / (1.0 + d)
    w2 = p_grp * d / (1.0 + d)
    cmb_ref[...] = jnp.where(lane == i1, w1, 0.0) + jnp.where(lane == i2, w2, 0.0)


def router(x, g, wr, layer, tm=512):
    n, d = x.shape
    return pl.pallas_call(
        _router_body,
        grid=(n // tm,),
        in_specs=[pl.BlockSpec((tm, d), lambda i: (i, 0)),
                  pl.BlockSpec((None, 1, d), lambda i: (layer, 0, 0)),
                  pl.BlockSpec((None,) + wr.shape[1:], lambda i: (layer, 0, 0))],
        out_specs=[pl.BlockSpec((tm, d), lambda i: (i, 0)),
                   pl.BlockSpec((tm, LANES), lambda i: (i, 0))],
        out_shape=[jax.ShapeDtypeStruct((n, d), BF16),
                   jax.ShapeDtypeStruct((n, LANES), F32)],
        compiler_params=_cparams(("parallel",)),
        name="router",
    )(x, g, wr)


MOE_EXPERTS_PER_STEP = 4


def _moe_body(x_ref, xn_ref, cmb_ref, wgu_ref, wd_ref, gf_ref, o_ref, *, final_norm):
    g = pl.program_id(1)
    eps = MOE_EXPERTS_PER_STEP

    @pl.when(g == 0)
    def _():
        o_ref[...] = x_ref[...]

    cmb = cmb_ref[...]
    xn = xn_ref[...]
    lane = lax.broadcasted_iota(jnp.int32, cmb.shape, 1)
    hids = []
    for j in range(eps):
        c = jnp.sum(jnp.where(lane == g * eps + j, cmb, 0.0), axis=-1, keepdims=True)
        gu = _dot(xn, wgu_ref[j])
        gt = gu[:, :EXPERT_FF]
        hids.append((c * (gt * _sigmoid(gt) * gu[:, EXPERT_FF:])).astype(BF16))
    wd = wd_ref[...].reshape(eps * EXPERT_FF, wd_ref.shape[-1])
    o_ref[...] += _dot(jnp.concatenate(hids, axis=1), wd)

    if final_norm:
        @pl.when(g == pl.num_programs(1) - 1)
        def _():
            o_ref[...] = _rms(o_ref[...], gf_ref[...])


def moe(x, xn, cmb, wgu, wd, gf, layer, final_norm, tm=512):
    n, d = x.shape
    eps = MOE_EXPERTS_PER_STEP
    return pl.pallas_call(
        functools.partial(_moe_body, final_norm=final_norm),
        grid=(n // tm, N_EXPERTS // eps),
        in_specs=[pl.BlockSpec((tm, d), lambda i, e: (i, 0)),
                  pl.BlockSpec((tm, d), lambda i, e: (i, 0)),
                  pl.BlockSpec((tm, LANES), lambda i, e: (i, 0)),
                  pl.BlockSpec((None, eps, d, 2 * EXPERT_FF), lambda i, e: (layer, e, 0, 0)),
                  pl.BlockSpec((None, eps, EXPERT_FF, d), lambda i, e: (layer, e, 0, 0)),
                  pl.BlockSpec((1, d), lambda i, e: (0, 0))],
        out_specs=pl.BlockSpec((tm, d), lambda i, e: (i, 0)),
        out_shape=jax.ShapeDtypeStruct((n, d), F32),
        compiler_params=_cparams(("parallel", "arbitrary")),
        name="moe",
    )(x, xn, cmb, wgu, wd, gf)


T_PAD = SUBLANES
QROWS = N_HEADS * T_PAD


def _pad_rows(x_ref, buf):
    buf[0:T_PAD, :] = jnp.zeros((T_PAD, buf.shape[1]), buf.dtype)
    buf[0:x_ref.shape[0], :] = x_ref[...]
    return buf[0:T_PAD, :]


def _q_block_diag(q8):
    lane = lax.broadcasted_iota(jnp.int32, q8.shape, 1)
    return jnp.concatenate(
        [jnp.where(lane // HEAD_DIM == h, q8, 0.0) for h in range(N_HEADS)], axis=0)


def _out_from_acc(acc):
    lane = lax.broadcasted_iota(jnp.int32, (T_PAD, ATT_WIDTH), 1)
    out = jnp.zeros((T_PAD, ATT_WIDTH), F32)
    for h in range(N_HEADS):
        out = jnp.where(lane // HEAD_DIM == h, acc[h * T_PAD:(h + 1) * T_PAD, :], out)
    return out


N_AHEAD = 2


def _sb_sample_body(pt_ref, q_ref, kn_ref, vn_ref, ck_ref, cv_ref, tri_ref, o_ref,
                    kbuf, vbuf, acc_ref, car_ref, flag, kpg, vpg, kx, vx, sem_pg, sem_x,
                    *, n_pages, t_new, layer):
    b = pl.program_id(0)
    n_seq = pl.num_programs(0)
    slot = b % 2
    scale = HEAD_DIM ** -0.5
    qbd = (_q_block_diag(_pad_rows(q_ref, kbuf)) * scale).astype(BF16)
    tri = tri_ref[...]

    def ahead_copies(seq, sl):
        cps = []
        for j in range(N_AHEAD):
            pg = pt_ref[seq, n_pages - 1 - j]
            cps.append(pltpu.make_async_copy(ck_ref.at[layer, pg], kpg.at[sl, j], sem_pg.at[sl, j, 0]))
            cps.append(pltpu.make_async_copy(cv_ref.at[layer, pg], vpg.at[sl, j], sem_pg.at[sl, j, 1]))
        return cps

    def demand_copies(p):
        pg = pt_ref[b, p]
        return [pltpu.make_async_copy(ck_ref.at[layer, pg], kx, sem_x.at[0]),
                pltpu.make_async_copy(cv_ref.at[layer, pg], vx, sem_x.at[1])]

    @pl.when(b == 0)
    def _():
        for cp in ahead_copies(0, 0):
            cp.start()

    @pl.when(b + 1 < n_seq)
    def _():
        for cp in ahead_copies(b + 1, 1 - slot):
            cp.start()

    def page(z, weigh, mask):
        lk = -_softplus(z)
        if mask is not None:
            lk = jnp.where(mask, lk, 0.0)
        sums = _suffix_sums(lk, tri)
        w = jnp.exp(z + sums[:, :PAGE] + car_ref[...])
        if mask is not None:
            w = jnp.where(mask, w, 0.0)
        acc_ref[...] += weigh(w.astype(BF16))
        car = car_ref[...] + sums[:, PAGE:]
        car_ref[...] = car
        flag[0] = (jnp.max(car) >= EXP_ZERO).astype(jnp.int32)

    kbuf[...] = jnp.zeros_like(kbuf)
    vbuf[...] = jnp.zeros_like(vbuf)
    kbuf[0:t_new, :] = kn_ref[...]
    vbuf[0:t_new, :] = vn_ref[...]
    acc_ref[...] = jnp.zeros_like(acc_ref)
    car_ref[...] = jnp.zeros_like(car_ref)
    r = lax.broadcasted_iota(jnp.int32, (QROWS, PAGE), 0)
    c = lax.broadcasted_iota(jnp.int32, (QROWS, PAGE), 1)
    past = jnp.logical_and(c < r % T_PAD, c < t_new)
    page(_dot_nt(qbd, kbuf[...].astype(BF16)), lambda w: _dot(w, vbuf[...].astype(BF16)), past)

    def cache_page(k_page, v_page):
        vt = v_page.reshape(ATT_WIDTH, PAGE).astype(BF16)
        page(_dot(qbd, k_page.reshape(ATT_WIDTH, PAGE).astype(BF16)), lambda w: _dot_nt(w, vt), None)

    for cp in ahead_copies(b, slot):
        cp.wait()
    for j in range(N_AHEAD):
        @pl.when(flag[0] > 0)
        def _():
            cache_page(kpg[slot, j], vpg[slot, j])

    for p in reversed(range(n_pages - N_AHEAD)):
        @pl.when(flag[0] > 0)
        def _():
            cps = demand_copies(p)
            for cp in cps:
                cp.start()
            for cp in cps:
                cp.wait()
            cache_page(kx[...], vx[...])

    o_ref[...] = _out_from_acc(acc_ref[...]).astype(o_ref.dtype)


def _page_specs(layer, n_pages):
    return [pl.BlockSpec((None, None, N_HEADS, HEAD_DIM, PAGE),
                         functools.partial(lambda b, pt, p: (layer, pt[b, p], 0, 0, 0), p=p))
            for p in range(n_pages)]


def sb_sample(page_table, us8, cache_k, cache_v, layer, t_new):
    n_seq, n_pages = page_table.shape
    assert n_pages >= N_AHEAD
    tri = _tri_ge_ones(PAGE)
    tok = lambda cb: pl.BlockSpec((None, t_new, ATT_WIDTH), lambda b, pt: (b, 0, cb // 4))
    const = lambda a: pl.BlockSpec(a.shape, lambda b, pt: (0,) * a.ndim)
    page_shape = (N_HEADS, HEAD_DIM, PAGE)
    grid_spec = pltpu.PrefetchScalarGridSpec(
        num_scalar_prefetch=1,
        grid=(n_seq,),
        in_specs=[tok(CB_Q_SB), tok(CB_K_SB), tok(CB_V_SB),
                  pl.BlockSpec(memory_space=pl.ANY), pl.BlockSpec(memory_space=pl.ANY),
                  const(tri)],
        out_specs=pl.BlockSpec((None, T_PAD, ATT_WIDTH), lambda b, pt: (b, 0, 0)),
        scratch_shapes=[pltpu.VMEM((PAGE, ATT_WIDTH), F32), pltpu.VMEM((PAGE, ATT_WIDTH), F32),
                        pltpu.VMEM((QROWS, ATT_WIDTH), F32), pltpu.VMEM((QROWS, LANES), F32),
                        pltpu.SMEM((1,), jnp.int32),
                        pltpu.VMEM((2, N_AHEAD) + page_shape, F32),
                        pltpu.VMEM((2, N_AHEAD) + page_shape, F32),
                        pltpu.VMEM(page_shape, F32), pltpu.VMEM(page_shape, F32),
                        pltpu.SemaphoreType.DMA((2, N_AHEAD, 2)), pltpu.SemaphoreType.DMA((2,))],
    )
    return pl.pallas_call(
        functools.partial(_sb_sample_body, n_pages=n_pages, t_new=t_new, layer=layer),
        grid_spec=grid_spec,
        out_shape=jax.ShapeDtypeStruct((n_seq, T_PAD, ATT_WIDTH), BF16),
        compiler_params=_cparams(("arbitrary",)),
        name="sb_sample",
    )(page_table, us8, us8, us8, cache_k, cache_v, tri)


def _moba_sample_body(pt_ref, q_ref, kn_ref, vn_ref, *rest, n_pages, t_new):
    k_refs = rest[:n_pages]
    v_refs = rest[n_pages:2 * n_pages]
    bias_ref, o_ref, kbuf, vbuf, s_ref = rest[2 * n_pages:]
    ppb = MOBA_BLOCK // PAGE
    n_blk = n_pages // ppb
    scale = HEAD_DIM ** -0.5
    qf = _q_block_diag(_pad_rows(q_ref, kbuf))
    qs = (qf * scale).astype(BF16)
    neg = -jnp.inf
    lane = lax.broadcasted_iota(jnp.int32, (QROWS, LANES), 1)
    lane_w = lax.broadcasted_iota(jnp.int32, (ATT_WIDTH, LANES), 1)

    ksum = jnp.zeros((ATT_WIDTH, LANES), F32)
    for n in range(n_blk):
        blk = k_refs[ppb * n][...].reshape(ATT_WIDTH, PAGE)
        for j in range(1, ppb):
            blk = blk + k_refs[ppb * n + j][...].reshape(ATT_WIDTH, PAGE)
        ksum = jnp.where(lane_w == n, jnp.sum(blk, axis=-1, keepdims=True), ksum)
    kmean = ksum * (1.0 / MOBA_BLOCK)
    qh, ql = _split2(qf)
    mh, ml = _split2(kmean)
    route = _dot(qh, mh) + _dot(qh, ml) + _dot(ql, mh)
    sel = _route_topk(route, lane, n_blk, n_blk)

    mrun = jnp.full((QROWS, PAGE), neg, F32)
    for p in range(n_pages):
        s = _dot(qs, k_refs[p][...].reshape(ATT_WIDTH, PAGE).astype(BF16)) + bias_ref[p]
        s = jnp.where(sel[:, p // ppb:p // ppb + 1] > 0.5, s, neg)
        s_ref[p] = s
        mrun = jnp.maximum(mrun, s)
    kbuf[...] = jnp.zeros_like(kbuf)
    vbuf[...] = jnp.zeros_like(vbuf)
    kbuf[0:t_new, :] = kn_ref[...]
    vbuf[0:t_new, :] = vn_ref[...]
    r = lax.broadcasted_iota(jnp.int32, (QROWS, PAGE), 0)
    c = lax.broadcasted_iota(jnp.int32, (QROWS, PAGE), 1)
    causal = jnp.logical_and(c <= r % T_PAD, c < T_PAD)
    s = _dot_nt(qs, kbuf[...].astype(BF16)) + bias_ref[n_pages]
    s = jnp.where(causal, s, neg)
    s_ref[n_pages] = s
    m = jnp.max(jnp.maximum(mrun, s), axis=-1, keepdims=True)

    lsum = jnp.zeros((QROWS, PAGE), F32)
    acc = jnp.zeros((QROWS, ATT_WIDTH), F32)
    for p in range(n_pages + 1):
        pe = jnp.exp(s_ref[p] - m)
        lsum = lsum + pe
        pe = pe.astype(BF16)
        if p < n_pages:
            acc = acc + _dot_nt(pe, v_refs[p][...].reshape(ATT_WIDTH, PAGE).astype(BF16))
        else:
            acc = acc + _dot(pe, vbuf[...].astype(BF16))
    o_ref[...] = _out_from_acc(acc / jnp.sum(lsum, axis=-1, keepdims=True)).astype(o_ref.dtype)


def moba_sample_bias_idx(n_pages):
    t = np.arange(T_PAD)[:, None]
    s = np.arange(PAGE)[None, :]
    past = [_rel_bucket_np((n_pages - p) * PAGE + t - s) for p in range(n_pages)]
    return np.stack(past + [_rel_bucket_np(t - s)])


def moba_sample(page_table, us8, cache_k, cache_v, bias, layer, t_new):
    n_seq, n_pages = page_table.shape
    tok = lambda cb: pl.BlockSpec((None, t_new, ATT_WIDTH), lambda b, pt: (b, 0, cb // 4))
    const = lambda a: pl.BlockSpec(a.shape, lambda b, pt: (0,) * a.ndim)
    grid_spec = pltpu.PrefetchScalarGridSpec(
        num_scalar_prefetch=1,
        grid=(n_seq,),
        in_specs=[tok(CB_Q_MB), tok(CB_K_MB), tok(CB_V_MB)]
        + _page_specs(layer, n_pages) + _page_specs(layer, n_pages)
        + [const(bias)],
        out_specs=pl.BlockSpec((None, T_PAD, ATT_WIDTH), lambda b, pt: (b, 0, 0)),
        scratch_shapes=[pltpu.VMEM((PAGE, ATT_WIDTH), F32), pltpu.VMEM((PAGE, ATT_WIDTH), F32),
                        pltpu.VMEM((n_pages + 1, QROWS, PAGE), F32)],
    )
    return pl.pallas_call(
        functools.partial(_moba_sample_body, n_pages=n_pages, t_new=t_new),
        grid_spec=grid_spec,
        out_shape=jax.ShapeDtypeStruct((n_seq, T_PAD, ATT_WIDTH), BF16),
        compiler_params=_cparams(("arbitrary",)),
        name="moba_sample",
    )(page_table, us8, us8, us8, *([cache_k] * n_pages), *([cache_v] * n_pages), bias)


def kernel(x_prompt, x_sample, cache_k_sb, cache_v_sb, cache_k_moba, cache_v_moba, page_table, state_ssm, state_conv, rel_bias, g_mix, w_in, conv_w, conv_b, dt_bias, a_log, d_skip, g_ssm, w_branch_sb, w_branch_moba, w_branch_ssm, w_out, g_ffn, w_router_group, b_router_group, w_router_expert, b_router_expert, w_expert_gate, w_expert_up, w_expert_down, g_final):
    bsz, seq, d = x_prompt.shape
    n_seq, t_new, _ = x_sample.shape
    depth = w_in.shape[0]
    n_pages = page_table.shape[1]
    n_p = bsz * seq
    n_s = n_seq * t_new
    assert d == D_MODEL and seq % MOBA_BLOCK == 0 and t_new <= T_PAD
    assert (n_pages * PAGE) % MOBA_BLOCK == 0 and cache_k_sb.shape[2] == PAGE

    x_p = x_prompt.reshape(n_p, d)
    x_s = x_sample.reshape(n_s, d)
    tm_p = 1024 if n_p % 1024 == 0 else 512
    tm_s = min(n_s, 512)
    n_c = seq // SSD_CHUNK
    pages = [c.transpose(0, 1, 3, 4, 2) for c in (cache_k_sb, cache_v_sb, cache_k_moba, cache_v_moba)]
    tq_mb = MOBA_BLOCK
    bias_p = bias_tiles(rel_bias, jnp.asarray(moba_prompt_bias_idx(tq_mb)))
    bias_s = bias_tiles(rel_bias, jnp.asarray(moba_sample_bias_idx(n_pages))).reshape(
        n_pages + 1, QROWS, PAGE)
    zero_conv = jnp.zeros((1, bsz, SUBLANES, CONV_DIM), F32)
    zero_state = jnp.zeros((1, bsz, SSM_HEADS, SSM_P, SSM_N), F32)
    n_main = CB_XBC * LANES + CONV_DIM

    def col(a, cb, w):
        return a[..., cb * LANES:cb * LANES + w]

    def kv(a, cb):
        return col(a, cb, ATT_WIDTH).reshape(a.shape[0], a.shape[1], N_HEADS, HEAD_DIM)

    def lane_pad(v):
        return jnp.pad(v, ((0, 0), (0, LANES - v.shape[1])))[:, None, :]

    w_main = jnp.concatenate([w_in[:, :, :n_main], w_in[:, :, n_main + SSM_HEADS:]], axis=2).astype(BF16)
    w_dt = jnp.pad(w_in[:, :, n_main:n_main + SSM_HEADS],
                   ((0, 0), (0, 0), (0, LANES - SSM_HEADS))).astype(BF16)
    gm, gs, gf = g_mix[:, None, :], g_ssm[:, None, :], g_ffn[:, None, :]
    ssm_par = (conv_w, conv_b[:, None, :], lane_pad(dt_bias), lane_pad(a_log), lane_pad(d_skip))
    lane_fill = LANES - N_EXPERTS - N_GROUPS
    wr = jnp.concatenate([
        jnp.pad(jnp.concatenate([w_router_expert, w_router_group], axis=2),
                ((0, 0), (0, 0), (0, lane_fill))),
        jnp.pad(jnp.concatenate([b_router_expert, b_router_group], axis=1)[:, None, :],
                ((0, 0), (0, SUBLANES - 1), (0, lane_fill)))], axis=1)
    wgu = jnp.concatenate([w_expert_gate, w_expert_up], axis=-1).astype(BF16)
    wd = w_expert_down.astype(BF16)
    w_br = (w_branch_sb.astype(BF16), w_branch_moba.astype(BF16),
            w_branch_ssm.astype(BF16), w_out.astype(BF16))
    cprev_s = jnp.pad(state_conv, ((0, 0), (0, 0), (SUBLANES - (CONV_W - 1), 0), (0, 0)))
    g_fin = g_final.reshape(1, d)

    new_p, new_s = [], []
    for l in range(depth):
        def tail(x, osb, omb, y, u, tm):
            x1 = merge(x, osb, omb, y, u, gs, *w_br, l, tm=min(tm, 512))
            xn, cmb = router(x1, gf, wr, l, tm=min(tm, 512))
            return moe(x1, xn, cmb, wgu, wd, g_fin, l, final_norm=(l == depth - 1), tm=tm)

        u_p, dt_p, qkv_p = in_proj(x_p, gm, w_main, w_dt, l, tm=tm_p)
        osb_p = sb_prompt(qkv_p, qkv_p, qkv_p, CB_Q_SB, CB_K_SB, CB_V_SB, bsz, seq)
        km = kmean_prompt(u_p, CB_K_MB, bsz, seq)
        pen = moba_route(u_p, km, CB_Q_MB, bsz, seq, tq_mb)
        omb_p = moba_prompt(qkv_p, qkv_p, qkv_p, pen, rel_bias, bias_p, CB_Q_MB, CB_K_MB, CB_V_MB,
                            bsz, seq, tq=tq_mb)
        y_p, ssm_p = ssd(u_p.reshape(bsz * n_c, SSD_CHUNK, U_COLS), CB_XBC,
                         dt_p.reshape(bsz * n_c, SSD_CHUNK, LANES), zero_conv, zero_state, 0,
                         *ssm_par, l, bsz=bsz, n_c=n_c)
        x_p = tail(x_p, osb_p, omb_p, y_p.reshape(n_p, SSM_INNER), u_p, tm_p)

        u_s, dt_s, _ = in_proj(x_s, gm, w_main, w_dt, l, tm=tm_s)
        u_s3 = u_s.reshape(n_seq, t_new, U_COLS)
        osb_s = sb_sample(page_table, u_s3, pages[0], pages[1], l, t_new)[:, :t_new]
        omb_s = moba_sample(page_table, u_s3, pages[2], pages[3], bias_s, l, t_new)[:, :t_new]
        y_s, ssm_s = ssd(u_s3, CB_XBC, dt_s.reshape(n_seq, t_new, LANES), cprev_s, state_ssm, l,
                         *ssm_par, l, bsz=n_seq, n_c=1, q=T_PAD)
        x_s = tail(x_s, osb_s.reshape(n_s, ATT_WIDTH), omb_s.reshape(n_s, ATT_WIDTH),
                   y_s[:, :t_new].reshape(n_s, SSM_INNER), u_s, tm_s)

        u_p3 = u_p.reshape(bsz, seq, U_COLS)
        new_p.append((kv(u_p3, CB_K_SB), kv(u_p3, CB_V_SB), kv(u_p3, CB_K_MB), kv(u_p3, CB_V_MB),
                      ssm_p, col(u_p3, CB_XBC, CONV_DIM)[:, seq - (CONV_W - 1):]))
        conv_s = jnp.concatenate([state_conv[l], col(u_s3, CB_XBC, CONV_DIM)], axis=1)[:, t_new:]
        new_s.append((kv(u_s3, CB_K_SB), kv(u_s3, CB_V_SB), kv(u_s3, CB_K_MB), kv(u_s3, CB_V_MB),
                      ssm_s, conv_s))

    stack = lambda states: tuple(jnp.stack([s[i] for s in states]) for i in range(6))
    return ((x_p.reshape(bsz, seq, d), x_s.reshape(n_seq, t_new, d)) + stack(new_p) + stack(new_s))
```

```python
import functools
import math

import numpy as np
import jax
import jax.numpy as jnp
from jax import lax
from jax.experimental import pallas as pl
from jax.experimental.pallas import tpu as pltpu

F32 = jnp.float32
BF16 = jnp.bfloat16

LANES = 128
SUBLANES = 8
VMEM_LIMIT = 56 * 1024 * 1024

D_MODEL = 1024
HEAD_DIM = 64
N_HEADS = 8
ATT_WIDTH = N_HEADS * HEAD_DIM
MOBA_BLOCK = 256
MOBA_TOPK = 3
PAGE = 128
SSM_HEADS = 16
SSM_P = 64
SSM_GROUPS = 4
SSM_N = 128
SSM_INNER = SSM_HEADS * SSM_P
CONV_W = 4
CONV_DIM = SSM_INNER + 2 * SSM_GROUPS * SSM_N
SSD_CHUNK = 128
REL_BUCKETS = 32
REL_MAX_DIST = 128
N_GROUPS = 4
EPG = 8
N_EXPERTS = N_GROUPS * EPG
EXPERT_FF = 256
EPS = 1e-6
EXP_ZERO = -88.0
NEG_BIG = -1e30

CB_Q_SB, CB_K_SB, CB_V_SB = 0, 4, 8
CB_Q_MB, CB_K_MB, CB_V_MB = 12, 16, 20
CB_Z = 24
CB_XBC = 32
CB_GATE = 48
U_COLS = 72 * LANES


def _cparams(sem):
    return pltpu.CompilerParams(dimension_semantics=sem, vmem_limit_bytes=VMEM_LIMIT)


def _split2(a):
    hi = a.astype(BF16)
    lo = (a - hi.astype(F32)).astype(BF16)
    return hi, lo


def _dot(a, b):
    return jnp.dot(a, b, preferred_element_type=F32)


def _dot_nt(a, b):
    return lax.dot_general(a, b, (((1,), (1,)), ((), ())), preferred_element_type=F32)


def _dot2(a, b_exact):
    hi, lo = _split2(a)
    return _dot(hi, b_exact) + _dot(lo, b_exact)


def _dot2_nt(a_exact, b):
    hi, lo = _split2(b)
    return _dot_nt(a_exact, hi) + _dot_nt(a_exact, lo)


def _dot3_nt(a, b):
    ah, al = _split2(a)
    bh, bl = _split2(b)
    return _dot_nt(ah, bh) + _dot_nt(ah, bl) + _dot_nt(al, bh)


def _sigmoid(x):
    return 1.0 / (1.0 + jnp.exp(-x))


def _softplus(x):
    return jnp.maximum(x, 0.0) + jnp.log(1.0 + jnp.exp(-jnp.abs(x)))


def _rms(x, g):
    ms = jnp.mean(x * x, axis=-1, keepdims=True)
    return x * lax.rsqrt(ms + EPS) * g


def _in_proj_body(x_ref, g_ref, w_ref, wdt_ref, u_ref, dt_ref, qkv_ref, xn_ref, *, n_qkv):
    j = pl.program_id(1)

    @pl.when(j == 0)
    def _():
        xn = _rms(x_ref[...], g_ref[...]).astype(BF16)
        xn_ref[...] = xn
        dt_ref[...] = _dot(xn, wdt_ref[...])

    u = _dot(xn_ref[...], w_ref[...])
    u_ref[...] = u

    @pl.when(j < n_qkv)
    def _():
        qkv_ref[...] = u.astype(BF16)


def in_proj(x, g, w, wdt, layer, tm=512, tn=1024):
    n, d = x.shape
    cols = w.shape[2]
    n_qkv = (CB_Z * LANES) // tn
    return pl.pallas_call(
        functools.partial(_in_proj_body, n_qkv=n_qkv),
        grid=(n // tm, cols // tn),
        in_specs=[
            pl.BlockSpec((tm, d), lambda i, j: (i, 0)),
            pl.BlockSpec((None, 1, d), lambda i, j: (layer, 0, 0)),
            pl.BlockSpec((None, d, tn), lambda i, j: (layer, 0, j)),
            pl.BlockSpec((None, d, LANES), lambda i, j: (layer, 0, 0)),
        ],
        out_specs=[
            pl.BlockSpec((tm, tn), lambda i, j: (i, j)),
            pl.BlockSpec((tm, LANES), lambda i, j: (i, 0)),
            pl.BlockSpec((tm, tn), lambda i, j: (i, jnp.minimum(j, n_qkv - 1))),
        ],
        out_shape=[jax.ShapeDtypeStruct((n, cols), F32),
                   jax.ShapeDtypeStruct((n, LANES), F32),
                   jax.ShapeDtypeStruct((n, n_qkv * tn), BF16)],
        scratch_shapes=[pltpu.VMEM((tm, d), BF16)],
        compiler_params=_cparams(("parallel", "arbitrary")),
        name="in_proj",
    )(x, g, w, wdt)


def _suffix_sums(lk, tri2):
    hi, lo = _split2(lk)
    return _dot(jnp.concatenate([hi, lo], axis=1), tri2)


def _sb_tile(qh, k, v, carry, tri2, mask):
    tk = k.shape[0]
    z = _dot_nt(qh, k)
    lk = -_softplus(z)
    if mask is not None:
        lk = jnp.where(mask, lk, 0.0)
    sums = _suffix_sums(lk, tri2)
    w = jnp.exp(z + sums[:, :tk] + carry)
    if mask is not None:
        w = jnp.where(mask, w, 0.0)
    return _dot(w.astype(BF16), v), carry + sums[:, tk:]


def _stack_heads(q, scale):
    lane = lax.broadcasted_iota(jnp.int32, q.shape, 1)
    q = q * scale
    return jnp.concatenate([jnp.where(lane < HEAD_DIM, q, 0.0),
                            jnp.where(lane >= HEAD_DIM, q, 0.0)], axis=0).astype(BF16)


def _unstack_heads(o, tq):
    lane = lax.broadcasted_iota(jnp.int32, (tq, LANES), 1)
    return jnp.where(lane < HEAD_DIM, o[:tq], o[tq:])


def _sb_prompt_body(q_ref, k_ref, v_ref, tri_ref, o_ref, acc_ref, car_ref, *, tk):
    i = pl.program_id(2)
    scale = HEAD_DIM ** -0.5
    q = q_ref[...].astype(F32)
    qa = jnp.concatenate([_stack_heads(q[:tk], scale), _stack_heads(q[tk:], scale)], axis=0)
    hb = 2 * tk
    tri2 = tri_ref[...]
    row = lax.broadcasted_iota(jnp.int32, (2 * hb, tk), 0)
    col = lax.broadcasted_iota(jnp.int32, (2 * hb, tk), 1)
    tri_mask = col < row % tk

    def tile(j, carry, mask):
        kj = pl.multiple_of(j * tk, tk)
        return _sb_tile(qa, k_ref[pl.ds(kj, tk), :], v_ref[pl.ds(kj, tk), :], carry, tri2, mask)

    tiles = [(2 * i + 1, qa[hb:], tri_mask[hb:]),
             (2 * i, qa, jnp.logical_or(row >= hb, tri_mask)),
             (jnp.maximum(2 * i - 1, 0), qa, jnp.broadcast_to(i > 0, (2 * hb, tk)))]
    starts = [pl.multiple_of(j * tk, tk) for j, _, _ in tiles]
    zs = [_dot_nt(qr, k_ref[pl.ds(s, tk), :]) for s, (_, qr, _) in zip(starts, tiles)]
    lks = [jnp.where(mask, -_softplus(z), 0.0) for z, (_, _, mask) in zip(zs, tiles)]
    sums = [_suffix_sums(lk, tri2) for lk in lks]
    ws = []
    c = jnp.zeros((hb, tk), F32)
    for t, (z, sm, (_, _, mask)) in enumerate(zip(zs, sums, tiles)):
        ws.append(jnp.where(mask, jnp.exp(z + sm[:, :tk] + c), 0.0).astype(BF16))
        c = c + sm[:, tk:]
        if t == 0:
            c = jnp.concatenate([jnp.zeros((hb, tk), F32), c], axis=0)
    os_ = [_dot(w, v_ref[pl.ds(s, tk), :]) for w, s in zip(ws, starts)]
    acc_ref[...] = os_[1] + os_[2]
    acc_ref[hb:, :] += os_[0]
    car_ref[...] = c

    def cond(j):
        return jnp.logical_and(j >= 0, jnp.max(car_ref[...]) >= EXP_ZERO)

    def body(j):
        o, c = tile(j, car_ref[...], None)
        acc_ref[...] += o
        car_ref[...] = c
        return j - 1

    lax.while_loop(cond, body, 2 * i - 2)
    o_ref[0:tk, :] = _unstack_heads(acc_ref[0:hb, :], tk).astype(o_ref.dtype)
    o_ref[tk:, :] = _unstack_heads(acc_ref[hb:, :], tk).astype(o_ref.dtype)


def _tri_ge_ones(n):
    t2 = np.concatenate([np.tril(np.ones((n, n), np.float32)), np.ones((n, n), np.float32)], axis=1)
    return jnp.asarray(np.concatenate([t2, t2], axis=0), BF16)


def sb_prompt(qa, ka, va, cbq, cbk, cbv, bsz, t_len, tk=128):
    tq = 2 * tk
    n_q = t_len // tq
    n_hp = N_HEADS // 2
    return pl.pallas_call(
        functools.partial(_sb_prompt_body, tk=tk),
        grid=(bsz, n_hp, n_q),
        in_specs=[
            pl.BlockSpec((tq, LANES), lambda b, p, i: (b * n_q + i, cbq + p)),
            pl.BlockSpec((t_len, LANES), lambda b, p, i: (b, cbk + p)),
            pl.BlockSpec((t_len, LANES), lambda b, p, i: (b, cbv + p)),
            pl.BlockSpec((2 * tk, 2 * tk), lambda b, p, i: (0, 0)),
        ],
        out_specs=pl.BlockSpec((tq, LANES), lambda b, p, i: (b * n_q + i, p)),
        out_shape=jax.ShapeDtypeStruct((bsz * t_len, ATT_WIDTH), BF16),
        scratch_shapes=[pltpu.VMEM((2 * tq, LANES), F32), pltpu.VMEM((2 * tq, tk), F32)],
        compiler_params=_cparams(("parallel", "parallel", "arbitrary")),
        name="sb_prompt",
    )(qa, ka, va, _tri_ge_ones(tk))


def _rel_bucket_np(dist):
    exact = REL_BUCKETS // 2
    d = np.maximum(dist, 0)
    df = np.maximum(d, 1).astype(np.float32)
    large = exact + (np.log(df / np.float32(exact)) / np.float32(math.log(REL_MAX_DIST / exact))
                     * np.float32(REL_BUCKETS - exact)).astype(np.int32)
    return np.where(d < exact, d, np.minimum(large, REL_BUCKETS - 1)).astype(np.int32)


def _bias_body(rb_ref, idx_ref, o_ref):
    idx = idx_ref[...]
    for h in range(N_HEADS):
        acc = jnp.zeros(idx.shape, F32)
        for b in range(REL_BUCKETS):
            acc = jnp.where(idx == b, rb_ref[b, h], acc)
        o_ref[h] = acc


def bias_tiles(rel_bias, idx):
    nv, r, c = idx.shape
    return pl.pallas_call(
        _bias_body,
        grid=(nv,),
        in_specs=[pl.BlockSpec(memory_space=pltpu.SMEM),
                  pl.BlockSpec((None, r, c), lambda v: (v, 0, 0))],
        out_specs=pl.BlockSpec((None, N_HEADS, r, c), lambda v: (v, 0, 0, 0)),
        out_shape=jax.ShapeDtypeStruct((nv, N_HEADS, r, c), F32),
        compiler_params=_cparams(("parallel",)),
        name="bias_tiles",
    )(rel_bias, idx)


def _kmean_body(k_ref, o_ref):
    n = pl.program_id(1)

    @pl.when(n == 0)
    def _():
        o_ref[...] = jnp.zeros_like(o_ref)

    o_ref[pl.ds(n, 1), :] = jnp.mean(k_ref[...], axis=0, keepdims=True)


def kmean_prompt(ka, cbk, bsz, t_len):
    n_blk = t_len // MOBA_BLOCK
    return pl.pallas_call(
        _kmean_body,
        grid=(bsz, n_blk),
        in_specs=[pl.BlockSpec((MOBA_BLOCK, ATT_WIDTH), lambda b, n: (b * n_blk + n, cbk // 4))],
        out_specs=pl.BlockSpec((None, LANES, ATT_WIDTH), lambda b, n: (b, 0, 0)),
        out_shape=jax.ShapeDtypeStruct((bsz, LANES, ATT_WIDTH), F32),
        compiler_params=_cparams(("parallel", "arbitrary")),
        name="kmean_prompt",
    )(ka)


def _route_topk(route, lane, n_valid, n_ok):
    neg = -jnp.inf
    r = jnp.where(lane < n_valid, route, neg)
    sel = jnp.zeros(route.shape, F32)
    for j in range(MOBA_TOPK):
        m = jnp.max(r, axis=-1, keepdims=True)
        idx = jnp.min(jnp.where(r == m, lane, LANES), axis=-1, keepdims=True)
        hit = lane == idx
        sel = jnp.where(jnp.logical_and(hit, j < n_ok), 1.0, sel)
        r = jnp.where(hit, neg, r)
    return sel


def _block_penalty_t(route_t, own):
    blk = lax.broadcasted_iota(jnp.int32, route_t.shape, 0)
    neg = -jnp.inf
    r = jnp.where(blk < own, route_t, neg)
    keep = blk == own
    for j in range(MOBA_TOPK):
        m = jnp.max(r, axis=0, keepdims=True)
        idx = jnp.min(jnp.where(r == m, blk, route_t.shape[0]), axis=0, keepdims=True)
        hit = blk == idx
        keep = jnp.logical_or(keep, jnp.logical_and(hit, j < own))
        r = jnp.where(hit, neg, r)
    return jnp.where(keep, 0.0, NEG_BIG)


ROUTE_TILES = 8


def _moba_route_body(q_ref, eye_ref, km_ref, pen_ref, *, tq, n_blk):
    g = pl.program_id(2)
    per = MOBA_BLOCK // tq
    km = km_ref[0:n_blk, :]
    q = q_ref[...]
    lane = lax.broadcasted_iota(jnp.int32, (tq, LANES), 1)
    qf = jnp.concatenate(
        [jnp.where((lane < HEAD_DIM) if h == 0 else (lane >= HEAD_DIM), q[t * tq:(t + 1) * tq], 0.0)
         for t in range(ROUTE_TILES) for h in range(2)], axis=0)
    n_col = 2 * ROUTE_TILES * tq
    tile_of_col = lax.broadcasted_iota(jnp.int32, (1, n_col), 1) // (2 * tq)
    own = (g * ROUTE_TILES + tile_of_col) // per
    pen_t = _block_penalty_t(_dot3_nt(km, qf), own)
    pen_t = jnp.concatenate([pen_t, jnp.zeros((LANES - n_blk, n_col), F32)], axis=0).astype(BF16)
    for t in range(ROUTE_TILES):
        for h in range(2):
            c0 = (2 * t + h) * tq
            pen = _dot_nt(eye_ref[...], pen_t[:, c0:c0 + tq])
            pen_ref[t, h * tq:(h + 1) * tq, :] = pen.astype(BF16)


def moba_route(qa, kmean, cbq, bsz, t_len, tq):
    n_q = t_len // tq
    n_g = n_q // ROUTE_TILES
    n_hp = N_HEADS // 2
    n_blk = t_len // MOBA_BLOCK
    assert n_blk <= LANES and n_q % ROUTE_TILES == 0
    eye = jnp.asarray(np.eye(tq, dtype=np.float32), BF16)
    return pl.pallas_call(
        functools.partial(_moba_route_body, tq=tq, n_blk=n_blk),
        grid=(bsz, n_hp, n_g),
        in_specs=[
            pl.BlockSpec((ROUTE_TILES * tq, LANES), lambda b, p, g: (b * n_g + g, cbq + p)),
            pl.BlockSpec((tq, tq), lambda b, p, g: (0, 0)),
            pl.BlockSpec((None, LANES, LANES), lambda b, p, g: (b, 0, p)),
        ],
        out_specs=pl.BlockSpec((ROUTE_TILES, 2 * tq, LANES),
                               lambda b, p, g: ((b * n_hp + p) * n_g + g, 0, 0)),
        out_shape=jax.ShapeDtypeStruct((bsz * n_hp * n_q, 2 * tq, LANES), BF16),
        compiler_params=_cparams(("parallel", "parallel", "parallel")),
        name="moba_route",
    )(qa, eye, kmean)


def _moba_prompt_body(rb_ref, q_ref, pen_ref, k_ref, v_ref, e_ref, bias_ref, o_ref,
                      qa_ref, s_ref, mrun_ref, lrun_ref, acc_ref, *, tq, n_blk):
    p = pl.program_id(1)
    i = pl.program_id(2)
    per = MOBA_BLOCK // tq
    own = i // per
    par = i % per
    qa_ref[...] = jnp.concatenate(
        [_stack_heads(q_ref[...].astype(F32), HEAD_DIM ** -0.5), pen_ref[...]], axis=1)
    qa = qa_ref[...]

    row = lax.broadcasted_iota(jnp.int32, (2 * tq, MOBA_BLOCK), 0)
    col = lax.broadcasted_iota(jnp.int32, (2 * tq, MOBA_BLOCK), 1)
    causal = col <= row % tq + par * tq
    far_bias = jnp.where(row[:, 0:1] < tq, rb_ref[REL_BUCKETS - 1, 2 * p],
                         rb_ref[REL_BUCKETS - 1, 2 * p + 1])

    def logits(n, bias):
        n0 = pl.multiple_of(n * MOBA_BLOCK, MOBA_BLOCK)
        kaug = jnp.concatenate([k_ref[pl.ds(n0, MOBA_BLOCK), :], e_ref[pl.ds(n0, MOBA_BLOCK), :]],
                               axis=1)
        return _dot_nt(qa, kaug) + bias

    prev = jnp.maximum(own - 1, 0)
    prev_slot = jnp.where(own >= 1, own - 1, n_blk)
    n_far = prev

    def far_loop(step, group):
        def grouped(g, carry):
            step([group * g + j for j in range(group)])
            return carry

        lax.fori_loop(0, n_far // group, grouped, 0)
        base = (n_far // group) * group
        size = group // 2
        while size >= 1:
            take = (n_far - base) >= size

            @pl.when(take)
            def _(base=base, size=size):
                step([base + j for j in range(size)])

            base = base + jnp.where(take, size, 0)
            size //= 2

    def fold(ss, op):
        out = None
        for s in ss:
            h = op(s[:, :LANES], s[:, LANES:])
            out = h if out is None else op(out, h)
        return out

    s_own = jnp.where(causal, logits(own, bias_ref[par, 0].reshape(2 * tq, MOBA_BLOCK)), NEG_BIG)
    s_prev = jnp.where(own >= 1, logits(prev, bias_ref[par, 1].reshape(2 * tq, MOBA_BLOCK)), NEG_BIG)
    s_ref[own] = s_own
    s_ref[prev_slot] = s_prev
    mrun_ref[...] = fold([s_own, s_prev], jnp.maximum)

    def far_max(ns):
        ss = [logits(n, far_bias) for n in ns]
        for n, s in zip(ns, ss):
            s_ref[n] = s
        mrun_ref[...] = jnp.maximum(mrun_ref[...], fold(ss, jnp.maximum))

    far_loop(far_max, 8)
    m = jnp.max(mrun_ref[...], axis=-1, keepdims=True)

    def weigh(slots, ns):
        pes = [jnp.exp(s_ref[sl] - m) for sl in slots]
        o = None
        for pe, n in zip(pes, ns):
            n0 = pl.multiple_of(n * MOBA_BLOCK, MOBA_BLOCK)
            on = _dot(pe.astype(BF16), v_ref[pl.ds(n0, MOBA_BLOCK), :])
            o = on if o is None else o + on
        return fold(pes, jnp.add), o

    lrun_ref[...], acc_ref[...] = weigh([own, prev_slot], [own, prev])

    def far_acc(ns):
        l, o = weigh(ns, ns)
        lrun_ref[...] += l
        acc_ref[...] += o

    far_loop(far_acc, 8)
    o = acc_ref[...] / jnp.sum(lrun_ref[...], axis=-1, keepdims=True)
    o_ref[...] = _unstack_heads(o, tq).astype(o_ref.dtype)


def moba_prompt_bias_idx(tq):
    per = MOBA_BLOCK // tq
    t = np.arange(tq)[:, None]
    s = np.arange(MOBA_BLOCK)[None, :]
    idx = np.stack([np.stack([_rel_bucket_np(par * tq + kind * MOBA_BLOCK + t - s)
                              for kind in range(2)]) for par in range(per)])
    return idx.reshape(per * 2, tq, MOBA_BLOCK)


def moba_prompt(qa, ka, va, pen, rel_bias, bias, cbq, cbk, cbv, bsz, t_len, tq=128):
    n_q = t_len // tq
    n_hp = N_HEADS // 2
    per = MOBA_BLOCK // tq
    bias = bias.reshape(per, 2, N_HEADS, tq, MOBA_BLOCK)
    n_blk = t_len // MOBA_BLOCK
    blk_of_row = np.arange(t_len) // MOBA_BLOCK
    onehot = jnp.asarray(blk_of_row[:, None] == np.arange(LANES)[None, :], BF16)
    return pl.pallas_call(
        functools.partial(_moba_prompt_body, tq=tq, n_blk=n_blk),
        grid=(bsz, n_hp, n_q),
        in_specs=[
            pl.BlockSpec(memory_space=pltpu.SMEM),
            pl.BlockSpec((tq, LANES), lambda b, p, i: (b * n_q + i, cbq + p)),
            pl.BlockSpec((None, 2 * tq, LANES), lambda b, p, i: ((b * n_hp + p) * n_q + i, 0, 0)),
            pl.BlockSpec((t_len, LANES), lambda b, p, i: (b, cbk + p)),
            pl.BlockSpec((t_len, LANES), lambda b, p, i: (b, cbv + p)),
            pl.BlockSpec((t_len, LANES), lambda b, p, i: (0, 0)),
            pl.BlockSpec((per, 2, 2, tq, MOBA_BLOCK), lambda b, p, i: (0, 0, p, 0, 0)),
        ],
        out_specs=pl.BlockSpec((tq, LANES), lambda b, p, i: (b * n_q + i, p)),
        out_shape=jax.ShapeDtypeStruct((bsz * t_len, ATT_WIDTH), BF16),
        scratch_shapes=[pltpu.VMEM((2 * tq, 2 * LANES), BF16),
                        pltpu.VMEM((n_blk + 1, 2 * tq, MOBA_BLOCK), F32),
                        pltpu.VMEM((2 * tq, LANES), F32),
                        pltpu.VMEM((2 * tq, LANES), F32),
                        pltpu.VMEM((2 * tq, LANES), F32)],
        compiler_params=_cparams(("parallel", "parallel", "arbitrary")),
        name="moba_prompt",
    )(rel_bias, qa, pen, ka, va, onehot, bias)


def _ssd_body(xbc_ref, dt_ref, cprev_ref, h0_ref, cw_ref, cb_ref, dtb_ref, alog_ref, dsk_ref,
              e_ref, et_ref, eye_ref, tril_ref, y_ref, h_ref, xf_ref, dtf_ref, *, q):
    c = pl.program_id(1)
    t_in = xbc_ref.shape[0]
    t_out = y_ref.shape[0]
    hp2 = 2 * SSM_P
    gw = (SSM_HEADS // SSM_GROUPS) * SSM_P

    @pl.when(c == 0)
    def _():
        xf_ref[0:SUBLANES, :] = cprev_ref[...]
        h_ref[...] = h0_ref[...]

    if t_in < q:
        xf_ref[SUBLANES:SUBLANES + q, :] = jnp.zeros((q, CONV_DIM), F32)
        dtf_ref[...] = jnp.zeros_like(dtf_ref)
    xf_ref[SUBLANES:SUBLANES + t_in, :] = xbc_ref[...]
    dtf_ref[0:t_in, :] = dt_ref[...]
    base = SUBLANES - (CONV_W - 1)
    conv = cb_ref[...]
    for j in range(CONV_W):
        conv = conv + xf_ref[base + j:base + j + q, :] * cw_ref[j:j + 1, :]
    xf_ref[0:SUBLANES, :] = xf_ref[q:q + SUBLANES, :]
    act = conv * _sigmoid(conv)
    xs = act[:, :SSM_INNER]

    dt = _softplus(dtf_ref[...] + dtb_ref[...])
    if t_in < q:
        trow = lax.broadcasted_iota(jnp.int32, dt.shape, 0)
        dt = jnp.where(trow < t_in, dt, 0.0)
    a = dt * (-jnp.exp(alog_ref[...]))
    a_hi, a_lo = _split2(a)
    tril = tril_ref[...]
    acs = _dot(tril, a_hi) + _dot(tril, a_lo)
    acs_t = _dot2_nt(eye_ref[...], acs)
    e = e_ref[...]
    xr = xs * _dot2(dt, e)
    eacs_e = _dot2(jnp.exp(acs), e)
    xd = xr * _dot2(jnp.exp(acs[q - 1:q, :] - acs), e)
    xd_t = xd.T
    cd = jnp.broadcast_to(jnp.exp(acs_t[:, q - 1:q]), (LANES, LANES))
    cd_hi, cd_lo = _split2(cd)
    et = et_ref[...]
    f = _dot(et, cd_hi) + _dot(et, cd_lo)
    dsk_e = _dot2(jnp.broadcast_to(dsk_ref[...], (SUBLANES, LANES)), e)[0:1, :]

    row = lax.broadcasted_iota(jnp.int32, (q, q), 0)
    col = lax.broadcasted_iota(jnp.int32, (q, q), 1)
    causal = col <= row
    lane = lax.broadcasted_iota(jnp.int32, (q, hp2), 1)
    hpg = SSM_HEADS // SSM_GROUPS
    for g in range(SSM_GROUPS):
        bg = act[:, SSM_INNER + g * SSM_N:SSM_INNER + (g + 1) * SSM_N].astype(BF16)
        cg = act[:, SSM_INNER + (SSM_GROUPS + g) * SSM_N:
                 SSM_INNER + (SSM_GROUPS + g + 1) * SSM_N].astype(BF16)
        cbm = _dot_nt(cg, bg)
        hg = h_ref[g * hpg:(g + 1) * hpg].reshape(gw, SSM_N)
        y_off = _dot_nt(cg, hg.astype(BF16)) * eacs_e[:, g * gw:(g + 1) * gw]
        for pr in range(hpg // 2):
            yd = []
            xr_pair = xr[:, g * gw + pr * hp2:g * gw + (pr + 1) * hp2].astype(BF16)
            for hh in range(2):
                h = g * hpg + pr * 2 + hh
                seg = acs[:, h:h + 1] - acs_t[h:h + 1, :]
                m = (cbm * jnp.where(causal, jnp.exp(seg), 0.0)).astype(BF16)
                yd.append(_dot(m, xr_pair))
            y_pair = (jnp.where(lane < SSM_P, yd[0], yd[1]) + y_off[:, pr * hp2:(pr + 1) * hp2]
                      + xs[:, g * gw + pr * hp2:g * gw + (pr + 1) * hp2]
                      * dsk_e[:, g * gw + pr * hp2:g * gw + (pr + 1) * hp2])
            y_ref[:, g * gw + pr * hp2:g * gw + (pr + 1) * hp2] = y_pair[:t_out]
        st = _dot(xd_t[g * gw:(g + 1) * gw, :].astype(BF16), bg)
        h_ref[g * hpg:(g + 1) * hpg] = (hg * f[g * gw:(g + 1) * gw, :] + st).reshape(hpg, SSM_P, SSM_N)


def _ssd_consts(q):
    hidx = np.arange(SSM_INNER) // SSM_P
    e = (np.arange(LANES)[:, None] == hidx[None, :]).astype(np.float32)
    return (jnp.asarray(e, BF16), jnp.asarray(e.T, BF16),
            jnp.asarray(np.eye(LANES, dtype=np.float32), BF16),
            jnp.asarray(np.tril(np.ones((q, q), np.float32)), BF16))


def ssd(xa, cbx, dta, cprev, h0, state_layer, cw, cb, dtb, alog, dsk, layer, bsz, n_c, q=SSD_CHUNK):
    e, et, eye, tril = _ssd_consts(q)
    xblk = CONV_DIM // LANES
    t_in = xa.shape[1]
    t_out = q if t_in == q else SUBLANES
    const = lambda shape: pl.BlockSpec(shape, lambda b, c: (0,) * len(shape))
    par = lambda shape: pl.BlockSpec((None,) + shape, lambda b, c: (layer, 0, 0))
    return pl.pallas_call(
        functools.partial(_ssd_body, q=q),
        grid=(bsz, n_c),
        in_specs=[
            pl.BlockSpec((None, t_in, CONV_DIM), lambda b, c: (b * n_c + c, 0, cbx // xblk)),
            pl.BlockSpec((None, t_in, LANES), lambda b, c: (b * n_c + c, 0, 0)),
            pl.BlockSpec((None, None, SUBLANES, CONV_DIM), lambda b, c: (state_layer, b, 0, 0)),
            pl.BlockSpec((None, None, SSM_HEADS, SSM_P, SSM_N), lambda b, c: (state_layer, b, 0, 0, 0)),
            par((CONV_W, CONV_DIM)), par((1, CONV_DIM)),
            par((1, LANES)), par((1, LANES)), par((1, LANES)),
            const((LANES, SSM_INNER)), const((SSM_INNER, LANES)), const((LANES, LANES)),
            const((q, q)),
        ],
        out_specs=[
            pl.BlockSpec((None, t_out, SSM_INNER), lambda b, c: (b * n_c + c, 0, 0)),
            pl.BlockSpec((None, SSM_HEADS, SSM_P, SSM_N), lambda b, c: (b, 0, 0, 0)),
        ],
        out_shape=[jax.ShapeDtypeStruct((bsz * n_c, t_out, SSM_INNER), F32),
                   jax.ShapeDtypeStruct((bsz, SSM_HEADS, SSM_P, SSM_N), F32)],
        scratch_shapes=[pltpu.VMEM((q + SUBLANES, CONV_DIM), F32), pltpu.VMEM((q, LANES), F32)],
        compiler_params=_cparams(("parallel", "arbitrary")),
        name="ssd",
    )(xa, dta, cprev, h0, cw, cb, dtb, alog, dsk, e, et, eye, tril)


def _merge_body(x_ref, osb_ref, omb_ref, y_ref, z_ref, g0_ref, g1_ref, g2_ref, gs_ref,
                wsb_ref, wmb_ref, wss_ref, wo_ref, o_ref):
    z = z_ref[...]
    y = _rms(y_ref[...] * (z * _sigmoid(z)), gs_ref[...]).astype(BF16)
    merged = (_sigmoid(g0_ref[...]) * _dot(osb_ref[...], wsb_ref[...])
              + _sigmoid(g1_ref[...]) * _dot(omb_ref[...], wmb_ref[...])
              + _sigmoid(g2_ref[...]) * _dot(y, wss_ref[...]))
    o_ref[...] = x_ref[...] + _dot(merged.astype(BF16), wo_ref[...])


def merge(x, osb, omb, y, u, gs, wsb, wmb, wss, wo, layer, tm=512):
    n, d = x.shape
    db = d // LANES
    row = lambda cb, w: pl.BlockSpec((tm, w), lambda i: (i, cb))
    const = lambda a: pl.BlockSpec((None,) + a.shape[1:], lambda i: (layer, 0, 0))
    return pl.pallas_call(
        _merge_body,
        grid=(n // tm,),
        in_specs=[row(0, d), row(0, ATT_WIDTH), row(0, ATT_WIDTH), row(0, d),
                  row(CB_Z // db, d), row(CB_GATE // db, d), row(CB_GATE // db + 1, d),
                  row(CB_GATE // db + 2, d),
                  const(gs), const(wsb), const(wmb), const(wss), const(wo)],
        out_specs=row(0, d),
        out_shape=jax.ShapeDtypeStruct((n, d), F32),
        compiler_params=_cparams(("parallel",)),
        name="merge",
    )(x, osb, omb, y, u, u, u, u, gs, wsb, wmb, wss, wo)


def _router_body(x_ref, g_ref, wr_ref, xn_ref, cmb_ref):
    xn = _rms(x_ref[...], g_ref[...])
    xn_ref[...] = xn.astype(BF16)
    wr = wr_ref[...]
    xh, xl = _split2(xn)
    wh, wl = _split2(wr[:-SUBLANES, :])
    logit = _dot(xh, wh) + _dot(xh, wl) + _dot(xl, wh) + wr[-SUBLANES:-SUBLANES + 1, :]
    lane = lax.broadcasted_iota(jnp.int32, logit.shape, 1)
    neg = -jnp.inf
    is_g = jnp.logical_and(lane >= N_EXPERTS, lane < N_EXPERTS + N_GROUPS)
    gl = jnp.where(is_g, logit, neg)
    gmax = jnp.max(gl, axis=-1, keepdims=True)
    grp = jnp.min(jnp.where(gl == gmax, lane, LANES), axis=-1, keepdims=True) - N_EXPERTS
    p_grp = 1.0 / jnp.sum(jnp.exp(gl - gmax), axis=-1, keepdims=True)
    in_grp = jnp.logical_and(lane >= grp * EPG, lane < grp * EPG + EPG)
    el = jnp.where(in_grp, logit, neg)
    v1 = jnp.max(el, axis=-1, keepdims=True)
    i1 = jnp.min(jnp.where(el == v1, lane, LANES), axis=-1, keepdims=True)
    el2 = jnp.where(lane == i1, neg, el)
    v2 = jnp.max(el2, axis=-1, keepdims=True)
    i2 = jnp.min(jnp.where(el2 == v2, lane, LANES), axis=-1, keepdims=True)
    d = jnp.exp(v2 - v1)
    w1 = p_grp / (1.0 + d)
    w2 = p_grp * d / (1.0 + d)
    cmb_ref[...] = jnp.where(lane == i1, w1, 0.0) + jnp.where(lane == i2, w2, 0.0)


def router(x, g, wr, layer, tm=512):
    n, d = x.shape
    return pl.pallas_call(
        _router_body,
        grid=(n // tm,),
        in_specs=[pl.BlockSpec((tm, d), lambda i: (i, 0)),
                  pl.BlockSpec((None, 1, d), lambda i: (layer, 0, 0)),
                  pl.BlockSpec((None,) + wr.shape[1:], lambda i: (layer, 0, 0))],
        out_specs=[pl.BlockSpec((tm, d), lambda i: (i, 0)),
                   pl.BlockSpec((tm, LANES), lambda i: (i, 0))],
        out_shape=[jax.ShapeDtypeStruct((n, d), BF16),
                   jax.ShapeDtypeStruct((n, LANES), F32)],
        compiler_params=_cparams(("parallel",)),
        name="router",
    )(x, g, wr)


MOE_EXPERTS_PER_STEP = 4


def _moe_body(x_ref, xn_ref, cmb_ref, wgu_ref, wd_ref, gf_ref, o_ref, *, final_norm):
    g = pl.program_id(1)
    eps = MOE_EXPERTS_PER_STEP

    @pl.when(g == 0)
    def _():
        o_ref[...] = x_ref[...]

    cmb = cmb_ref[...]
    xn = xn_ref[...]
    lane = lax.broadcasted_iota(jnp.int32, cmb.shape, 1)
    hids = []
    for j in range(eps):
        c = jnp.sum(jnp.where(lane == g * eps + j, cmb, 0.0), axis=-1, keepdims=True)
        gu = _dot(xn, wgu_ref[j])
        gt = gu[:, :EXPERT_FF]
        hids.append((c * (gt * _sigmoid(gt) * gu[:, EXPERT_FF:])).astype(BF16))
    wd = wd_ref[...].reshape(eps * EXPERT_FF, wd_ref.shape[-1])
    o_ref[...] += _dot(jnp.concatenate(hids, axis=1), wd)

    if final_norm:
        @pl.when(g == pl.num_programs(1) - 1)
        def _():
            o_ref[...] = _rms(o_ref[...], gf_ref[...])


def moe(x, xn, cmb, wgu, wd, gf, layer, final_norm, tm=512):
    n, d = x.shape
    eps = MOE_EXPERTS_PER_STEP
    return pl.pallas_call(
        functools.partial(_moe_body, final_norm=final_norm),
        grid=(n // tm, N_EXPERTS // eps),
        in_specs=[pl.BlockSpec((tm, d), lambda i, e: (i, 0)),
                  pl.BlockSpec((tm, d), lambda i, e: (i, 0)),
                  pl.BlockSpec((tm, LANES), lambda i, e: (i, 0)),
                  pl.BlockSpec((None, eps, d, 2 * EXPERT_FF), lambda i, e: (layer, e, 0, 0)),
                  pl.BlockSpec((None, eps, EXPERT_FF, d), lambda i, e: (layer, e, 0, 0)),
                  pl.BlockSpec((1, d), lambda i, e: (0, 0))],
        out_specs=pl.BlockSpec((tm, d), lambda i, e: (i, 0)),
        out_shape=jax.ShapeDtypeStruct((n, d), F32),
        compiler_params=_cparams(("parallel", "arbitrary")),
        name="moe",
    )(x, xn, cmb, wgu, wd, gf)


T_PAD = SUBLANES
QROWS = N_HEADS * T_PAD


def _pad_rows(x_ref, buf):
    buf[0:T_PAD, :] = jnp.zeros((T_PAD, buf.shape[1]), buf.dtype)
    buf[0:x_ref.shape[0], :] = x_ref[...]
    return buf[0:T_PAD, :]


def _q_block_diag(q8):
    lane = lax.broadcasted_iota(jnp.int32, q8.shape, 1)
    return jnp.concatenate(
        [jnp.where(lane // HEAD_DIM == h, q8, 0.0) for h in range(N_HEADS)], axis=0)


def _out_from_acc(acc):
    lane = lax.broadcasted_iota(jnp.int32, (T_PAD, ATT_WIDTH), 1)
    out = jnp.zeros((T_PAD, ATT_WIDTH), F32)
    for h in range(N_HEADS):
        out = jnp.where(lane // HEAD_DIM == h, acc[h * T_PAD:(h + 1) * T_PAD, :], out)
    return out


N_AHEAD = 2


def _sb_sample_body(pt_ref, q_ref, kn_ref, vn_ref, ck_ref, cv_ref, tri_ref, o_ref,
                    kbuf, vbuf, acc_ref, car_ref, flag, kpg, vpg, kx, vx, sem_pg, sem_x,
                    *, n_pages, t_new, layer):
    b = pl.program_id(0)
    n_seq = pl.num_programs(0)
    slot = b % 2
    scale = HEAD_DIM ** -0.5
    qbd = (_q_block_diag(_pad_rows(q_ref, kbuf)) * scale).astype(BF16)
    tri = tri_ref[...]

    def ahead_copies(seq, sl):
        cps = []
        for j in range(N_AHEAD):
            pg = pt_ref[seq, n_pages - 1 - j]
            cps.append(pltpu.make_async_copy(ck_ref.at[layer, pg], kpg.at[sl, j], sem_pg.at[sl, j, 0]))
            cps.append(pltpu.make_async_copy(cv_ref.at[layer, pg], vpg.at[sl, j], sem_pg.at[sl, j, 1]))
        return cps

    def demand_copies(p):
        pg = pt_ref[b, p]
        return [pltpu.make_async_copy(ck_ref.at[layer, pg], kx, sem_x.at[0]),
                pltpu.make_async_copy(cv_ref.at[layer, pg], vx, sem_x.at[1])]

    @pl.when(b == 0)
    def _():
        for cp in ahead_copies(0, 0):
            cp.start()

    @pl.when(b + 1 < n_seq)
    def _():
        for cp in ahead_copies(b + 1, 1 - slot):
            cp.start()

    def page(z, weigh, mask):
        lk = -_softplus(z)
        if mask is not None:
            lk = jnp.where(mask, lk, 0.0)
        sums = _suffix_sums(lk, tri)
        w = jnp.exp(z + sums[:, :PAGE] + car_ref[...])
        if mask is not None:
            w = jnp.where(mask, w, 0.0)
        acc_ref[...] += weigh(w.astype(BF16))
        car = car_ref[...] + sums[:, PAGE:]
        car_ref[...] = car
        flag[0] = (jnp.max(car) >= EXP_ZERO).astype(jnp.int32)

    kbuf[...] = jnp.zeros_like(kbuf)
    vbuf[...] = jnp.zeros_like(vbuf)
    kbuf[0:t_new, :] = kn_ref[...]
    vbuf[0:t_new, :] = vn_ref[...]
    acc_ref[...] = jnp.zeros_like(acc_ref)
    car_ref[...] = jnp.zeros_like(car_ref)
    r = lax.broadcasted_iota(jnp.int32, (QROWS, PAGE), 0)
    c = lax.broadcasted_iota(jnp.int32, (QROWS, PAGE), 1)
    past = jnp.logical_and(c < r % T_PAD, c < t_new)
    page(_dot_nt(qbd, kbuf[...].astype(BF16)), lambda w: _dot(w, vbuf[...].astype(BF16)), past)

    def cache_page(k_page, v_page):
        vt = v_page.reshape(ATT_WIDTH, PAGE).astype(BF16)
        page(_dot(qbd, k_page.reshape(ATT_WIDTH, PAGE).astype(BF16)), lambda w: _dot_nt(w, vt), None)

    for cp in ahead_copies(b, slot):
        cp.wait()
    for j in range(N_AHEAD):
        @pl.when(flag[0] > 0)
        def _():
            cache_page(kpg[slot, j], vpg[slot, j])

    for p in reversed(range(n_pages - N_AHEAD)):
        @pl.when(flag[0] > 0)
        def _():
            cps = demand_copies(p)
            for cp in cps:
                cp.start()
            for cp in cps:
                cp.wait()
            cache_page(kx[...], vx[...])

    o_ref[...] = _out_from_acc(acc_ref[...]).astype(o_ref.dtype)


def _page_specs(layer, n_pages):
    return [pl.BlockSpec((None, None, N_HEADS, HEAD_DIM, PAGE),
                         functools.partial(lambda b, pt, p: (layer, pt[b, p], 0, 0, 0), p=p))
            for p in range(n_pages)]


def sb_sample(page_table, us8, cache_k, cache_v, layer, t_new):
    n_seq, n_pages = page_table.shape
    assert n_pages >= N_AHEAD
    tri = _tri_ge_ones(PAGE)
    tok = lambda cb: pl.BlockSpec((None, t_new, ATT_WIDTH), lambda b, pt: (b, 0, cb // 4))
    const = lambda a: pl.BlockSpec(a.shape, lambda b, pt: (0,) * a.ndim)
    page_shape = (N_HEADS, HEAD_DIM, PAGE)
    grid_spec = pltpu.PrefetchScalarGridSpec(
        num_scalar_prefetch=1,
        grid=(n_seq,),
        in_specs=[tok(CB_Q_SB), tok(CB_K_SB), tok(CB_V_SB),
                  pl.BlockSpec(memory_space=pl.ANY), pl.BlockSpec(memory_space=pl.ANY),
                  const(tri)],
        out_specs=pl.BlockSpec((None, T_PAD, ATT_WIDTH), lambda b, pt: (b, 0, 0)),
        scratch_shapes=[pltpu.VMEM((PAGE, ATT_WIDTH), F32), pltpu.VMEM((PAGE, ATT_WIDTH), F32),
                        pltpu.VMEM((QROWS, ATT_WIDTH), F32), pltpu.VMEM((QROWS, LANES), F32),
                        pltpu.SMEM((1,), jnp.int32),
                        pltpu.VMEM((2, N_AHEAD) + page_shape, F32),
                        pltpu.VMEM((2, N_AHEAD) + page_shape, F32),
                        pltpu.VMEM(page_shape, F32), pltpu.VMEM(page_shape, F32),
                        pltpu.SemaphoreType.DMA((2, N_AHEAD, 2)), pltpu.SemaphoreType.DMA((2,))],
    )
    return pl.pallas_call(
        functools.partial(_sb_sample_body, n_pages=n_pages, t_new=t_new, layer=layer),
        grid_spec=grid_spec,
        out_shape=jax.ShapeDtypeStruct((n_seq, T_PAD, ATT_WIDTH), BF16),
        compiler_params=_cparams(("arbitrary",)),
        name="sb_sample",
    )(page_table, us8, us8, us8, cache_k, cache_v, tri)


def _moba_sample_body(pt_ref, q_ref, kn_ref, vn_ref, *rest, n_pages, t_new):
    k_refs = rest[:n_pages]
    v_refs = rest[n_pages:2 * n_pages]
    bias_ref, o_ref, kbuf, vbuf, s_ref = rest[2 * n_pages:]
    ppb = MOBA_BLOCK // PAGE
    n_blk = n_pages // ppb
    scale = HEAD_DIM ** -0.5
    qf = _q_block_diag(_pad_rows(q_ref, kbuf))
    qs = (qf * scale).astype(BF16)
    neg = -jnp.inf
    lane = lax.broadcasted_iota(jnp.int32, (QROWS, LANES), 1)
    lane_w = lax.broadcasted_iota(jnp.int32, (ATT_WIDTH, LANES), 1)

    ksum = jnp.zeros((ATT_WIDTH, LANES), F32)
    for n in range(n_blk):
        blk = k_refs[ppb * n][...].reshape(ATT_WIDTH, PAGE)
        for j in range(1, ppb):
            blk = blk + k_refs[ppb * n + j][...].reshape(ATT_WIDTH, PAGE)
        ksum = jnp.where(lane_w == n, jnp.sum(blk, axis=-1, keepdims=True), ksum)
    kmean = ksum * (1.0 / MOBA_BLOCK)
    qh, ql = _split2(qf)
    mh, ml = _split2(kmean)
    route = _dot(qh, mh) + _dot(qh, ml) + _dot(ql, mh)
    sel = _route_topk(route, lane, n_blk, n_blk)

    mrun = jnp.full((QROWS, PAGE), neg, F32)
    for p in range(n_pages):
        s = _dot(qs, k_refs[p][...].reshape(ATT_WIDTH, PAGE).astype(BF16)) + bias_ref[p]
        s = jnp.where(sel[:, p // ppb:p // ppb + 1] > 0.5, s, neg)
        s_ref[p] = s
        mrun = jnp.maximum(mrun, s)
    kbuf[...] = jnp.zeros_like(kbuf)
    vbuf[...] = jnp.zeros_like(vbuf)
    kbuf[0:t_new, :] = kn_ref[...]
    vbuf[0:t_new, :] = vn_ref[...]
    r = lax.broadcasted_iota(jnp.int32, (QROWS, PAGE), 0)
    c = lax.broadcasted_iota(jnp.int32, (QROWS, PAGE), 1)
    causal = jnp.logical_and(c <= r % T_PAD, c < T_PAD)
    s = _dot_nt(qs, kbuf[...].astype(BF16)) + bias_ref[n_pages]
    s = jnp.where(causal, s, neg)
    s_ref[n_pages] = s
    m = jnp.max(jnp.maximum(mrun, s), axis=-1, keepdims=True)

    lsum = jnp.zeros((QROWS, PAGE), F32)
    acc = jnp.zeros((QROWS, ATT_WIDTH), F32)
    for p in range(n_pages + 1):
        pe = jnp.exp(s_ref[p] - m)
        lsum = lsum + pe
        pe = pe.astype(BF16)
        if p < n_pages:
            acc = acc + _dot_nt(pe, v_refs[p][...].reshape(ATT_WIDTH, PAGE).astype(BF16))
        else:
            acc = acc + _dot(pe, vbuf[...].astype(BF16))
    o_ref[...] = _out_from_acc(acc / jnp.sum(lsum, axis=-1, keepdims=True)).astype(o_ref.dtype)


def moba_sample_bias_idx(n_pages):
    t = np.arange(T_PAD)[:, None]
    s = np.arange(PAGE)[None, :]
    past = [_rel_bucket_np((n_pages - p) * PAGE + t - s) for p in range(n_pages)]
    return np.stack(past + [_rel_bucket_np(t - s)])


def moba_sample(page_table, us8, cache_k, cache_v, bias, layer, t_new):
    n_seq, n_pages = page_table.shape
    tok = lambda cb: pl.BlockSpec((None, t_new, ATT_WIDTH), lambda b, pt: (b, 0, cb // 4))
    const = lambda a: pl.BlockSpec(a.shape, lambda b, pt: (0,) * a.ndim)
    grid_spec = pltpu.PrefetchScalarGridSpec(
        num_scalar_prefetch=1,
        grid=(n_seq,),
        in_specs=[tok(CB_Q_MB), tok(CB_K_MB), tok(CB_V_MB)]
        + _page_specs(layer, n_pages) + _page_specs(layer, n_pages)
        + [const(bias)],
        out_specs=pl.BlockSpec((None, T_PAD, ATT_WIDTH), lambda b, pt: (b, 0, 0)),
        scratch_shapes=[pltpu.VMEM((PAGE, ATT_WIDTH), F32), pltpu.VMEM((PAGE, ATT_WIDTH), F32),
                        pltpu.VMEM((n_pages + 1, QROWS, PAGE), F32)],
    )
    return pl.pallas_call(
        functools.partial(_moba_sample_body, n_pages=n_pages, t_new=t_new),
        grid_spec=grid_spec,
        out_shape=jax.ShapeDtypeStruct((n_seq, T_PAD, ATT_WIDTH), BF16),
        compiler_params=_cparams(("arbitrary",)),
        name="moba_sample",
    )(page_table, us8, us8, us8, *([cache_k] * n_pages), *([cache_v] * n_pages), bias)


def kernel(x_prompt, x_sample, cache_k_sb, cache_v_sb, cache_k_moba, cache_v_moba, page_table, state_ssm, state_conv, rel_bias, g_mix, w_in, conv_w, conv_b, dt_bias, a_log, d_skip, g_ssm, w_branch_sb, w_branch_moba, w_branch_ssm, w_out, g_ffn, w_router_group, b_router_group, w_router_expert, b_router_expert, w_expert_gate, w_expert_up, w_expert_down, g_final):
    bsz, seq, d = x_prompt.shape
    n_seq, t_new, _ = x_sample.shape
    depth = w_in.shape[0]
    n_pages = page_table.shape[1]
    n_p = bsz * seq
    n_s = n_seq * t_new
    assert d == D_MODEL and seq % MOBA_BLOCK == 0 and t_new <= T_PAD
    assert (n_pages * PAGE) % MOBA_BLOCK == 0 and cache_k_sb.shape[2] == PAGE

    x_p = x_prompt.reshape(n_p, d)
    x_s = x_sample.reshape(n_s, d)
    tm_p = 1024 if n_p % 1024 == 0 else 512
    tm_s = min(n_s, 512)
    n_c = seq // SSD_CHUNK
    pages = [c.transpose(0, 1, 3, 4, 2) for c in (cache_k_sb, cache_v_sb, cache_k_moba, cache_v_moba)]
    tq_mb = MOBA_BLOCK
    bias_p = bias_tiles(rel_bias, jnp.asarray(moba_prompt_bias_idx(tq_mb)))
    bias_s = bias_tiles(rel_bias, jnp.asarray(moba_sample_bias_idx(n_pages))).reshape(
        n_pages + 1, QROWS, PAGE)
    zero_conv = jnp.zeros((1, bsz, SUBLANES, CONV_DIM), F32)
    zero_state = jnp.zeros((1, bsz, SSM_HEADS, SSM_P, SSM_N), F32)
    n_main = CB_XBC * LANES + CONV_DIM

    def col(a, cb, w):
        return a[..., cb * LANES:cb * LANES + w]

    def kv(a, cb):
        return col(a, cb, ATT_WIDTH).reshape(a.shape[0], a.shape[1], N_HEADS, HEAD_DIM)

    def lane_pad(v):
        return jnp.pad(v, ((0, 0), (0, LANES - v.shape[1])))[:, None, :]

    w_main = jnp.concatenate([w_in[:, :, :n_main], w_in[:, :, n_main + SSM_HEADS:]], axis=2).astype(BF16)
    w_dt = jnp.pad(w_in[:, :, n_main:n_main + SSM_HEADS],
                   ((0, 0), (0, 0), (0, LANES - SSM_HEADS))).astype(BF16)
    gm, gs, gf = g_mix[:, None, :], g_ssm[:, None, :], g_ffn[:, None, :]
    ssm_par = (conv_w, conv_b[:, None, :], lane_pad(dt_bias), lane_pad(a_log), lane_pad(d_skip))
    lane_fill = LANES - N_EXPERTS - N_GROUPS
    wr = jnp.concatenate([
        jnp.pad(jnp.concatenate([w_router_expert, w_router_group], axis=2),
                ((0, 0), (0, 0), (0, lane_fill))),
        jnp.pad(jnp.concatenate([b_router_expert, b_router_group], axis=1)[:, None, :],
                ((0, 0), (0, SUBLANES - 1), (0, lane_fill)))], axis=1)
    wgu = jnp.concatenate([w_expert_gate, w_expert_up], axis=-1).astype(BF16)
    wd = w_expert_down.astype(BF16)
    w_br = (w_branch_sb.astype(BF16), w_branch_moba.astype(BF16),
            w_branch_ssm.astype(BF16), w_out.astype(BF16))
    cprev_s = jnp.pad(state_conv, ((0, 0), (0, 0), (SUBLANES - (CONV_W - 1), 0), (0, 0)))
    g_fin = g_final.reshape(1, d)

    new_p, new_s = [], []
    for l in range(depth):
        def tail(x, osb, omb, y, u, tm):
            x1 = merge(x, osb, omb, y, u, gs, *w_br, l, tm=min(tm, 512))
            xn, cmb = router(x1, gf, wr, l, tm=min(tm, 512))
            return moe(x1, xn, cmb, wgu, wd, g_fin, l, final_norm=(l == depth - 1), tm=tm)

        u_p, dt_p, qkv_p = in_proj(x_p, gm, w_main, w_dt, l, tm=tm_p)
        osb_p = sb_prompt(qkv_p, qkv_p, qkv_p, CB_Q_SB, CB_K_SB, CB_V_SB, bsz, seq)
        km = kmean_prompt(u_p, CB_K_MB, bsz, seq)
        pen = moba_route(u_p, km, CB_Q_MB, bsz, seq, tq_mb)
        omb_p = moba_prompt(qkv_p, qkv_p, qkv_p, pen, rel_bias, bias_p, CB_Q_MB, CB_K_MB, CB_V_MB,
                            bsz, seq, tq=tq_mb)
        y_p, ssm_p = ssd(u_p.reshape(bsz * n_c, SSD_CHUNK, U_COLS), CB_XBC,
                         dt_p.reshape(bsz * n_c, SSD_CHUNK, LANES), zero_conv, zero_state, 0,
                         *ssm_par, l, bsz=bsz, n_c=n_c)
        x_p = tail(x_p, osb_p, omb_p, y_p.reshape(n_p, SSM_INNER), u_p, tm_p)

        u_s, dt_s, _ = in_proj(x_s, gm, w_main, w_dt, l, tm=tm_s)
        u_s3 = u_s.reshape(n_seq, t_new, U_COLS)
        osb_s = sb_sample(page_table, u_s3, pages[0], pages[1], l, t_new)[:, :t_new]
        omb_s = moba_sample(page_table, u_s3, pages[2], pages[3], bias_s, l, t_new)[:, :t_new]
        y_s, ssm_s = ssd(u_s3, CB_XBC, dt_s.reshape(n_seq, t_new, LANES), cprev_s, state_ssm, l,
                         *ssm_par, l, bsz=n_seq, n_c=1, q=T_PAD)
        x_s = tail(x_s, osb_s.reshape(n_s, ATT_WIDTH), omb_s.reshape(n_s, ATT_WIDTH),
                   y_s[:, :t_new].reshape(n_s, SSM_INNER), u_s, tm_s)

        u_p3 = u_p.reshape(bsz, seq, U_COLS)
        new_p.append((kv(u_p3, CB_K_SB), kv(u_p3, CB_V_SB), kv(u_p3, CB_K_MB), kv(u_p3, CB_V_MB),
                      ssm_p, col(u_p3, CB_XBC, CONV_DIM)[:, seq - (CONV_W - 1):]))
        conv_s = jnp.concatenate([state_conv[l], col(u_s3, CB_XBC, CONV_DIM)], axis=1)[:, t_new:]
        new_s.append((kv(u_s3, CB_K_SB), kv(u_s3, CB_V_SB), kv(u_s3, CB_K_MB), kv(u_s3, CB_V_MB),
                      ssm_s, conv_s))

    stack = lambda states: tuple(jnp.stack([s[i] for s in states]) for i in range(6))
    return ((x_p.reshape(bsz, seq, d), x_s.reshape(n_seq, t_new, d)) + stack(new_p) + stack(new_s))
```

```python
import functools
import math

import numpy as np
import jax
import jax.numpy as jnp
from jax import lax
from jax.experimental import pallas as pl
from jax.experimental.pallas import tpu as pltpu

F32 = jnp.float32
BF16 = jnp.bfloat16

LANES = 128
SUBLANES = 8
VMEM_LIMIT = 56 * 1024 * 1024

D_MODEL = 1024
HEAD_DIM = 64
N_HEADS = 8
ATT_WIDTH = N_HEADS * HEAD_DIM
MOBA_BLOCK = 256
MOBA_TOPK = 3
PAGE = 128
SSM_HEADS = 16
SSM_P = 64
SSM_GROUPS = 4
SSM_N = 128
SSM_INNER = SSM_HEADS * SSM_P
CONV_W = 4
CONV_DIM = SSM_INNER + 2 * SSM_GROUPS * SSM_N
SSD_CHUNK = 128
REL_BUCKETS = 32
REL_MAX_DIST = 128
N_GROUPS = 4
EPG = 8
N_EXPERTS = N_GROUPS * EPG
EXPERT_FF = 256
EPS = 1e-6
EXP_ZERO = -88.0
NEG_BIG = -1e30

CB_Q_SB, CB_K_SB, CB_V_SB = 0, 4, 8
CB_Q_MB, CB_K_MB, CB_V_MB = 12, 16, 20
CB_Z = 24
CB_XBC = 32
CB_GATE = 48
U_COLS = 72 * LANES


def _cparams(sem):
    return pltpu.CompilerParams(dimension_semantics=sem, vmem_limit_bytes=VMEM_LIMIT)


def _split2(a):
    hi = a.astype(BF16)
    lo = (a - hi.astype(F32)).astype(BF16)
    return hi, lo


def _dot(a, b):
    return jnp.dot(a, b, preferred_element_type=F32)


def _dot_nt(a, b):
    return lax.dot_general(a, b, (((1,), (1,)), ((), ())), preferred_element_type=F32)


def _dot2(a, b_exact):
    hi, lo = _split2(a)
    return _dot(hi, b_exact) + _dot(lo, b_exact)


def _dot2_nt(a_exact, b):
    hi, lo = _split2(b)
    return _dot_nt(a_exact, hi) + _dot_nt(a_exact, lo)


def _dot3_nt(a, b):
    ah, al = _split2(a)
    bh, bl = _split2(b)
    return _dot_nt(ah, bh) + _dot_nt(ah, bl) + _dot_nt(al, bh)


def _sigmoid(x):
    return 1.0 / (1.0 + jnp.exp(-x))


def _softplus(x):
    return jnp.maximum(x, 0.0) + jnp.log(1.0 + jnp.exp(-jnp.abs(x)))


def _rms(x, g):
    ms = jnp.mean(x * x, axis=-1, keepdims=True)
    return x * lax.rsqrt(ms + EPS) * g


def _in_proj_body(x_ref, g_ref, w_ref, wdt_ref, u_ref, dt_ref, qkv_ref, xn_ref, *, n_qkv):
    j = pl.program_id(1)

    @pl.when(j == 0)
    def _():
        xn = _rms(x_ref[...], g_ref[...]).astype(BF16)
        xn_ref[...] = xn
        dt_ref[...] = _dot(xn, wdt_ref[...])

    u = _dot(xn_ref[...], w_ref[...])
    u_ref[...] = u

    @pl.when(j < n_qkv)
    def _():
        qkv_ref[...] = u.astype(BF16)


def in_proj(x, g, w, wdt, layer, tm=512, tn=1024):
    n, d = x.shape
    cols = w.shape[2]
    n_qkv = (CB_Z * LANES) // tn
    return pl.pallas_call(
        functools.partial(_in_proj_body, n_qkv=n_qkv),
        grid=(n // tm, cols // tn),
        in_specs=[
            pl.BlockSpec((tm, d), lambda i, j: (i, 0)),
            pl.BlockSpec((None, 1, d), lambda i, j: (layer, 0, 0)),
            pl.BlockSpec((None, d, tn), lambda i, j: (layer, 0, j)),
            pl.BlockSpec((None, d, LANES), lambda i, j: (layer, 0, 0)),
        ],
        out_specs=[
            pl.BlockSpec((tm, tn), lambda i, j: (i, j)),
            pl.BlockSpec((tm, LANES), lambda i, j: (i, 0)),
            pl.BlockSpec((tm, tn), lambda i, j: (i, jnp.minimum(j, n_qkv - 1))),
        ],
        out_shape=[jax.ShapeDtypeStruct((n, cols), F32),
                   jax.ShapeDtypeStruct((n, LANES), F32),
                   jax.ShapeDtypeStruct((n, n_qkv * tn), BF16)],
        scratch_shapes=[pltpu.VMEM((tm, d), BF16)],
        compiler_params=_cparams(("parallel", "arbitrary")),
        name="in_proj",
    )(x, g, w, wdt)


def _suffix_sums(lk, tri2):
    hi, lo = _split2(lk)
    return _dot(jnp.concatenate([hi, lo], axis=1), tri2)


def _sb_tile(qh, k, v, carry, tri2, mask):
    tk = k.shape[0]
    z = _dot_nt(qh, k)
    lk = -_softplus(z)
    if mask is not None:
        lk = jnp.where(mask, lk, 0.0)
    sums = _suffix_sums(lk, tri2)
    w = jnp.exp(z + sums[:, :tk] + carry)
    if mask is not None:
        w = jnp.where(mask, w, 0.0)
    return _dot(w.astype(BF16), v), carry + sums[:, tk:]


def _stack_heads(q, scale):
    lane = lax.broadcasted_iota(jnp.int32, q.shape, 1)
    q = q * scale
    return jnp.concatenate([jnp.where(lane < HEAD_DIM, q, 0.0),
                            jnp.where(lane >= HEAD_DIM, q, 0.0)], axis=0).astype(BF16)


def _unstack_heads(o, tq):
    lane = lax.broadcasted_iota(jnp.int32, (tq, LANES), 1)
    return jnp.where(lane < HEAD_DIM, o[:tq], o[tq:])


def _sb_prompt_body(q_ref, k_ref, v_ref, tri_ref, o_ref, acc_ref, car_ref, *, tk):
    i = pl.program_id(2)
    scale = HEAD_DIM ** -0.5
    q = q_ref[...].astype(F32)
    qa = jnp.concatenate([_stack_heads(q[:tk], scale), _stack_heads(q[tk:], scale)], axis=0)
    hb = 2 * tk
    tri2 = tri_ref[...]
    row = lax.broadcasted_iota(jnp.int32, (2 * hb, tk), 0)
    col = lax.broadcasted_iota(jnp.int32, (2 * hb, tk), 1)
    tri_mask = col < row % tk

    def tile(j, carry, mask):
        kj = pl.multiple_of(j * tk, tk)
        return _sb_tile(qa, k_ref[pl.ds(kj, tk), :], v_ref[pl.ds(kj, tk), :], carry, tri2, mask)

    tiles = [(2 * i + 1, qa[hb:], tri_mask[hb:]),
             (2 * i, qa, jnp.logical_or(row >= hb, tri_mask)),
             (jnp.maximum(2 * i - 1, 0), qa, jnp.broadcast_to(i > 0, (2 * hb, tk)))]
    starts = [pl.multiple_of(j * tk, tk) for j, _, _ in tiles]
    zs = [_dot_nt(qr, k_ref[pl.ds(s, tk), :]) for s, (_, qr, _) in zip(starts, tiles)]
    lks = [jnp.where(mask, -_softplus(z), 0.0) for z, (_, _, mask) in zip(zs, tiles)]
    sums = [_suffix_sums(lk, tri2) for lk in lks]
    ws = []
    c = jnp.zeros((hb, tk), F32)
    for t, (z, sm, (_, _, mask)) in enumerate(zip(zs, sums, tiles)):
        ws.append(jnp.where(mask, jnp.exp(z + sm[:, :tk] + c), 0.0).astype(BF16))
        c = c + sm[:, tk:]
        if t == 0:
            c = jnp.concatenate([jnp.zeros((hb, tk), F32), c], axis=0)
    os_ = [_dot(w, v_ref[pl.ds(s, tk), :]) for w, s in zip(ws, starts)]
    acc_ref[...] = os_[1] + os_[2]
    acc_ref[hb:, :] += os_[0]
    car_ref[...] = c

    def cond(j):
        return jnp.logical_and(j >= 0, jnp.max(car_ref[...]) >= EXP_ZERO)

    def body(j):
        o, c = tile(j, car_ref[...], None)
        acc_ref[...] += o
        car_ref[...] = c
        return j - 1

    lax.while_loop(cond, body, 2 * i - 2)
    o_ref[0:tk, :] = _unstack_heads(acc_ref[0:hb, :], tk).astype(o_ref.dtype)
    o_ref[tk:, :] = _unstack_heads(acc_ref[hb:, :], tk).astype(o_ref.dtype)


def _tri_ge_ones(n):
    t2 = np.concatenate([np.tril(np.ones((n, n), np.float32)), np.ones((n, n), np.float32)], axis=1)
    return jnp.asarray(np.concatenate([t2, t2], axis=0), BF16)


def sb_prompt(qa, ka, va, cbq, cbk, cbv, bsz, t_len, tk=128):
    tq = 2 * tk
    n_q = t_len // tq
    n_hp = N_HEADS // 2
    return pl.pallas_call(
        functools.partial(_sb_prompt_body, tk=tk),
        grid=(bsz, n_hp, n_q),
        in_specs=[
            pl.BlockSpec((tq, LANES), lambda b, p, i: (b * n_q + i, cbq + p)),
            pl.BlockSpec((t_len, LANES), lambda b, p, i: (b, cbk + p)),
            pl.BlockSpec((t_len, LANES), lambda b, p, i: (b, cbv + p)),
            pl.BlockSpec((2 * tk, 2 * tk), lambda b, p, i: (0, 0)),
        ],
        out_specs=pl.BlockSpec((tq, LANES), lambda b, p, i: (b * n_q + i, p)),
        out_shape=jax.ShapeDtypeStruct((bsz * t_len, ATT_WIDTH), BF16),
        scratch_shapes=[pltpu.VMEM((2 * tq, LANES), F32), pltpu.VMEM((2 * tq, tk), F32)],
        compiler_params=_cparams(("parallel", "parallel", "arbitrary")),
        name="sb_prompt",
    )(qa, ka, va, _tri_ge_ones(tk))


def _rel_bucket_np(dist):
    exact = REL_BUCKETS // 2
    d = np.maximum(dist, 0)
    df = np.maximum(d, 1).astype(np.float32)
    large = exact + (np.log(df / np.float32(exact)) / np.float32(math.log(REL_MAX_DIST / exact))
                     * np.float32(REL_BUCKETS - exact)).astype(np.int32)
    return np.where(d < exact, d, np.minimum(large, REL_BUCKETS - 1)).astype(np.int32)


def _bias_body(rb_ref, idx_ref, o_ref):
    idx = idx_ref[...]
    for h in range(N_HEADS):
        acc = jnp.zeros(idx.shape, F32)
        for b in range(REL_BUCKETS):
            acc = jnp.where(idx == b, rb_ref[b, h], acc)
        o_ref[h] = acc


def bias_tiles(rel_bias, idx):
    nv, r, c = idx.shape
    return pl.pallas_call(
        _bias_body,
        grid=(nv,),
        in_specs=[pl.BlockSpec(memory_space=pltpu.SMEM),
                  pl.BlockSpec((None, r, c), lambda v: (v, 0, 0))],
        out_specs=pl.BlockSpec((None, N_HEADS, r, c), lambda v: (v, 0, 0, 0)),
        out_shape=jax.ShapeDtypeStruct((nv, N_HEADS, r, c), F32),
        compiler_params=_cparams(("parallel",)),
        name="bias_tiles",
    )(rel_bias, idx)


def _kmean_body(k_ref, o_ref):
    n = pl.program_id(1)

    @pl.when(n == 0)
    def _():
        o_ref[...] = jnp.zeros_like(o_ref)

    o_ref[pl.ds(n, 1), :] = jnp.mean(k_ref[...], axis=0, keepdims=True)


def kmean_prompt(ka, cbk, bsz, t_len):
    n_blk = t_len // MOBA_BLOCK
    return pl.pallas_call(
        _kmean_body,
        grid=(bsz, n_blk),
        in_specs=[pl.BlockSpec((MOBA_BLOCK, ATT_WIDTH), lambda b, n: (b * n_blk + n, cbk // 4))],
        out_specs=pl.BlockSpec((None, LANES, ATT_WIDTH), lambda b, n: (b, 0, 0)),
        out_shape=jax.ShapeDtypeStruct((bsz, LANES, ATT_WIDTH), F32),
        compiler_params=_cparams(("parallel", "arbitrary")),
        name="kmean_prompt",
    )(ka)


def _route_topk(route, lane, n_valid, n_ok):
    neg = -jnp.inf
    r = jnp.where(lane < n_valid, route, neg)
    sel = jnp.zeros(route.shape, F32)
    for j in range(MOBA_TOPK):
        m = jnp.max(r, axis=-1, keepdims=True)
        idx = jnp.min(jnp.where(r == m, lane, LANES), axis=-1, keepdims=True)
        hit = lane == idx
        sel = jnp.where(jnp.logical_and(hit, j < n_ok), 1.0, sel)
        r = jnp.where(hit, neg, r)
    return sel


def _block_penalty_t(route_t, own):
    blk = lax.broadcasted_iota(jnp.int32, route_t.shape, 0)
    neg = -jnp.inf
    r = jnp.where(blk < own, route_t, neg)
    keep = blk == own
    for j in range(MOBA_TOPK):
        m = jnp.max(r, axis=0, keepdims=True)
        idx = jnp.min(jnp.where(r == m, blk, route_t.shape[0]), axis=0, keepdims=True)
        hit = blk == idx
        keep = jnp.logical_or(keep, jnp.logical_and(hit, j < own))
        r = jnp.where(hit, neg, r)
    return jnp.where(keep, 0.0, NEG_BIG)


ROUTE_TILES = 8


def _moba_route_body(q_ref, eye_ref, km_ref, pen_ref, *, tq, n_blk):
    g = pl.program_id(2)
    per = MOBA_BLOCK // tq
    km = km_ref[0:n_blk, :]
    q = q_ref[...]
    lane = lax.broadcasted_iota(jnp.int32, (tq, LANES), 1)
    qf = jnp.concatenate(
        [jnp.where((lane < HEAD_DIM) if h == 0 else (lane >= HEAD_DIM), q[t * tq:(t + 1) * tq], 0.0)
         for t in range(ROUTE_TILES) for h in range(2)], axis=0)
    n_col = 2 * ROUTE_TILES * tq
    tile_of_col = lax.broadcasted_iota(jnp.int32, (1, n_col), 1) // (2 * tq)
    own = (g * ROUTE_TILES + tile_of_col) // per
    pen_t = _block_penalty_t(_dot3_nt(km, qf), own)
    pen_t = jnp.concatenate([pen_t, jnp.zeros((LANES - n_blk, n_col), F32)], axis=0).astype(BF16)
    for t in range(ROUTE_TILES):
        for h in range(2):
            c0 = (2 * t + h) * tq
            pen = _dot_nt(eye_ref[...], pen_t[:, c0:c0 + tq])
            pen_ref[t, h * tq:(h + 1) * tq, :] = pen.astype(BF16)


def moba_route(qa, kmean, cbq, bsz, t_len, tq):
    n_q = t_len // tq
    n_g = n_q // ROUTE_TILES
    n_hp = N_HEADS // 2
    n_blk = t_len // MOBA_BLOCK
    assert n_blk <= LANES and n_q % ROUTE_TILES == 0
    eye = jnp.asarray(np.eye(tq, dtype=np.float32), BF16)
    return pl.pallas_call(
        functools.partial(_moba_route_body, tq=tq, n_blk=n_blk),
        grid=(bsz, n_hp, n_g),
        in_specs=[
            pl.BlockSpec((ROUTE_TILES * tq, LANES), lambda b, p, g: (b * n_g + g, cbq + p)),
            pl.BlockSpec((tq, tq), lambda b, p, g: (0, 0)),
            pl.BlockSpec((None, LANES, LANES), lambda b, p, g: (b, 0, p)),
        ],
        out_specs=pl.BlockSpec((ROUTE_TILES, 2 * tq, LANES),
                               lambda b, p, g: ((b * n_hp + p) * n_g + g, 0, 0)),
        out_shape=jax.ShapeDtypeStruct((bsz * n_hp * n_q, 2 * tq, LANES), BF16),
        compiler_params=_cparams(("parallel", "parallel", "parallel")),
        name="moba_route",
    )(qa, eye, kmean)


def _moba_prompt_body(rb_ref, q_ref, pen_ref, k_ref, v_ref, e_ref, bias_ref, o_ref,
                      qa_ref, s_ref, mrun_ref, lrun_ref, acc_ref, *, tq, n_blk):
    p = pl.program_id(1)
    i = pl.program_id(2)
    per = MOBA_BLOCK // tq
    own = i // per
    par = i % per
    qa_ref[...] = jnp.concatenate(
        [_stack_heads(q_ref[...].astype(F32), HEAD_DIM ** -0.5), pen_ref[...]], axis=1)
    qa = qa_ref[...]

    row = lax.broadcasted_iota(jnp.int32, (2 * tq, MOBA_BLOCK), 0)
    col = lax.broadcasted_iota(jnp.int32, (2 * tq, MOBA_BLOCK), 1)
    causal = col <= row % tq + par * tq
    far_bias = jnp.where(row[:, 0:1] < tq, rb_ref[REL_BUCKETS - 1, 2 * p],
                         rb_ref[REL_BUCKETS - 1, 2 * p + 1])

    def logits(n, bias):
        n0 = pl.multiple_of(n * MOBA_BLOCK, MOBA_BLOCK)
        kaug = jnp.concatenate([k_ref[pl.ds(n0, MOBA_BLOCK), :], e_ref[pl.ds(n0, MOBA_BLOCK), :]],
                               axis=1)
        return _dot_nt(qa, kaug) + bias

    prev = jnp.maximum(own - 1, 0)
    prev_slot = jnp.where(own >= 1, own - 1, n_blk)
    n_far = prev

    def far_loop(step, group):
        def grouped(g, carry):
            step([group * g + j for j in range(group)])
            return carry

        lax.fori_loop(0, n_far // group, grouped, 0)
        base = (n_far // group) * group
        size = group // 2
        while size >= 1:
            take = (n_far - base) >= size

            @pl.when(take)
            def _(base=base, size=size):
                step([base + j for j in range(size)])

            base = base + jnp.where(take, size, 0)
            size //= 2

    def fold(ss, op):
        out = None
        for s in ss:
            h = op(s[:, :LANES], s[:, LANES:])
            out = h if out is None else op(out, h)
        return out

    s_own = jnp.where(causal, logits(own, bias_ref[par, 0].reshape(2 * tq, MOBA_BLOCK)), NEG_BIG)
    s_prev = jnp.where(own >= 1, logits(prev, bias_ref[par, 1].reshape(2 * tq, MOBA_BLOCK)), NEG_BIG)
    s_ref[own] = s_own
    s_ref[prev_slot] = s_prev
    mrun_ref[...] = fold([s_own, s_prev], jnp.maximum)

    def far_max(ns):
        ss = [logits(n, far_bias) for n in ns]
        for n, s in zip(ns, ss):
            s_ref[n] = s
        mrun_ref[...] = jnp.maximum(mrun_ref[...], fold(ss, jnp.maximum))

    far_loop(far_max, 16)
    m = jnp.max(mrun_ref[...], axis=-1, keepdims=True)

    def weigh(slots, ns):
        pes = [jnp.exp(s_ref[sl] - m) for sl in slots]
        o = None
        for pe, n in zip(pes, ns):
            n0 = pl.multiple_of(n * MOBA_BLOCK, MOBA_BLOCK)
            on = _dot(pe.astype(BF16), v_ref[pl.ds(n0, MOBA_BLOCK), :])
            o = on if o is None else o + on
        return fold(pes, jnp.add), o

    lrun_ref[...], acc_ref[...] = weigh([own, prev_slot], [own, prev])

    def far_acc(ns):
        l, o = weigh(ns, ns)
        lrun_ref[...] += l
        acc_ref[...] += o

    far_loop(far_acc, 8)
    o = acc_ref[...] / jnp.sum(lrun_ref[...], axis=-1, keepdims=True)
    o_ref[...] = _unstack_heads(o, tq).astype(o_ref.dtype)


def moba_prompt_bias_idx(tq):
    per = MOBA_BLOCK // tq
    t = np.arange(tq)[:, None]
    s = np.arange(MOBA_BLOCK)[None, :]
    idx = np.stack([np.stack([_rel_bucket_np(par * tq + kind * MOBA_BLOCK + t - s)
                              for kind in range(2)]) for par in range(per)])
    return idx.reshape(per * 2, tq, MOBA_BLOCK)


def moba_prompt(qa, ka, va, pen, rel_bias, bias, cbq, cbk, cbv, bsz, t_len, tq=128):
    n_q = t_len // tq
    n_hp = N_HEADS // 2
    per = MOBA_BLOCK // tq
    bias = bias.reshape(per, 2, N_HEADS, tq, MOBA_BLOCK)
    n_blk = t_len // MOBA_BLOCK
    blk_of_row = np.arange(t_len) // MOBA_BLOCK
    onehot = jnp.asarray(blk_of_row[:, None] == np.arange(LANES)[None, :], BF16)
    return pl.pallas_call(
        functools.partial(_moba_prompt_body, tq=tq, n_blk=n_blk),
        grid=(bsz, n_hp, n_q),
        in_specs=[
            pl.BlockSpec(memory_space=pltpu.SMEM),
            pl.BlockSpec((tq, LANES), lambda b, p, i: (b * n_q + i, cbq + p)),
            pl.BlockSpec((None, 2 * tq, LANES), lambda b, p, i: ((b * n_hp + p) * n_q + i, 0, 0)),
            pl.BlockSpec((t_len, LANES), lambda b, p, i: (b, cbk + p)),
            pl.BlockSpec((t_len, LANES), lambda b, p, i: (b, cbv + p)),
            pl.BlockSpec((t_len, LANES), lambda b, p, i: (0, 0)),
            pl.BlockSpec((per, 2, 2, tq, MOBA_BLOCK), lambda b, p, i: (0, 0, p, 0, 0)),
        ],
        out_specs=pl.BlockSpec((tq, LANES), lambda b, p, i: (b * n_q + i, p)),
        out_shape=jax.ShapeDtypeStruct((bsz * t_len, ATT_WIDTH), BF16),
        scratch_shapes=[pltpu.VMEM((2 * tq, 2 * LANES), BF16),
                        pltpu.VMEM((n_blk + 1, 2 * tq, MOBA_BLOCK), F32),
                        pltpu.VMEM((2 * tq, LANES), F32),
                        pltpu.VMEM((2 * tq, LANES), F32),
                        pltpu.VMEM((2 * tq, LANES), F32)],
        compiler_params=_cparams(("parallel", "parallel", "arbitrary")),
        name="moba_prompt",
    )(rel_bias, qa, pen, ka, va, onehot, bias)


def _ssd_body(xbc_ref, dt_ref, cprev_ref, h0_ref, cw_ref, cb_ref, dtb_ref, alog_ref, dsk_ref,
              e_ref, et_ref, eye_ref, tril_ref, y_ref, h_ref, xf_ref, dtf_ref, *, q):
    c = pl.program_id(1)
    t_in = xbc_ref.shape[0]
    t_out = y_ref.shape[0]
    hp2 = 2 * SSM_P
    gw = (SSM_HEADS // SSM_GROUPS) * SSM_P

    @pl.when(c == 0)
    def _():
        xf_ref[0:SUBLANES, :] = cprev_ref[...]
        h_ref[...] = h0_ref[...]

    if t_in < q:
        xf_ref[SUBLANES:SUBLANES + q, :] = jnp.zeros((q, CONV_DIM), F32)
        dtf_ref[...] = jnp.zeros_like(dtf_ref)
    xf_ref[SUBLANES:SUBLANES + t_in, :] = xbc_ref[...]
    dtf_ref[0:t_in, :] = dt_ref[...]
    base = SUBLANES - (CONV_W - 1)
    conv = cb_ref[...]
    for j in range(CONV_W):
        conv = conv + xf_ref[base + j:base + j + q, :] * cw_ref[j:j + 1, :]
    xf_ref[0:SUBLANES, :] = xf_ref[q:q + SUBLANES, :]
    act = conv * _sigmoid(conv)
    xs = act[:, :SSM_INNER]

    dt = _softplus(dtf_ref[...] + dtb_ref[...])
    if t_in < q:
        trow = lax.broadcasted_iota(jnp.int32, dt.shape, 0)
        dt = jnp.where(trow < t_in, dt, 0.0)
    a = dt * (-jnp.exp(alog_ref[...]))
    a_hi, a_lo = _split2(a)
    tril = tril_ref[...]
    acs = _dot(tril, a_hi) + _dot(tril, a_lo)
    acs_t = _dot2_nt(eye_ref[...], acs)
    e = e_ref[...]
    xr = xs * _dot2(dt, e)
    eacs_e = _dot2(jnp.exp(acs), e)
    xd = xr * _dot2(jnp.exp(acs[q - 1:q, :] - acs), e)
    xd_t = xd.T
    cd = jnp.broadcast_to(jnp.exp(acs_t[:, q - 1:q]), (LANES, LANES))
    cd_hi, cd_lo = _split2(cd)
    et = et_ref[...]
    f = _dot(et, cd_hi) + _dot(et, cd_lo)
    dsk_e = _dot2(jnp.broadcast_to(dsk_ref[...], (SUBLANES, LANES)), e)[0:1, :]

    row = lax.broadcasted_iota(jnp.int32, (q, q), 0)
    col = lax.broadcasted_iota(jnp.int32, (q, q), 1)
    causal = col <= row
    lane = lax.broadcasted_iota(jnp.int32, (q, hp2), 1)
    hpg = SSM_HEADS // SSM_GROUPS
    for g in range(SSM_GROUPS):
        bg = act[:, SSM_INNER + g * SSM_N:SSM_INNER + (g + 1) * SSM_N].astype(BF16)
        cg = act[:, SSM_INNER + (SSM_GROUPS + g) * SSM_N:
                 SSM_INNER + (SSM_GROUPS + g + 1) * SSM_N].astype(BF16)
        cbm = _dot_nt(cg, bg)
        hg = h_ref[g * hpg:(g + 1) * hpg].reshape(gw, SSM_N)
        y_off = _dot_nt(cg, hg.astype(BF16)) * eacs_e[:, g * gw:(g + 1) * gw]
        for pr in range(hpg // 2):
            yd = []
            xr_pair = xr[:, g * gw + pr * hp2:g * gw + (pr + 1) * hp2].astype(BF16)
            for hh in range(2):
                h = g * hpg + pr * 2 + hh
                seg = acs[:, h:h + 1] - acs_t[h:h + 1, :]
                m = (cbm * jnp.where(causal, jnp.exp(seg), 0.0)).astype(BF16)
                yd.append(_dot(m, xr_pair))
            y_pair = (jnp.where(lane < SSM_P, yd[0], yd[1]) + y_off[:, pr * hp2:(pr + 1) * hp2]
                      + xs[:, g * gw + pr * hp2:g * gw + (pr + 1) * hp2]
                      * dsk_e[:, g * gw + pr * hp2:g * gw + (pr + 1) * hp2])
            y_ref[:, g * gw + pr * hp2:g * gw + (pr + 1) * hp2] = y_pair[:t_out]
        st = _dot(xd_t[g * gw:(g + 1) * gw, :].astype(BF16), bg)
        h_ref[g * hpg:(g + 1) * hpg] = (hg * f[g * gw:(g + 1) * gw, :] + st).reshape(hpg, SSM_P, SSM_N)


def _ssd_consts(q):
    hidx = np.arange(SSM_INNER) // SSM_P
    e = (np.arange(LANES)[:, None] == hidx[None, :]).astype(np.float32)
    return (jnp.asarray(e, BF16), jnp.asarray(e.T, BF16),
            jnp.asarray(np.eye(LANES, dtype=np.float32), BF16),
            jnp.asarray(np.tril(np.ones((q, q), np.float32)), BF16))


def ssd(xa, cbx, dta, cprev, h0, state_layer, cw, cb, dtb, alog, dsk, layer, bsz, n_c, q=SSD_CHUNK):
    e, et, eye, tril = _ssd_consts(q)
    xblk = CONV_DIM // LANES
    t_in = xa.shape[1]
    t_out = q if t_in == q else SUBLANES
    const = lambda shape: pl.BlockSpec(shape, lambda b, c: (0,) * len(shape))
    par = lambda shape: pl.BlockSpec((None,) + shape, lambda b, c: (layer, 0, 0))
    return pl.pallas_call(
        functools.partial(_ssd_body, q=q),
        grid=(bsz, n_c),
        in_specs=[
            pl.BlockSpec((None, t_in, CONV_DIM), lambda b, c: (b * n_c + c, 0, cbx // xblk)),
            pl.BlockSpec((None, t_in, LANES), lambda b, c: (b * n_c + c, 0, 0)),
            pl.BlockSpec((None, None, SUBLANES, CONV_DIM), lambda b, c: (state_layer, b, 0, 0)),
            pl.BlockSpec((None, None, SSM_HEADS, SSM_P, SSM_N), lambda b, c: (state_layer, b, 0, 0, 0)),
            par((CONV_W, CONV_DIM)), par((1, CONV_DIM)),
            par((1, LANES)), par((1, LANES)), par((1, LANES)),
            const((LANES, SSM_INNER)), const((SSM_INNER, LANES)), const((LANES, LANES)),
            const((q, q)),
        ],
        out_specs=[
            pl.BlockSpec((None, t_out, SSM_INNER), lambda b, c: (b * n_c + c, 0, 0)),
            pl.BlockSpec((None, SSM_HEADS, SSM_P, SSM_N), lambda b, c: (b, 0, 0, 0)),
        ],
        out_shape=[jax.ShapeDtypeStruct((bsz * n_c, t_out, SSM_INNER), F32),
                   jax.ShapeDtypeStruct((bsz, SSM_HEADS, SSM_P, SSM_N), F32)],
        scratch_shapes=[pltpu.VMEM((q + SUBLANES, CONV_DIM), F32), pltpu.VMEM((q, LANES), F32)],
        compiler_params=_cparams(("parallel", "arbitrary")),
        name="ssd",
    )(xa, dta, cprev, h0, cw, cb, dtb, alog, dsk, e, et, eye, tril)


def _merge_body(x_ref, osb_ref, omb_ref, y_ref, z_ref, g0_ref, g1_ref, g2_ref, gs_ref,
                wsb_ref, wmb_ref, wss_ref, wo_ref, o_ref):
    z = z_ref[...]
    y = _rms(y_ref[...] * (z * _sigmoid(z)), gs_ref[...]).astype(BF16)
    merged = (_sigmoid(g0_ref[...]) * _dot(osb_ref[...], wsb_ref[...])
              + _sigmoid(g1_ref[...]) * _dot(omb_ref[...], wmb_ref[...])
              + _sigmoid(g2_ref[...]) * _dot(y, wss_ref[...]))
    o_ref[...] = x_ref[...] + _dot(merged.astype(BF16), wo_ref[...])


def merge(x, osb, omb, y, u, gs, wsb, wmb, wss, wo, layer, tm=512):
    n, d = x.shape
    db = d // LANES
    row = lambda cb, w: pl.BlockSpec((tm, w), lambda i: (i, cb))
    const = lambda a: pl.BlockSpec((None,) + a.shape[1:], lambda i: (layer, 0, 0))
    return pl.pallas_call(
        _merge_body,
        grid=(n // tm,),
        in_specs=[row(0, d), row(0, ATT_WIDTH), row(0, ATT_WIDTH), row(0, d),
                  row(CB_Z // db, d), row(CB_GATE // db, d), row(CB_GATE // db + 1, d),
                  row(CB_GATE // db + 2, d),
                  const(gs), const(wsb), const(wmb), const(wss), const(wo)],
        out_specs=row(0, d),
        out_shape=jax.ShapeDtypeStruct((n, d), F32),
        compiler_params=_cparams(("parallel",)),
        name="merge",
    )(x, osb, omb, y, u, u, u, u, gs, wsb, wmb, wss, wo)


def _router_body(x_ref, g_ref, wr_ref, xn_ref, cmb_ref):
    xn = _rms(x_ref[...], g_ref[...])
    xn_ref[...] = xn.astype(BF16)
    wr = wr_ref[...]
    xh, xl = _split2(xn)
    wh, wl = _split2(wr[:-SUBLANES, :])
    logit = _dot(xh, wh) + _dot(xh, wl) + _dot(xl, wh) + wr[-SUBLANES:-SUBLANES + 1, :]
    lane = lax.broadcasted_iota(jnp.int32, logit.shape, 1)
    neg = -jnp.inf
    is_g = jnp.logical_and(lane >= N_EXPERTS, lane < N_EXPERTS + N_GROUPS)
    gl = jnp.where(is_g, logit, neg)
    gmax = jnp.max(gl, axis=-1, keepdims=True)
    grp = jnp.min(jnp.where(gl == gmax, lane, LANES), axis=-1, keepdims=True) - N_EXPERTS
    p_grp = 1.0 / jnp.sum(jnp.exp(gl - gmax), axis=-1, keepdims=True)
    in_grp = jnp.logical_and(lane >= grp * EPG, lane < grp * EPG + EPG)
    el = jnp.where(in_grp, logit, neg)
    v1 = jnp.max(el, axis=-1, keepdims=True)
    i1 = jnp.min(jnp.where(el == v1, lane, LANES), axis=-1, keepdims=True)
    el2 = jnp.where(lane == i1, neg, el)
    v2 = jnp.max(el2, axis=-1, keepdims=True)
    i2 = jnp.min(jnp.where(el2 == v2, lane, LANES), axis=-1, keepdims=True)
    d = jnp.exp(v2 - v1)
    w1 = p_grp / (1.0 + d)
    w2 = p_grp * d / (1.0 + d)
    cmb_ref[...] = jnp.where(lane == i1, w1, 0.0) + jnp.where(lane == i2, w2, 0.0)


def router(x, g, wr, layer, tm=512):
    n, d = x.shape
    return pl.pallas_call(
        _router_body,
        grid=(n // tm,),
        in_specs=[pl.BlockSpec((tm, d), lambda i: (i, 0)),
                  pl.BlockSpec((None, 1, d), lambda i: (layer, 0, 0)),
                  pl.BlockSpec((None,) + wr.shape[1:], lambda i: (layer, 0, 0))],
        out_specs=[pl.BlockSpec((tm, d), lambda i: (i, 0)),
                   pl.BlockSpec((tm, LANES), lambda i: (i, 0))],
        out_shape=[jax.ShapeDtypeStruct((n, d), BF16),
                   jax.ShapeDtypeStruct((n, LANES), F32)],
        compiler_params=_cparams(("parallel",)),
        name="router",
    )(x, g, wr)


MOE_EXPERTS_PER_STEP = 8


def _moe_body(x_ref, xn_ref, cmb_ref, wgu_ref, wd_ref, gf_ref, o_ref, *, final_norm):
    g = pl.program_id(1)
    eps = MOE_EXPERTS_PER_STEP

    @pl.when(g == 0)
    def _():
        o_ref[...] = x_ref[...]

    cmb = cmb_ref[...]
    xn = xn_ref[...]
    lane = lax.broadcasted_iota(jnp.int32, cmb.shape, 1)
    hids = []
    for j in range(eps):
        c = jnp.sum(jnp.where(lane == g * eps + j, cmb, 0.0), axis=-1, keepdims=True)
        gu = _dot(xn, wgu_ref[j])
        gt = gu[:, :EXPERT_FF]
        hids.append((c * (gt * _sigmoid(gt) * gu[:, EXPERT_FF:])).astype(BF16))
    wd = wd_ref[...].reshape(eps * EXPERT_FF, wd_ref.shape[-1])
    o_ref[...] += _dot(jnp.concatenate(hids, axis=1), wd)

    if final_norm:
        @pl.when(g == pl.num_programs(1) - 1)
        def _():
            o_ref[...] = _rms(o_ref[...], gf_ref[...])


def moe(x, xn, cmb, wgu, wd, gf, layer, final_norm, tm=512):
    n, d = x.shape
    eps = MOE_EXPERTS_PER_STEP
    return pl.pallas_call(
        functools.partial(_moe_body, final_norm=final_norm),
        grid=(n // tm, N_EXPERTS // eps),
        in_specs=[pl.BlockSpec((tm, d), lambda i, e: (i, 0)),
                  pl.BlockSpec((tm, d), lambda i, e: (i, 0)),
                  pl.BlockSpec((tm, LANES), lambda i, e: (i, 0)),
                  pl.BlockSpec((None, eps, d, 2 * EXPERT_FF), lambda i, e: (layer, e, 0, 0)),
                  pl.BlockSpec((None, eps, EXPERT_FF, d), lambda i, e: (layer, e, 0, 0)),
                  pl.BlockSpec((1, d), lambda i, e: (0, 0))],
        out_specs=pl.BlockSpec((tm, d), lambda i, e: (i, 0)),
        out_shape=jax.ShapeDtypeStruct((n, d), F32),
        compiler_params=_cparams(("parallel", "arbitrary")),
        name="moe",
    )(x, xn, cmb, wgu, wd, gf)


T_PAD = SUBLANES
QROWS = N_HEADS * T_PAD


def _pad_rows(x_ref, buf):
    buf[0:T_PAD, :] = jnp.zeros((T_PAD, buf.shape[1]), buf.dtype)
    buf[0:x_ref.shape[0], :] = x_ref[...]
    return buf[0:T_PAD, :]


def _q_block_diag(q8):
    lane = lax.broadcasted_iota(jnp.int32, q8.shape, 1)
    return jnp.concatenate(
        [jnp.where(lane // HEAD_DIM == h, q8, 0.0) for h in range(N_HEADS)], axis=0)


def _out_from_acc(acc):
    lane = lax.broadcasted_iota(jnp.int32, (T_PAD, ATT_WIDTH), 1)
    out = jnp.zeros((T_PAD, ATT_WIDTH), F32)
    for h in range(N_HEADS):
        out = jnp.where(lane // HEAD_DIM == h, acc[h * T_PAD:(h + 1) * T_PAD, :], out)
    return out


N_AHEAD = 2


def _sb_sample_body(pt_ref, q_ref, kn_ref, vn_ref, ck_ref, cv_ref, tri_ref, o_ref,
                    kbuf, vbuf, acc_ref, car_ref, flag, kpg, vpg, kx, vx, sem_pg, sem_x,
                    *, n_pages, t_new, layer):
    b = pl.program_id(0)
    n_seq = pl.num_programs(0)
    slot = b % 2
    scale = HEAD_DIM ** -0.5
    qbd = (_q_block_diag(_pad_rows(q_ref, kbuf)) * scale).astype(BF16)
    tri = tri_ref[...]

    def ahead_copies(seq, sl):
        cps = []
        for j in range(N_AHEAD):
            pg = pt_ref[seq, n_pages - 1 - j]
            cps.append(pltpu.make_async_copy(ck_ref.at[layer, pg], kpg.at[sl, j], sem_pg.at[sl, j, 0]))
            cps.append(pltpu.make_async_copy(cv_ref.at[layer, pg], vpg.at[sl, j], sem_pg.at[sl, j, 1]))
        return cps

    def demand_copies(p):
        pg = pt_ref[b, p]
        return [pltpu.make_async_copy(ck_ref.at[layer, pg], kx, sem_x.at[0]),
                pltpu.make_async_copy(cv_ref.at[layer, pg], vx, sem_x.at[1])]

    @pl.when(b == 0)
    def _():
        for cp in ahead_copies(0, 0):
            cp.start()

    @pl.when(b + 1 < n_seq)
    def _():
        for cp in ahead_copies(b + 1, 1 - slot):
            cp.start()

    def page(z, weigh, mask):
        lk = -_softplus(z)
        if mask is not None:
            lk = jnp.where(mask, lk, 0.0)
        sums = _suffix_sums(lk, tri)
        w = jnp.exp(z + sums[:, :PAGE] + car_ref[...])
        if mask is not None:
            w = jnp.where(mask, w, 0.0)
        acc_ref[...] += weigh(w.astype(BF16))
        car = car_ref[...] + sums[:, PAGE:]
        car_ref[...] = car
        flag[0] = (jnp.max(car) >= EXP_ZERO).astype(jnp.int32)

    kbuf[...] = jnp.zeros_like(kbuf)
    vbuf[...] = jnp.zeros_like(vbuf)
    kbuf[0:t_new, :] = kn_ref[...]
    vbuf[0:t_new, :] = vn_ref[...]
    acc_ref[...] = jnp.zeros_like(acc_ref)
    car_ref[...] = jnp.zeros_like(car_ref)
    r = lax.broadcasted_iota(jnp.int32, (QROWS, PAGE), 0)
    c = lax.broadcasted_iota(jnp.int32, (QROWS, PAGE), 1)
    past = jnp.logical_and(c < r % T_PAD, c < t_new)
    page(_dot_nt(qbd, kbuf[...].astype(BF16)), lambda w: _dot(w, vbuf[...].astype(BF16)), past)

    def cache_page(k_page, v_page):
        vt = v_page.reshape(ATT_WIDTH, PAGE).astype(BF16)
        page(_dot(qbd, k_page.reshape(ATT_WIDTH, PAGE).astype(BF16)), lambda w: _dot_nt(w, vt), None)

    for cp in ahead_copies(b, slot):
        cp.wait()
    for j in range(N_AHEAD):
        @pl.when(flag[0] > 0)
        def _():
            cache_page(kpg[slot, j], vpg[slot, j])

    for p in reversed(range(n_pages - N_AHEAD)):
        @pl.when(flag[0] > 0)
        def _():
            cps = demand_copies(p)
            for cp in cps:
                cp.start()
            for cp in cps:
                cp.wait()
            cache_page(kx[...], vx[...])

    o_ref[...] = _out_from_acc(acc_ref[...]).astype(o_ref.dtype)


def _page_specs(layer, n_pages):
    return [pl.BlockSpec((None, None, N_HEADS, HEAD_DIM, PAGE),
                         functools.partial(lambda b, pt, p: (layer, pt[b, p], 0, 0, 0), p=p))
            for p in range(n_pages)]


def sb_sample(page_table, us8, cache_k, cache_v, layer, t_new):
    n_seq, n_pages = page_table.shape
    assert n_pages >= N_AHEAD
    tri = _tri_ge_ones(PAGE)
    tok = lambda cb: pl.BlockSpec((None, t_new, ATT_WIDTH), lambda b, pt: (b, 0, cb // 4))
    const = lambda a: pl.BlockSpec(a.shape, lambda b, pt: (0,) * a.ndim)
    page_shape = (N_HEADS, HEAD_DIM, PAGE)
    grid_spec = pltpu.PrefetchScalarGridSpec(
        num_scalar_prefetch=1,
        grid=(n_seq,),
        in_specs=[tok(CB_Q_SB), tok(CB_K_SB), tok(CB_V_SB),
                  pl.BlockSpec(memory_space=pl.ANY), pl.BlockSpec(memory_space=pl.ANY),
                  const(tri)],
        out_specs=pl.BlockSpec((None, T_PAD, ATT_WIDTH), lambda b, pt: (b, 0, 0)),
        scratch_shapes=[pltpu.VMEM((PAGE, ATT_WIDTH), F32), pltpu.VMEM((PAGE, ATT_WIDTH), F32),
                        pltpu.VMEM((QROWS, ATT_WIDTH), F32), pltpu.VMEM((QROWS, LANES), F32),
                        pltpu.SMEM((1,), jnp.int32),
                        pltpu.VMEM((2, N_AHEAD) + page_shape, F32),
                        pltpu.VMEM((2, N_AHEAD) + page_shape, F32),
                        pltpu.VMEM(page_shape, F32), pltpu.VMEM(page_shape, F32),
                        pltpu.SemaphoreType.DMA((2, N_AHEAD, 2)), pltpu.SemaphoreType.DMA((2,))],
    )
    return pl.pallas_call(
        functools.partial(_sb_sample_body, n_pages=n_pages, t_new=t_new, layer=layer),
        grid_spec=grid_spec,
        out_shape=jax.ShapeDtypeStruct((n_seq, T_PAD, ATT_WIDTH), BF16),
        compiler_params=_cparams(("arbitrary",)),
        name="sb_sample",
    )(page_table, us8, us8, us8, cache_k, cache_v, tri)


def _moba_sample_body(pt_ref, q_ref, kn_ref, vn_ref, *rest, n_pages, t_new):
    k_refs = rest[:n_pages]
    v_refs = rest[n_pages:2 * n_pages]
    bias_ref, o_ref, kbuf, vbuf, s_ref = rest[2 * n_pages:]
    ppb = MOBA_BLOCK // PAGE
    n_blk = n_pages // ppb
    scale = HEAD_DIM ** -0.5
    qf = _q_block_diag(_pad_rows(q_ref, kbuf))
    qs = (qf * scale).astype(BF16)
    neg = -jnp.inf
    lane = lax.broadcasted_iota(jnp.int32, (QROWS, LANES), 1)
    lane_w = lax.broadcasted_iota(jnp.int32, (ATT_WIDTH, LANES), 1)

    ksum = jnp.zeros((ATT_WIDTH, LANES), F32)
    for n in range(n_blk):
        blk = k_refs[ppb * n][...].reshape(ATT_WIDTH, PAGE)
        for j in range(1, ppb):
            blk = blk + k_refs[ppb * n + j][...].reshape(ATT_WIDTH, PAGE)
        ksum = jnp.where(lane_w == n, jnp.sum(blk, axis=-1, keepdims=True), ksum)
    kmean = ksum * (1.0 / MOBA_BLOCK)
    qh, ql = _split2(qf)
    mh, ml = _split2(kmean)
    route = _dot(qh, mh) + _dot(qh, ml) + _dot(ql, mh)
    sel = _route_topk(route, lane, n_blk, n_blk)

    mrun = jnp.full((QROWS, PAGE), neg, F32)
    for p in range(n_pages):
        s = _dot(qs, k_refs[p][...].reshape(ATT_WIDTH, PAGE).astype(BF16)) + bias_ref[p]
        s = jnp.where(sel[:, p // ppb:p // ppb + 1] > 0.5, s, neg)
        s_ref[p] = s
        mrun = jnp.maximum(mrun, s)
    kbuf[...] = jnp.zeros_like(kbuf)
    vbuf[...] = jnp.zeros_like(vbuf)
    kbuf[0:t_new, :] = kn_ref[...]
    vbuf[0:t_new, :] = vn_ref[...]
    r = lax.broadcasted_iota(jnp.int32, (QROWS, PAGE), 0)
    c = lax.broadcasted_iota(jnp.int32, (QROWS, PAGE), 1)
    causal = jnp.logical_and(c <= r % T_PAD, c < T_PAD)
    s = _dot_nt(qs, kbuf[...].astype(BF16)) + bias_ref[n_pages]
    s = jnp.where(causal, s, neg)
    s_ref[n_pages] = s
    m = jnp.max(jnp.maximum(mrun, s), axis=-1, keepdims=True)

    lsum = jnp.zeros((QROWS, PAGE), F32)
    acc = jnp.zeros((QROWS, ATT_WIDTH), F32)
    for p in range(n_pages + 1):
        pe = jnp.exp(s_ref[p] - m)
        lsum = lsum + pe
        pe = pe.astype(BF16)
        if p < n_pages:
            acc = acc + _dot_nt(pe, v_refs[p][...].reshape(ATT_WIDTH, PAGE).astype(BF16))
        else:
            acc = acc + _dot(pe, vbuf[...].astype(BF16))
    o_ref[...] = _out_from_acc(acc / jnp.sum(lsum, axis=-1, keepdims=True)).astype(o_ref.dtype)


def moba_sample_bias_idx(n_pages):
    t = np.arange(T_PAD)[:, None]
    s = np.arange(PAGE)[None, :]
    past = [_rel_bucket_np((n_pages - p) * PAGE + t - s) for p in range(n_pages)]
    return np.stack(past + [_rel_bucket_np(t - s)])


def moba_sample(page_table, us8, cache_k, cache_v, bias, layer, t_new):
    n_seq, n_pages = page_table.shape
    tok = lambda cb: pl.BlockSpec((None, t_new, ATT_WIDTH), lambda b, pt: (b, 0, cb // 4))
    const = lambda a: pl.BlockSpec(a.shape, lambda b, pt: (0,) * a.ndim)
    grid_spec = pltpu.PrefetchScalarGridSpec(
        num_scalar_prefetch=1,
        grid=(n_seq,),
        in_specs=[tok(CB_Q_MB), tok(CB_K_MB), tok(CB_V_MB)]
        + _page_specs(layer, n_pages) + _page_specs(layer, n_pages)
        + [const(bias)],
        out_specs=pl.BlockSpec((None, T_PAD, ATT_WIDTH), lambda b, pt: (b, 0, 0)),
        scratch_shapes=[pltpu.VMEM((PAGE, ATT_WIDTH), F32), pltpu.VMEM((PAGE, ATT_WIDTH), F32),
                        pltpu.VMEM((n_pages + 1, QROWS, PAGE), F32)],
    )
    return pl.pallas_call(
        functools.partial(_moba_sample_body, n_pages=n_pages, t_new=t_new),
        grid_spec=grid_spec,
        out_shape=jax.ShapeDtypeStruct((n_seq, T_PAD, ATT_WIDTH), BF16),
        compiler_params=_cparams(("arbitrary",)),
        name="moba_sample",
    )(page_table, us8, us8, us8, *([cache_k] * n_pages), *([cache_v] * n_pages), bias)


def kernel(x_prompt, x_sample, cache_k_sb, cache_v_sb, cache_k_moba, cache_v_moba, page_table, state_ssm, state_conv, rel_bias, g_mix, w_in, conv_w, conv_b, dt_bias, a_log, d_skip, g_ssm, w_branch_sb, w_branch_moba, w_branch_ssm, w_out, g_ffn, w_router_group, b_router_group, w_router_expert, b_router_expert, w_expert_gate, w_expert_up, w_expert_down, g_final):
    bsz, seq, d = x_prompt.shape
    n_seq, t_new, _ = x_sample.shape
    depth = w_in.shape[0]
    n_pages = page_table.shape[1]
    n_p = bsz * seq
    n_s = n_seq * t_new
    assert d == D_MODEL and seq % MOBA_BLOCK == 0 and t_new <= T_PAD
    assert (n_pages * PAGE) % MOBA_BLOCK == 0 and cache_k_sb.shape[2] == PAGE

    x_p = x_prompt.reshape(n_p, d)
    x_s = x_sample.reshape(n_s, d)
    tm_p = 1024 if n_p % 1024 == 0 else 512
    tm_s = min(n_s, 512)
    n_c = seq // SSD_CHUNK
    pages = [c.transpose(0, 1, 3, 4, 2) for c in (cache_k_sb, cache_v_sb, cache_k_moba, cache_v_moba)]
    tq_mb = MOBA_BLOCK
    bias_p = bias_tiles(rel_bias, jnp.asarray(moba_prompt_bias_idx(tq_mb)))
    bias_s = bias_tiles(rel_bias, jnp.asarray(moba_sample_bias_idx(n_pages))).reshape(
        n_pages + 1, QROWS, PAGE)
    zero_conv = jnp.zeros((1, bsz, SUBLANES, CONV_DIM), F32)
    zero_state = jnp.zeros((1, bsz, SSM_HEADS, SSM_P, SSM_N), F32)
    n_main = CB_XBC * LANES + CONV_DIM

    def col(a, cb, w):
        return a[..., cb * LANES:cb * LANES + w]

    def kv(a, cb):
        return col(a, cb, ATT_WIDTH).reshape(a.shape[0], a.shape[1], N_HEADS, HEAD_DIM)

    def lane_pad(v):
        return jnp.pad(v, ((0, 0), (0, LANES - v.shape[1])))[:, None, :]

    w_main = jnp.concatenate([w_in[:, :, :n_main], w_in[:, :, n_main + SSM_HEADS:]], axis=2).astype(BF16)
    w_dt = jnp.pad(w_in[:, :, n_main:n_main + SSM_HEADS],
                   ((0, 0), (0, 0), (0, LANES - SSM_HEADS))).astype(BF16)
    gm, gs, gf = g_mix[:, None, :], g_ssm[:, None, :], g_ffn[:, None, :]
    ssm_par = (conv_w, conv_b[:, None, :], lane_pad(dt_bias), lane_pad(a_log), lane_pad(d_skip))
    lane_fill = LANES - N_EXPERTS - N_GROUPS
    wr = jnp.concatenate([
        jnp.pad(jnp.concatenate([w_router_expert, w_router_group], axis=2),
                ((0, 0), (0, 0), (0, lane_fill))),
        jnp.pad(jnp.concatenate([b_router_expert, b_router_group], axis=1)[:, None, :],
                ((0, 0), (0, SUBLANES - 1), (0, lane_fill)))], axis=1)
    wgu = jnp.concatenate([w_expert_gate, w_expert_up], axis=-1).astype(BF16)
    wd = w_expert_down.astype(BF16)
    w_br = (w_branch_sb.astype(BF16), w_branch_moba.astype(BF16),
            w_branch_ssm.astype(BF16), w_out.astype(BF16))
    cprev_s = jnp.pad(state_conv, ((0, 0), (0, 0), (SUBLANES - (CONV_W - 1), 0), (0, 0)))
    g_fin = g_final.reshape(1, d)

    new_p, new_s = [], []
    for l in range(depth):
        def tail(x, osb, omb, y, u, tm):
            x1 = merge(x, osb, omb, y, u, gs, *w_br, l, tm=min(tm, 512))
            xn, cmb = router(x1, gf, wr, l, tm=min(tm, 512))
            return moe(x1, xn, cmb, wgu, wd, g_fin, l, final_norm=(l == depth - 1), tm=tm)

        u_p, dt_p, qkv_p = in_proj(x_p, gm, w_main, w_dt, l, tm=tm_p)
        osb_p = sb_prompt(qkv_p, qkv_p, qkv_p, CB_Q_SB, CB_K_SB, CB_V_SB, bsz, seq)
        km = kmean_prompt(u_p, CB_K_MB, bsz, seq)
        pen = moba_route(u_p, km, CB_Q_MB, bsz, seq, tq_mb)
        omb_p = moba_prompt(qkv_p, qkv_p, qkv_p, pen, rel_bias, bias_p, CB_Q_MB, CB_K_MB, CB_V_MB,
                            bsz, seq, tq=tq_mb)
        y_p, ssm_p = ssd(u_p.reshape(bsz * n_c, SSD_CHUNK, U_COLS), CB_XBC,
                         dt_p.reshape(bsz * n_c, SSD_CHUNK, LANES), zero_conv, zero_state, 0,
                         *ssm_par, l, bsz=bsz, n_c=n_c)
        x_p = tail(x_p, osb_p, omb_p, y_p.reshape(n_p, SSM_INNER), u_p, tm_p)

        u_s, dt_s, _ = in_proj(x_s, gm, w_main, w_dt, l, tm=tm_s)
        u_s3 = u_s.reshape(n_seq, t_new, U_COLS)
        osb_s = sb_sample(page_table, u_s3, pages[0], pages[1], l, t_new)[:, :t_new]
        omb_s = moba_sample(page_table, u_s3, pages[2], pages[3], bias_s, l, t_new)[:, :t_new]
        y_s, ssm_s = ssd(u_s3, CB_XBC, dt_s.reshape(n_seq, t_new, LANES), cprev_s, state_ssm, l,
                         *ssm_par, l, bsz=n_seq, n_c=1, q=T_PAD)
        x_s = tail(x_s, osb_s.reshape(n_s, ATT_WIDTH), omb_s.reshape(n_s, ATT_WIDTH),
                   y_s[:, :t_new].reshape(n_s, SSM_INNER), u_s, tm_s)

        u_p3 = u_p.reshape(bsz, seq, U_COLS)
        new_p.append((kv(u_p3, CB_K_SB), kv(u_p3, CB_V_SB), kv(u_p3, CB_K_MB), kv(u_p3, CB_V_MB),
                      ssm_p, col(u_p3, CB_XBC, CONV_DIM)[:, seq - (CONV_W - 1):]))
        conv_s = jnp.concatenate([state_conv[l], col(u_s3, CB_XBC, CONV_DIM)], axis=1)[:, t_new:]
        new_s.append((kv(u_s3, CB_K_SB), kv(u_s3, CB_V_SB), kv(u_s3, CB_K_MB), kv(u_s3, CB_V_MB),
                      ssm_s, conv_s))

    stack = lambda states: tuple(jnp.stack([s[i] for s in states]) for i in range(6))
    return ((x_p.reshape(bsz, seq, d), x_s.reshape(n_seq, t_new, d)) + stack(new_p) + stack(new_s))
```
